```python
import jax, jax.numpy as jnp
from jax import lax
import numpy as np

D_MODEL = 1024
BATCH = 8
SEQ = 2048
DEPTH = 2

HEAD_DIM = 64
ROPE_THETA = 10000.0
Q_BLOCK = 128
LN_EPS = 1e-5
RMS_EPS = 1e-6

NSA_HEADS = 8
NSA_KV_HEADS = 2
NSA_CMP_LEN = 32
NSA_CMP_STRIDE = 16
NSA_CMP_HIDDEN = 128
NSA_SEL_BLOCK = 64
NSA_SEL_TOPN = 8
NSA_WINDOW = 512

MLA_HEADS = 8
MLA_Q_RANK = 256
MLA_KV_RANK = 128
MLA_NOPE_DIM = 64
MLA_ROPE_DIM = 32
MLA_V_DIM = 64

MOBA_HEADS = 16
MOBA_BLOCK = 256
MOBA_TOPK = 3
MOBA_Q_CHUNK = 32

N_EXPERTS = 64
N_GROUPS = 8
TOPK_GROUPS = 4
TOP_K = 8
EXPERT_FF = 256
SHARED_FF = 256
ROUTED_SCALE = 2.5
MOE_ROW_BLOCK = 128

DN_ALPHA = (2 * DEPTH) ** 0.25
DN_BETA = (8 * DEPTH) ** -0.25

NSA_Q_W = NSA_HEADS * HEAD_DIM
NSA_KV_W = 6 * NSA_KV_HEADS * HEAD_DIM
NSA_GATE_W = 3 * NSA_HEADS
SPLIT0 = (NSA_Q_W,
          NSA_Q_W + NSA_KV_W,
          NSA_Q_W + NSA_KV_W + NSA_GATE_W,
          NSA_Q_W + NSA_KV_W + NSA_GATE_W + MLA_Q_RANK,
          NSA_Q_W + NSA_KV_W + NSA_GATE_W + MLA_Q_RANK + MLA_KV_RANK)
IN_W0 = NSA_Q_W + NSA_KV_W + NSA_GATE_W + MLA_Q_RANK + MLA_KV_RANK + MLA_ROPE_DIM
MIX_W0 = NSA_HEADS * HEAD_DIM + MLA_HEADS * MLA_V_DIM

kernel_name = "hybrid_nsa_mla_moba_moe_deepnorm"


def layer_norm(x, g, b):
    xf = x.astype(jnp.float32)
    mu = jnp.mean(xf, -1, keepdims=True)
    var = jnp.mean(jnp.square(xf - mu), -1, keepdims=True)
    y = (xf - mu) * lax.rsqrt(var + LN_EPS) * g.astype(jnp.float32) + b.astype(jnp.float32)
    return y.astype(x.dtype)


def rms_norm(x, g):
    xf = x.astype(jnp.float32)
    y = xf * lax.rsqrt(jnp.mean(jnp.square(xf), -1, keepdims=True) + RMS_EPS)
    return (y * g.astype(jnp.float32)).astype(x.dtype)


def rope(x, pos):
    half = x.shape[-1] // 2
    inv_freq = ROPE_THETA ** (-jnp.arange(half, dtype=jnp.float32) / half)
    ang = pos.astype(jnp.float32)[:, None] * inv_freq[None, :]
    cos = jnp.cos(ang).astype(x.dtype)
    sin = jnp.sin(ang).astype(x.dtype)
    x1, x2 = x[..., :half], x[..., half:]
    return jnp.concatenate([x1 * cos - x2 * sin, x2 * cos + x1 * sin], axis=-1)


def masked_softmax(s, mask):
    s = jnp.where(mask, s.astype(jnp.float32), -jnp.inf)
    m = jnp.max(s, axis=-1, keepdims=True)
    e = jnp.exp(s - jnp.where(jnp.isfinite(m), m, 0.0))
    den = jnp.sum(e, axis=-1, keepdims=True)
    return e / jnp.where(den > 0, den, 1.0)


def seq_blocks(t, axis, size):
    n = t.shape[axis] // size
    t = t.reshape(t.shape[:axis] + (n, size) + t.shape[axis + 1:])
    return jnp.moveaxis(t, axis, 0)


def merge_blocks(t, axis):
    t = jnp.moveaxis(t, 0, axis)
    return t.reshape(t.shape[:axis] + (t.shape[axis] * t.shape[axis + 1],) + t.shape[axis + 2:])


def nsa_attention(q, kv, gate_logits, pe_k, pe_v, cmp_k1, cmp_k2, cmp_v1, cmp_v2):
    B, S, _ = q.shape
    H, G, dh = NSA_HEADS, NSA_KV_HEADS, HEAD_DIM
    R = H // G
    scale = dh ** -0.5
    pos = jnp.arange(S)
    qh = rope(q.reshape(B, S, H, dh).transpose(0, 2, 1, 3), pos).reshape(B, G, R, S, dh)
    kc, vc, ks, vs, kw, vw = [t.reshape(B, S, G, dh).transpose(0, 2, 1, 3)
                              for t in jnp.split(kv, 6, axis=-1)]

    n_cmp = (S - NSA_CMP_LEN) // NSA_CMP_STRIDE + 1
    cmp_tok = np.arange(n_cmp)[:, None] * NSA_CMP_STRIDE + np.arange(NSA_CMP_LEN)[None, :]
    cmp_end = jnp.asarray(cmp_tok[:, -1])

    def compress(t, pe, w1, w2):
        blocks = (t[:, :, cmp_tok] + pe).reshape(B, G, n_cmp, NSA_CMP_LEN * dh)
        return jax.nn.gelu(blocks @ w1) @ w2

    k_cmp = rope(compress(kc, pe_k, cmp_k1, cmp_k2), cmp_end)
    v_cmp = compress(vc, pe_v, cmp_v1, cmp_v2)
    s_cmp = jnp.einsum('bgrsd,bgcd->bgrsc', qh, k_cmp) * scale
    p_cmp = masked_softmax(s_cmp, cmp_end[None, :] <= pos[:, None])
    o_cmp = jnp.einsum('bgrsc,bgcd->bgrsd', p_cmp.astype(vc.dtype), v_cmp)

    Ls = NSA_SEL_BLOCK
    n_sel = S // Ls
    cmp_to_sel = (cmp_tok[:, :, None] // Ls == np.arange(n_sel)[None, None, :]).sum(1) / NSA_CMP_LEN
    imp = jnp.einsum('bgrsc,cj->bgsj', p_cmp, jnp.asarray(cmp_to_sel, dtype=jnp.float32))
    blk = jnp.arange(n_sel)[None, :]
    cur = (pos // Ls)[:, None]
    forced = (blk == 0) | (blk == cur) | (blk == cur - 1)
    imp = jnp.where(blk <= cur, jnp.where(forced, jnp.inf, imp), -jnp.inf)
    n_top = min(NSA_SEL_TOPN, n_sel)
    sel_val, sel_idx = lax.top_k(imp, n_top)
    sel_ok = sel_val > -jnp.inf
    ks_blk = rope(ks, pos).reshape(B, G, n_sel, Ls, dh)
    vs_blk = vs.reshape(B, G, n_sel, Ls, dh)
    bi = jnp.arange(B)[:, None, None, None]
    gi = jnp.arange(G)[None, :, None, None]

    def sel_block(args):
        qb, idx, ok, qpos = args
        qn = qb.shape[3]
        kg = ks_blk[bi, gi, idx].reshape(B, G, qn, n_top * Ls, dh)
        vg = vs_blk[bi, gi, idx].reshape(B, G, qn, n_top * Ls, dh)
        s = jnp.einsum('bgrqd,bgqkd->bgrqk', qb, kg) * scale
        kpos = idx[..., None] * Ls + jnp.arange(Ls)
        mask = (ok[..., None] & (kpos <= qpos[:, None, None])).reshape(B, G, 1, qn, n_top * Ls)
        p = masked_softmax(s, mask)
        return jnp.einsum('bgrqk,bgqkd->bgrqd', p.astype(vg.dtype), vg)

    o_sel = lax.map(sel_block, (seq_blocks(qh, 3, Q_BLOCK), seq_blocks(sel_idx, 2, Q_BLOCK),
                                seq_blocks(sel_ok, 2, Q_BLOCK), pos.reshape(-1, Q_BLOCK)))
    o_sel = merge_blocks(o_sel, 3)

    nqb = S // Q_BLOCK
    nw = NSA_WINDOW // Q_BLOCK
    band_idx = np.arange(nqb)[:, None] + np.arange(nw + 1)[None, :]

    def band(t):
        tp = jnp.pad(t, ((0, 0), (0, 0), (NSA_WINDOW, 0), (0, 0))).reshape(B, G, nqb + nw, Q_BLOCK, dh)
        return tp[:, :, band_idx].reshape(B, G, nqb, (nw + 1) * Q_BLOCK, dh)

    k_band = band(rope(kw, pos))
    v_band = band(vw)
    s_w = jnp.einsum('bgrnqd,bgnkd->bgrnqk', qh.reshape(B, G, R, nqb, Q_BLOCK, dh), k_band) * scale
    qpos = pos.reshape(nqb, Q_BLOCK)
    kpos = (jnp.arange(nqb)[:, None] - nw) * Q_BLOCK + jnp.arange((nw + 1) * Q_BLOCK)[None, :]
    diff = qpos[:, :, None] - kpos[:, None, :]
    mask_w = (diff >= 0) & (diff < NSA_WINDOW) & (kpos[:, None, :] >= 0)
    p_w = masked_softmax(s_w, mask_w)
    o_win = jnp.einsum('bgrnqk,bgnkd->bgrnqd', p_w.astype(vw.dtype), v_band).reshape(B, G, R, S, dh)

    g = jax.nn.sigmoid(gate_logits.astype(jnp.float32)).astype(q.dtype)
    g = g.reshape(B, S, 3, G, R).transpose(2, 0, 3, 4, 1)[..., None]
    o = g[0] * o_cmp + g[1] * o_sel + g[2] * o_win
    return o.reshape(B, H, S, dh).transpose(0, 2, 1, 3).reshape(B, S, H * dh)


def causal_attention_blocks(q, k, v, scale):
    S = q.shape[2]
    kpos = jnp.arange(S)

    def one(args):
        qb, qp = args
        s = jnp.einsum('bhqd,bhkd->bhqk', qb, k) * scale
        p = masked_softmax(s, kpos[None, :] <= qp[:, None])
        return jnp.einsum('bhqk,bhkd->bhqd', p.astype(v.dtype), v)

    o = lax.map(one, (seq_blocks(q, 2, Q_BLOCK), kpos.reshape(-1, Q_BLOCK)))
    return merge_blocks(o, 2)


def mla_attention(c_q, c_kv, k_rope, q_norm_g, w_uq, kv_norm_g, w_ukv):
    B, S, _ = c_q.shape
    H = MLA_HEADS
    pos = jnp.arange(S)
    q = (rms_norm(c_q, q_norm_g) @ w_uq).reshape(B, S, H, MLA_NOPE_DIM + MLA_ROPE_DIM).transpose(0, 2, 1, 3)
    kv = (rms_norm(c_kv, kv_norm_g) @ w_ukv).reshape(B, S, H, MLA_NOPE_DIM + MLA_V_DIM).transpose(0, 2, 1, 3)
    q_full = jnp.concatenate([q[..., :MLA_NOPE_DIM], rope(q[..., MLA_NOPE_DIM:], pos)], axis=-1)
    k_pe = jnp.broadcast_to(rope(k_rope[:, None], pos), (B, H, S, MLA_ROPE_DIM))
    k_full = jnp.concatenate([kv[..., :MLA_NOPE_DIM], k_pe], axis=-1)
    v = kv[..., MLA_NOPE_DIM:]
    o = causal_attention_blocks(q_full, k_full, v, (MLA_NOPE_DIM + MLA_ROPE_DIM) ** -0.5)
    return o.transpose(0, 2, 1, 3).reshape(B, S, H * MLA_V_DIM)


def sparse_latent_mixer(x, w_in, pe_k, pe_v, cmp_k1, cmp_k2, cmp_v1, cmp_v2,
                        q_norm_g, w_uq, kv_norm_g, w_ukv, w_out):
    q_nsa, kv_nsa, gates, c_q, c_kv, k_rope = jnp.split(x @ w_in, SPLIT0, axis=-1)
    o_nsa = nsa_attention(q_nsa, kv_nsa, gates, pe_k, pe_v, cmp_k1, cmp_k2, cmp_v1, cmp_v2)
    o_mla = mla_attention(c_q, c_kv, k_rope, q_norm_g, w_uq, kv_norm_g, w_ukv)
    return jnp.concatenate([o_nsa, o_mla], axis=-1) @ w_out


def moba_attention(q, k, v):
    B, H, S, dh = q.shape
    Lb = MOBA_BLOCK
    nb = -(-S // Lb)
    pad = ((0, 0), (0, 0), (0, nb * Lb - S), (0, 0))
    kb = jnp.pad(k, pad).reshape(B, H, nb, Lb, dh)
    vb = jnp.pad(v, pad).reshape(B, H, nb, Lb, dh)
    scale = dh ** -0.5
    pos = jnp.arange(S)
    n_top = min(MOBA_TOPK, max(nb - 1, 1))
    k_mean = jnp.mean(kb.astype(jnp.float32), axis=3).astype(k.dtype)
    gate = jnp.einsum('bhsd,bhnd->bhsn', q, k_mean).astype(jnp.float32)
    fully_past = jnp.arange(nb)[None, :] < (pos // Lb)[:, None]
    g_val, g_idx = lax.top_k(jnp.where(fully_past, gate, -jnp.inf), n_top)
    g_ok = g_val > -jnp.inf
    bi = jnp.arange(B)[:, None, None, None]
    hi = jnp.arange(H)[None, :, None, None]
    own_off = jnp.arange(Lb)

    def chunk(args):
        qb, qp, idx, ok = args
        qc = qb.shape[2]
        own = qp[0] // Lb
        k_own = lax.dynamic_index_in_dim(kb, own, axis=2, keepdims=False)
        v_own = lax.dynamic_index_in_dim(vb, own, axis=2, keepdims=False)
        kg = kb[bi, hi, idx].reshape(B, H, qc, n_top * Lb, dh)
        vg = vb[bi, hi, idx].reshape(B, H, qc, n_top * Lb, dh)
        s_past = jnp.einsum('bhqd,bhqkd->bhqk', qb, kg) * scale
        s_own = jnp.einsum('bhqd,bhkd->bhqk', qb, k_own) * scale
        m_own = jnp.broadcast_to((own * Lb + own_off)[None, :] <= qp[:, None], (B, H, qc, Lb))
        mask = jnp.concatenate([jnp.repeat(ok, Lb, axis=-1), m_own], axis=-1)
        p = masked_softmax(jnp.concatenate([s_past, s_own], axis=-1), mask).astype(v.dtype)
        return (jnp.einsum('bhqk,bhqkd->bhqd', p[..., :n_top * Lb], vg)
                + jnp.einsum('bhqk,bhkd->bhqd', p[..., n_top * Lb:], v_own))

    o = lax.map(chunk, (seq_blocks(q, 2, MOBA_Q_CHUNK), pos.reshape(-1, MOBA_Q_CHUNK),
                        seq_blocks(g_idx, 2, MOBA_Q_CHUNK), seq_blocks(g_ok, 2, MOBA_Q_CHUNK)))
    return merge_blocks(o, 2)


def moba_mixer(x, w_qkv, w_out):
    B, S, _ = x.shape
    pos = jnp.arange(S)
    q, k, v = [t.reshape(B, S, MOBA_HEADS, HEAD_DIM).transpose(0, 2, 1, 3)
               for t in jnp.split(x @ w_qkv, 3, axis=-1)]
    o = moba_attention(rope(q, pos), rope(k, pos), v)
    return o.transpose(0, 2, 1, 3).reshape(B, S, MOBA_HEADS * HEAD_DIM) @ w_out


def moe_ffn(x, w_router, router_bias, w_gate, w_up, w_down, ws_gate, ws_up, ws_down):
    B, S, D = x.shape
    T = B * S
    xt = x.reshape(T, D)
    scores = jax.nn.sigmoid((xt @ w_router).astype(jnp.float32))
    choice = scores + router_bias.astype(jnp.float32)
    grp_score = lax.top_k(choice.reshape(T, N_GROUPS, N_EXPERTS // N_GROUPS), 2)[0].sum(-1)
    _, g_idx = lax.top_k(grp_score, TOPK_GROUPS)
    g_mask = jnp.any(g_idx[:, :, None] == jnp.arange(N_GROUPS)[None, None, :], axis=1)
    e_mask = jnp.repeat(g_mask, N_EXPERTS // N_GROUPS, axis=1)
    _, e_idx = lax.top_k(jnp.where(e_mask, choice, -jnp.inf), TOP_K)
    w = jnp.take_along_axis(scores, e_idx, axis=-1)
    w = w / jnp.sum(w, axis=-1, keepdims=True) * ROUTED_SCALE

    A = T * TOP_K
    RB = MOE_ROW_BLOCK
    NB = -(-(A + N_EXPERTS * (RB - 1)) // RB)
    e_flat = e_idx.reshape(A)
    tok_flat = jnp.repeat(jnp.arange(T), TOP_K)
    order = jnp.argsort(e_flat)
    e_sorted = e_flat[order]
    counts = jnp.bincount(e_flat, length=N_EXPERTS)
    padded = (counts + RB - 1) // RB * RB
    start = jnp.cumsum(counts) - counts
    p_end = jnp.cumsum(padded)
    dest = (p_end - padded)[e_sorted] + jnp.arange(A) - start[e_sorted]
    row_tok = jnp.full((NB * RB,), T, jnp.int32).at[dest].set(tok_flat[order].astype(jnp.int32))
    row_gate = jnp.zeros((NB * RB,), x.dtype).at[dest].set(w.reshape(A)[order].astype(x.dtype))
    blk_expert = jnp.minimum(jnp.searchsorted(p_end, jnp.arange(NB) * RB, side='right'), N_EXPERTS - 1)
    x_pad = jnp.concatenate([xt, jnp.zeros((1, D), xt.dtype)], axis=0)

    def expert_block(args):
        tok, gate, e = args
        xb = x_pad[tok]
        h = jax.nn.silu(xb @ w_gate[e]) * (xb @ w_up[e])
        return (h @ w_down[e]) * gate[:, None]

    ys = lax.map(expert_block, (row_tok.reshape(NB, RB), row_gate.reshape(NB, RB), blk_expert))
    routed = jax.ops.segment_sum(ys.reshape(NB * RB, D), row_tok, num_segments=T + 1)[:T]
    shared = (jax.nn.silu(xt @ ws_gate) * (xt @ ws_up)) @ ws_down
    return (routed + shared).astype(x.dtype).reshape(B, S, D)


def _normal(key, shape, scale):
    return jax.random.normal(key, shape, jnp.float32) * scale


def setup_inputs(seed: int = 0) -> dict:
    key = jax.random.key(seed)
    k_x, k_layers = jax.random.split(key)
    out = {"x": jax.random.normal(k_x, (BATCH, SEQ, D_MODEL), jnp.float32)}
    for i, lk in enumerate(jax.random.split(k_layers, DEPTH)):
        ks = iter(jax.random.split(lk, 32))
        p = "l%d_" % i
        if i % 2 == 0:
            out[p + "w_in"] = _normal(next(ks), (D_MODEL, IN_W0), D_MODEL ** -0.5)
            out[p + "nsa_pe_k"] = _normal(next(ks), (NSA_CMP_LEN, HEAD_DIM), 0.1)
            out[p + "nsa_pe_v"] = _normal(next(ks), (NSA_CMP_LEN, HEAD_DIM), 0.1)
            out[p + "nsa_cmp_k1"] = _normal(next(ks), (NSA_CMP_LEN * HEAD_DIM, NSA_CMP_HIDDEN), (NSA_CMP_LEN * HEAD_DIM) ** -0.5)
            out[p + "nsa_cmp_k2"] = _normal(next(ks), (NSA_CMP_HIDDEN, HEAD_DIM), NSA_CMP_HIDDEN ** -0.5)
            out[p + "nsa_cmp_v1"] = _normal(next(ks), (NSA_CMP_LEN * HEAD_DIM, NSA_CMP_HIDDEN), (NSA_CMP_LEN * HEAD_DIM) ** -0.5)
            out[p + "nsa_cmp_v2"] = _normal(next(ks), (NSA_CMP_HIDDEN, HEAD_DIM), NSA_CMP_HIDDEN ** -0.5)
            out[p + "mla_q_norm"] = 1.0 + _normal(next(ks), (MLA_Q_RANK,), 0.01)
            out[p + "mla_w_uq"] = _normal(next(ks), (MLA_Q_RANK, MLA_HEADS * (MLA_NOPE_DIM + MLA_ROPE_DIM)), MLA_Q_RANK ** -0.5)
            out[p + "mla_kv_norm"] = 1.0 + _normal(next(ks), (MLA_KV_RANK,), 0.01)
            out[p + "mla_w_ukv"] = _normal(next(ks), (MLA_KV_RANK, MLA_HEADS * (MLA_NOPE_DIM + MLA_V_DIM)), MLA_KV_RANK ** -0.5)
            out[p + "w_out"] = _normal(next(ks), (MIX_W0, D_MODEL), MIX_W0 ** -0.5 * DN_BETA)
        else:
            out[p + "w_qkv"] = _normal(next(ks), (D_MODEL, 3 * MOBA_HEADS * HEAD_DIM), D_MODEL ** -0.5)
            out[p + "w_out"] = _normal(next(ks), (MOBA_HEADS * HEAD_DIM, D_MODEL), (MOBA_HEADS * HEAD_DIM) ** -0.5 * DN_BETA)
        out[p + "ln1_g"] = 1.0 + _normal(next(ks), (D_MODEL,), 0.01)
        out[p + "ln1_b"] = _normal(next(ks), (D_MODEL,), 0.01)
        out[p + "router"] = _normal(next(ks), (D_MODEL, N_EXPERTS), D_MODEL ** -0.5)
        out[p + "router_bias"] = _normal(next(ks), (N_EXPERTS,), 0.01)
        out[p + "exp_gate"] = _normal(next(ks), (N_EXPERTS, D_MODEL, EXPERT_FF), D_MODEL ** -0.5)
        out[p + "exp_up"] = _normal(next(ks), (N_EXPERTS, D_MODEL, EXPERT_FF), D_MODEL ** -0.5)
        out[p + "exp_down"] = _normal(next(ks), (N_EXPERTS, EXPERT_FF, D_MODEL), EXPERT_FF ** -0.5 * DN_BETA)
        out[p + "sh_gate"] = _normal(next(ks), (D_MODEL, SHARED_FF), D_MODEL ** -0.5)
        out[p + "sh_up"] = _normal(next(ks), (D_MODEL, SHARED_FF), D_MODEL ** -0.5)
        out[p + "sh_down"] = _normal(next(ks), (SHARED_FF, D_MODEL), SHARED_FF ** -0.5 * DN_BETA)
        out[p + "ln2_g"] = 1.0 + _normal(next(ks), (D_MODEL,), 0.01)
        out[p + "ln2_b"] = _normal(next(ks), (D_MODEL,), 0.01)
    return out


def reference(x,
              l0_w_in, l0_nsa_pe_k, l0_nsa_pe_v, l0_nsa_cmp_k1, l0_nsa_cmp_k2, l0_nsa_cmp_v1, l0_nsa_cmp_v2,
              l0_mla_q_norm, l0_mla_w_uq, l0_mla_kv_norm, l0_mla_w_ukv, l0_w_out,
              l0_ln1_g, l0_ln1_b, l0_router, l0_router_bias, l0_exp_gate, l0_exp_up, l0_exp_down,
              l0_sh_gate, l0_sh_up, l0_sh_down, l0_ln2_g, l0_ln2_b,
              l1_w_qkv, l1_w_out, l1_ln1_g, l1_ln1_b, l1_router, l1_router_bias,
              l1_exp_gate, l1_exp_up, l1_exp_down, l1_sh_gate, l1_sh_up, l1_sh_down, l1_ln2_g, l1_ln2_b):
    layers = [
        (lambda h: sparse_latent_mixer(h, l0_w_in, l0_nsa_pe_k, l0_nsa_pe_v, l0_nsa_cmp_k1, l0_nsa_cmp_k2,
                                       l0_nsa_cmp_v1, l0_nsa_cmp_v2, l0_mla_q_norm, l0_mla_w_uq,
                                       l0_mla_kv_norm, l0_mla_w_ukv, l0_w_out),
         (l0_ln1_g, l0_ln1_b),
         (l0_router, l0_router_bias, l0_exp_gate, l0_exp_up, l0_exp_down, l0_sh_gate, l0_sh_up, l0_sh_down),
         (l0_ln2_g, l0_ln2_b)),
        (lambda h: moba_mixer(h, l1_w_qkv, l1_w_out),
         (l1_ln1_g, l1_ln1_b),
         (l1_router, l1_router_bias, l1_exp_gate, l1_exp_up, l1_exp_down, l1_sh_gate, l1_sh_up, l1_sh_down),
         (l1_ln2_g, l1_ln2_b)),
    ]
    for i in range(DEPTH):
        mixer, ln1, moe_p, ln2 = layers[i]
        x = layer_norm(DN_ALPHA * x + mixer(x), *ln1)
        x = layer_norm(DN_ALPHA * x + moe_ffn(x, *moe_p), *ln2)
    return x
```

```python
import functools

import numpy as np
import jax
import jax.numpy as jnp
from jax import lax
from jax.experimental import pallas as pl
from jax.experimental.pallas import tpu as pltpu

F32 = jnp.float32
BF16 = jnp.bfloat16

LANES = 128
VMEM_LIMIT = 48 * 1024 * 1024

D_MODEL = 1024
DEPTH = 2
HEAD_DIM = 64
ROPE_THETA = 10000.0
LN_EPS = 1e-5
RMS_EPS = 1e-6

NSA_HEADS = 8
NSA_KV_HEADS = 2
NSA_GROUP = NSA_HEADS // NSA_KV_HEADS
NSA_CMP_LEN = 32
NSA_CMP_STRIDE = 16
NSA_CMP_HIDDEN = 128
NSA_SEL_BLOCK = 64
NSA_SEL_TOPN = 8
NSA_WINDOW = 512

MLA_HEADS = 8
MLA_Q_RANK = 256
MLA_KV_RANK = 128
MLA_NOPE_DIM = 64
MLA_ROPE_DIM = 32
MLA_V_DIM = 64

MOBA_HEADS = 16
MOBA_BLOCK = 256
MOBA_TOPK = 3

N_EXPERTS = 64
N_GROUPS = 8
GROUP_SIZE = N_EXPERTS // N_GROUPS
TOPK_GROUPS = 4
TOP_K = 8
EXPERT_FF = 256
ROUTED_SCALE = 2.5

DN_ALPHA = (2 * DEPTH) ** 0.25

ROW_TILE = 256
ATTN_TQ = 256
ATTN_TK = 256
MOE_TILE = 512

NEG_INF = float("-inf")


def _cparams(*sem):
    return pltpu.CompilerParams(dimension_semantics=sem, vmem_limit_bytes=VMEM_LIMIT)


def _dot(a, b):
    return jnp.dot(a.astype(BF16), b.astype(BF16), preferred_element_type=F32)


def _dot_nt(a, b):
    return lax.dot_general(a.astype(BF16), b.astype(BF16), (((1,), (1,)), ((), ())),
                           preferred_element_type=F32)


def _split(a):
    hi = a.astype(BF16)
    lo = (a - hi.astype(F32)).astype(BF16)
    return hi, lo


def _dot_split_lhs(a, b_bf16):
    hi, lo = _split(a)
    return (jnp.dot(hi, b_bf16, preferred_element_type=F32)
            + jnp.dot(lo, b_bf16, preferred_element_type=F32))


def _dot_split_both(a, b):
    ah, al = _split(a)
    bh, bl = _split(b)
    return (jnp.dot(ah, bh, preferred_element_type=F32)
            + jnp.dot(al, bh, preferred_element_type=F32)
            + jnp.dot(ah, bl, preferred_element_type=F32))


def _lane_iota(shape):
    return lax.broadcasted_iota(jnp.int32, shape, len(shape) - 1)


def _rope_lanes(x, cos, sin, first_half, half):
    n = x.shape[-1]
    rot = jnp.where(first_half, -pltpu.roll(x, n - half, 1), pltpu.roll(x, half, 1))
    return x * cos + rot * sin


def _layer_norm(z, g, b):
    mu = jnp.mean(z, axis=-1, keepdims=True)
    zc = z - mu
    var = jnp.mean(zc * zc, axis=-1, keepdims=True)
    return zc * lax.rsqrt(var + LN_EPS) * g + b


def _rms_norm(x, g):
    return x * lax.rsqrt(jnp.mean(x * x, axis=-1, keepdims=True) + RMS_EPS) * g


def _group_rank(x, group):
    n = x.shape[-1]
    pos = _lane_iota(x.shape) % group
    rank = jnp.zeros(x.shape, F32)
    for d in range(1, group):
        lower = pltpu.roll(x, d, 1)
        upper = pltpu.roll(x, n - d, 1)
        rank = rank + jnp.where((pos >= d) & (lower >= x), 1.0, 0.0)
        rank = rank + jnp.where((pos + d < group) & (upper > x), 1.0, 0.0)
    return rank


def _group_sum(x, group):
    n = x.shape[-1]
    pos = _lane_iota(x.shape) % group
    tot = x
    for d in range(1, group):
        tot = tot + jnp.where(pos >= d, pltpu.roll(x, d, 1), 0.0)
        tot = tot + jnp.where(pos + d < group, pltpu.roll(x, n - d, 1), 0.0)
    return tot


def _take_top(x, key, k):
    big = jnp.int32(1 << 30)
    taken = jnp.zeros(x.shape, jnp.bool_)
    for _ in range(k):
        m = jnp.max(x, axis=-1, keepdims=True)
        first = jnp.min(jnp.where(x == m, key, big), axis=-1, keepdims=True)
        hit = key == first
        taken = taken | hit
        x = jnp.where(hit, NEG_INF, x)
    return taken


def _rope_tables(S):
    pos = jnp.arange(S, dtype=F32)[:, None]

    def cs(half):
        inv = ROPE_THETA ** (-jnp.arange(half, dtype=F32) / half)
        ang = pos * inv[None, :]
        return jnp.cos(ang), jnp.sin(ang)

    c32, s32 = cs(HEAD_DIM // 2)
    c16, s16 = cs(MLA_ROPE_DIM // 2)
    one = jnp.ones((S, 1), F32)
    zero = jnp.zeros((S, 1), F32)
    cos_pair = jnp.concatenate([c32] * 4, -1)
    sin_pair = jnp.concatenate([s32] * 4, -1)
    cos_kv = jnp.concatenate([c32, c32, jnp.tile(one, (1, 64))], -1)
    sin_kv = jnp.concatenate([s32, s32, jnp.tile(zero, (1, 64))], -1)
    cos_mla = jnp.concatenate([jnp.tile(one, (1, 64)), c16, c16, jnp.tile(one, (1, 32))], -1)
    sin_mla = jnp.concatenate([jnp.tile(zero, (1, 64)), s16, s16, jnp.tile(zero, (1, 32))], -1)
    return jnp.stack([cos_pair, sin_pair, cos_kv, sin_kv, cos_mla, sin_mla])


def _cmp_rope_tables(n_chunks):
    pos = (jnp.arange(n_chunks, dtype=F32) * NSA_CMP_STRIDE + (NSA_CMP_LEN - 1))[:, None]
    half = HEAD_DIM // 2
    inv = ROPE_THETA ** (-jnp.arange(half, dtype=F32) / half)
    ang = pos * inv[None, :]
    c, s = jnp.cos(ang), jnp.sin(ang)
    one = jnp.ones((n_chunks, 64), F32)
    return jnp.stack([jnp.concatenate([c, c, one], -1), jnp.concatenate([s, s, 0 * one], -1)])


P0_Q = 0
P0_KVS = 512
P0_KVW = 768
P0_KC = 1024
P0_VC = 1152
P0_GATE = 1280
P0_CQ = 1536
P0_CKV = 1792
P0_KR = 1920
P0_W = 2048


def _proj0_columns():
    q_w = NSA_HEADS * HEAD_DIM
    kv0 = q_w
    piece = NSA_KV_HEADS * HEAD_DIM
    gate0 = kv0 + 6 * piece
    cq0 = gate0 + 3 * NSA_HEADS
    ckv0 = cq0 + MLA_Q_RANK
    kr0 = ckv0 + MLA_KV_RANK
    src = -np.ones((P0_W,), np.int64)
    src[P0_Q:P0_Q + q_w] = np.arange(q_w)
    d = np.arange(HEAD_DIM)
    for base, kp, vp in ((P0_KVS, 2, 3), (P0_KVW, 4, 5)):
        for g in range(NSA_KV_HEADS):
            src[base + g * 128 + d] = kv0 + kp * piece + g * HEAD_DIM + d
            src[base + g * 128 + 64 + d] = kv0 + vp * piece + g * HEAD_DIM + d
    src[P0_KC:P0_KC + piece] = kv0 + 0 * piece + np.arange(piece)
    src[P0_VC:P0_VC + piece] = kv0 + 1 * piece + np.arange(piece)
    for g in range(NSA_KV_HEADS):
        for br in range(3):
            for r in range(NSA_GROUP):
                src[P0_GATE + g * 128 + br * NSA_GROUP + r] = gate0 + br * NSA_HEADS + g * NSA_GROUP + r
    src[P0_CQ:P0_CQ + MLA_Q_RANK] = cq0 + np.arange(MLA_Q_RANK)
    src[P0_CKV:P0_CKV + MLA_KV_RANK] = ckv0 + np.arange(MLA_KV_RANK)
    src[P0_KR + 64:P0_KR + 64 + MLA_ROPE_DIM] = kr0 + np.arange(MLA_ROPE_DIM)
    return src


def _permute_columns(w, src):
    cols = jnp.take(w, jnp.asarray(np.maximum(src, 0)), axis=1)
    return jnp.where(jnp.asarray(src >= 0)[None, :], cols, 0.0)


def _proj0_kernel(x_ref, w_ref, tab_ref, qg_ref, kvg_ref, wuq_ref, wk_ref, wv_ref,
                  q_ref, kvs_ref, kvw_ref, kc_ref, vc_ref, gate_ref, qm_ref, km_ref, vm_ref):
    xb = x_ref[...].astype(BF16)
    lane = _lane_iota((xb.shape[0], LANES))
    pair_first = (lane % HEAD_DIM) < (HEAD_DIM // 2)
    mla_first = lane < (MLA_NOPE_DIM + MLA_ROPE_DIM // 2)
    cos_p, sin_p = tab_ref[0], tab_ref[1]
    cos_kv, sin_kv = tab_ref[2], tab_ref[3]
    cos_m, sin_m = tab_ref[4], tab_ref[5]

    def seg(c0, width):
        return jnp.dot(xb, w_ref[:, c0:c0 + width], preferred_element_type=F32)

    q = seg(P0_Q, 512)
    for j in range(4):
        blk = _rope_lanes(q[:, j * 128:(j + 1) * 128], cos_p, sin_p, pair_first, 32)
        q_ref[:, j * 128:(j + 1) * 128] = (blk * (HEAD_DIM ** -0.5)).astype(BF16)
    for c0, out in ((P0_KVS, kvs_ref), (P0_KVW, kvw_ref)):
        kv = seg(c0, 256)
        for j in range(2):
            blk = _rope_lanes(kv[:, j * 128:(j + 1) * 128], cos_kv, sin_kv, pair_first, 32)
            out[:, j * 128:(j + 1) * 128] = blk.astype(BF16)
    kc_ref[...] = seg(P0_KC, 128)
    vc_ref[...] = seg(P0_VC, 128)
    gate_ref[...] = seg(P0_GATE, 256)

    cq = _rms_norm(seg(P0_CQ, 256), qg_ref[...])
    qm = _dot(cq, wuq_ref[...])
    mla_scale = (MLA_NOPE_DIM + MLA_ROPE_DIM) ** -0.5
    for h in range(MLA_HEADS):
        blk = _rope_lanes(qm[:, h * 128:(h + 1) * 128], cos_m, sin_m, mla_first, 16)
        qm_ref[:, h * 128:(h + 1) * 128] = (blk * mla_scale).astype(BF16)
    ckv = _rms_norm(seg(P0_CKV, 128), kvg_ref[...]).astype(BF16)
    kn = jnp.dot(ckv, wk_ref[...], preferred_element_type=F32)
    kpe = _rope_lanes(seg(P0_KR, 128), cos_m, sin_m, mla_first, 16)
    for h in range(MLA_HEADS):
        km_ref[:, h * 128:(h + 1) * 128] = (kn[:, h * 128:(h + 1) * 128] + kpe).astype(BF16)
    vm_ref[...] = jnp.dot(ckv, wv_ref[...], preferred_element_type=F32).astype(BF16)


def _proj0(x2, w_perm, tabs, q_norm, kv_norm, wuq, wk, wv, S):
    T = x2.shape[0]
    tm = ROW_TILE
    ns = S // tm
    row = lambda w: pl.BlockSpec((tm, w), lambda i: (i, 0))
    full = lambda a: pl.BlockSpec(a.shape, lambda i: (0,) * a.ndim)
    widths = (512, 256, 256, 128, 128, 256, 1024, 1024, 512)
    dtypes = (BF16, BF16, BF16, F32, F32, F32, BF16, BF16, BF16)
    return pl.pallas_call(
        _proj0_kernel,
        grid=(T // tm,),
        in_specs=[row(D_MODEL), full(w_perm),
                  pl.BlockSpec((6, tm, LANES), lambda i: (0, i % ns, 0)),
                  full(q_norm), full(kv_norm), full(wuq), full(wk), full(wv)],
        out_specs=[row(w) for w in widths],
        out_shape=[jax.ShapeDtypeStruct((T, w), d) for w, d in zip(widths, dtypes)],
        compiler_params=_cparams("parallel"),
    )(x2, w_perm, tabs, q_norm, kv_norm, wuq, wk, wv)


def _compress_kernel(kc_ref, vc_ref, pe_ref, w1k_ref, w2k_ref, w1v_ref, w2v_ref, tab_ref,
                     ko_ref, vo_ref):
    half = NSA_CMP_STRIDE * HEAD_DIM
    n = kc_ref.shape[1]

    def mlp(x, pe_lo, pe_hi, w1_ref, w2_ref):
        first = _dot(x + pe_lo, w1_ref[0:half, :])
        second = _dot(x + pe_hi, w1_ref[half:2 * half, :])
        hidden = first + pltpu.roll(second, n - 1, 0)
        return _dot(jax.nn.gelu(hidden), w2_ref[...])

    k = mlp(kc_ref[0], pe_ref[0:1, :], pe_ref[1:2, :], w1k_ref, w2k_ref)
    lane = _lane_iota(k.shape)
    ko_ref[0] = _rope_lanes(k, tab_ref[0], tab_ref[1], lane < HEAD_DIM // 2, 32).astype(BF16)
    vo_ref[0] = mlp(vc_ref[0], pe_ref[2:3, :], pe_ref[3:4, :], w1v_ref, w2v_ref).astype(BF16)


def _compress(kc_chunks, vc_chunks, pe, w1k, w2k, w1v, w2v, ctab):
    n_bg, n, width = kc_chunks.shape
    blk = pl.BlockSpec((1, n, width), lambda i: (i, 0, 0))
    full = lambda a: pl.BlockSpec(a.shape, lambda i: (0,) * a.ndim)
    out = pl.BlockSpec((1, n, LANES), lambda i: (i, 0, 0))
    return pl.pallas_call(
        _compress_kernel,
        grid=(n_bg,),
        in_specs=[blk, blk, full(pe), full(w1k), full(w2k), full(w1v), full(w2v), full(ctab)],
        out_specs=[out, out],
        out_shape=[jax.ShapeDtypeStruct((n_bg, n, LANES), BF16)] * 2,
        compiler_params=_cparams("parallel"),
    )(kc_chunks, vc_chunks, pe, w1k, w2k, w1v, w2v, ctab)


def _cmp_attn_kernel(q_ref, k_ref, v_ref, gate_ref, c2s_ref, o_ref, sel_ref, *, n_sel, n_top):
    qi = pl.program_id(2)
    tq = q_ref.shape[1]
    n = k_ref.shape[1]
    q = q_ref[0]
    k = k_ref[0][:, 0:HEAD_DIM]
    v = v_ref[0][:, 0:HEAD_DIM]
    gates = jax.nn.sigmoid(gate_ref[0])
    pos = qi * tq + lax.broadcasted_iota(jnp.int32, (tq, 1), 0)
    cmp_end = lax.broadcasted_iota(jnp.int32, (1, n), 1) * NSA_CMP_STRIDE + (NSA_CMP_LEN - 1)
    visible = cmp_end <= pos
    p_sum = jnp.zeros((tq, n), F32)
    outs = []
    for r in range(NSA_GROUP):
        s = jnp.where(visible, _dot_nt(q[:, r * HEAD_DIM:(r + 1) * HEAD_DIM], k), NEG_INF)
        m = jnp.max(s, axis=-1, keepdims=True)
        e = jnp.exp(s - jnp.where(m > NEG_INF, m, 0.0))
        den = jnp.sum(e, axis=-1, keepdims=True)
        p = e / jnp.where(den > 0, den, 1.0)
        p_sum = p_sum + p
        outs.append(_dot(p, v) * gates[:, r:r + 1])
    o_ref[0] = jnp.concatenate(outs, axis=-1)

    imp = _dot_split_lhs(p_sum, c2s_ref[...])
    blk = _lane_iota(imp.shape)
    cur = pos // NSA_SEL_BLOCK
    forced = (blk == 0) | (blk == cur) | (blk == cur - 1)
    valid = (blk <= cur) & (blk < n_sel)
    score = jnp.where(valid, jnp.where(forced, jnp.inf, imp), NEG_INF)
    taken = _take_top(score, blk, n_top)
    sel_ref[0] = jnp.where(taken & valid, 1.0, 0.0)


def _cmp_attn(q3, kcmp, vcmp, gate3, c2s, n_sel):
    B, S, _ = q3.shape
    n = kcmp.shape[1]
    tq = ATTN_TQ
    G = NSA_KV_HEADS
    kern = functools.partial(_cmp_attn_kernel, n_sel=n_sel, n_top=min(NSA_SEL_TOPN, n_sel))
    return pl.pallas_call(
        kern,
        grid=(B, G, S // tq),
        in_specs=[pl.BlockSpec((1, tq, 256), lambda b, g, i: (b, i, g)),
                  pl.BlockSpec((1, n, LANES), lambda b, g, i: (b * G + g, 0, 0)),
                  pl.BlockSpec((1, n, LANES), lambda b, g, i: (b * G + g, 0, 0)),
                  pl.BlockSpec((1, tq, LANES), lambda b, g, i: (b, i, g)),
                  pl.BlockSpec(c2s.shape, lambda b, g, i: (0, 0))],
        out_specs=[pl.BlockSpec((1, tq, 256), lambda b, g, i: (b, i, g)),
                   pl.BlockSpec((1, tq, LANES), lambda b, g, i: (b, i, g))],
        out_shape=[jax.ShapeDtypeStruct((B, S, 512), F32),
                   jax.ShapeDtypeStruct((B, S, 256), F32)],
        compiler_params=_cparams("parallel", "parallel", "parallel"),
    )(q3, kcmp, vcmp, gate3, c2s)


def _flash_kernel(*refs, heads, dv, causal_window, sel_block, sel_base, gate_cols, v_from_k):
    it = iter(refs)
    q_ref = next(it)
    k_ref = next(it)
    v_ref = k_ref if v_from_k else next(it)
    sel_ref = next(it) if sel_block is not None else None
    gate_ref = next(it) if gate_cols is not None else None
    o_ref = next(it)
    m_sc, l_sc, acc_sc = next(it), next(it), next(it)

    hs = pl.program_id(1)
    qi = pl.program_id(2)
    ki = pl.program_id(3)
    nk = pl.num_programs(3)
    tq = q_ref.shape[1]
    tk = k_ref.shape[1]

    @pl.when(ki == 0)
    def _():
        m_sc[...] = jnp.full(m_sc.shape, NEG_INF, F32)
        l_sc[...] = jnp.zeros(l_sc.shape, F32)
        acc_sc[...] = jnp.zeros(acc_sc.shape, F32)

    q_lo_pos = qi * tq
    k_lo_pos = ki * tk
    needed = k_lo_pos <= q_lo_pos + (tq - 1)
    if causal_window is not None:
        needed = needed & (k_lo_pos + (tk - 1) > q_lo_pos - causal_window)

    @pl.when(needed)
    def _():
        qpos = q_lo_pos + lax.broadcasted_iota(jnp.int32, (tq, 1), 0)
        kpos = k_lo_pos + lax.broadcasted_iota(jnp.int32, (1, tk), 1)
        mask = kpos <= qpos
        if causal_window is not None:
            mask = mask & (qpos - kpos < causal_window)
        q = q_ref[0]
        k = k_ref[0]
        v = v_ref[0]
        if sel_ref is not None:
            sel = sel_ref[0].astype(BF16)
            sel_row = lax.broadcasted_iota(jnp.int32, (sel.shape[1], tk), 0)
            key_blk = (k_lo_pos + lax.broadcasted_iota(jnp.int32, (sel.shape[1], tk), 1)) // sel_block
        for h, (q_lo, dk, k_off, v_off) in enumerate(heads):
            hmask = mask
            if sel_ref is not None:
                expand = jnp.where(sel_row == sel_base(hs, h) + key_blk, 1.0, 0.0).astype(BF16)
                chosen = jnp.dot(sel, expand, preferred_element_type=F32)
                hmask = hmask & (chosen > 0.5)
            s = _dot_nt(q[:, q_lo:q_lo + dk], k[:, k_off:k_off + dk])
            s = jnp.where(hmask, s, NEG_INF)
            m_old = m_sc[h]
            m_new = jnp.maximum(m_old, jnp.max(s, axis=-1, keepdims=True))
            m_safe = jnp.where(m_new > NEG_INF, m_new, 0.0)
            p = jnp.exp(s - m_safe)
            alpha = jnp.exp(m_old - m_safe)
            l_sc[h] = alpha * l_sc[h] + jnp.sum(p, axis=-1, keepdims=True)
            acc_sc[h] = alpha * acc_sc[h] + _dot(p, v[:, v_off:v_off + dv])
            m_sc[h] = m_new

    @pl.when(ki == nk - 1)
    def _():
        outs = []
        if gate_ref is not None:
            gates = jax.nn.sigmoid(gate_ref[0])
        for h in range(len(heads)):
            l = l_sc[h]
            o = acc_sc[h] / jnp.where(l > 0, l, 1.0)
            if gate_ref is not None:
                c = gate_cols[h]
                o = o * gates[:, c:c + 1]
            outs.append(o)
        o_ref[0] = jnp.concatenate(outs, axis=-1).astype(o_ref.dtype)


def _flash(q, k, v, sel, gate, *, n_steps, q_w, k_w, v_w, heads, causal_window=None,
           sel_block=None, sel_base=None, sel_col=None, gate_cols=None, out_dtype=F32):
    B, S, _ = q.shape
    tq, tk = ATTN_TQ, ATTN_TK
    dv = HEAD_DIM
    nh = len(heads)

    def kv_idx(i, j):
        hi = (i * tq + tq - 1) // tk
        lo = 0
        if causal_window is not None:
            lo = jnp.maximum(i * tq - causal_window + 1, 0) // tk
        return jnp.clip(j, lo, hi)

    in_specs = [pl.BlockSpec((1, tq, q_w), lambda b, h, i, j: (b, i, h)),
                pl.BlockSpec((1, tk, k_w), lambda b, h, i, j: (b, kv_idx(i, j), h))]
    args = [q, k]
    if v is not None:
        in_specs.append(pl.BlockSpec((1, tk, v_w), lambda b, h, i, j: (b, kv_idx(i, j), h)))
        args.append(v)
    if sel is not None:
        in_specs.append(pl.BlockSpec((1, tq, LANES), lambda b, h, i, j: (b, i, sel_col(h))))
        args.append(sel)
    if gate is not None:
        in_specs.append(pl.BlockSpec((1, tq, LANES), lambda b, h, i, j: (b, i, h)))
        args.append(gate)
    kern = functools.partial(_flash_kernel, heads=tuple(heads), dv=dv, causal_window=causal_window,
                             sel_block=sel_block, sel_base=sel_base, gate_cols=gate_cols,
                             v_from_k=v is None)
    return pl.pallas_call(
        kern,
        grid=(B, n_steps, S // tq, S // tk),
        in_specs=in_specs,
        out_specs=pl.BlockSpec((1, tq, nh * dv), lambda b, h, i, j: (b, i, h)),
        out_shape=jax.ShapeDtypeStruct((B, S, n_steps * nh * dv), out_dtype),
        scratch_shapes=[pltpu.VMEM((nh, tq, 1), F32), pltpu.VMEM((nh, tq, 1), F32),
                        pltpu.VMEM((nh, tq, dv), F32)],
        compiler_params=_cparams("parallel", "parallel", "parallel", "arbitrary"),
    )(*args)


def _outproj_kernel(*refs, group_sizes):
    it = iter(refs)
    y = None
    for n_in in group_sizes:
        acts = [next(it)[...].astype(F32) for _ in range(n_in)]
        w_ref = next(it)
        a = acts[0]
        for extra in acts[1:]:
            a = a + extra
        part = _dot(a, w_ref[...])
        y = part if y is None else y + part
    x_ref, g_ref, b_ref, o_ref = next(it), next(it), next(it), next(it)
    o_ref[...] = _layer_norm(DN_ALPHA * x_ref[...] + y, g_ref[...], b_ref[...])


def _outproj_ln(groups, x2, g, b):
    T = x2.shape[0]
    tm = ROW_TILE
    in_specs, args, sizes = [], [], []
    for acts, w in groups:
        for a in acts:
            in_specs.append(pl.BlockSpec((tm, a.shape[1]), lambda i: (i, 0)))
            args.append(a)
        in_specs.append(pl.BlockSpec(w.shape, lambda i: (0, 0)))
        args.append(w)
        sizes.append(len(acts))
    in_specs += [pl.BlockSpec((tm, D_MODEL), lambda i: (i, 0)),
                 pl.BlockSpec((1, D_MODEL), lambda i: (0, 0)),
                 pl.BlockSpec((1, D_MODEL), lambda i: (0, 0))]
    args += [x2, g.reshape(1, -1), b.reshape(1, -1)]
    return pl.pallas_call(
        functools.partial(_outproj_kernel, group_sizes=tuple(sizes)),
        grid=(T // tm,),
        in_specs=in_specs,
        out_specs=pl.BlockSpec((tm, D_MODEL), lambda i: (i, 0)),
        out_shape=jax.ShapeDtypeStruct((T, D_MODEL), F32),
        compiler_params=_cparams("parallel"),
    )(*args)


def _router_kernel(x_ref, w_ref, bias_ref, g_ref):
    logits = _dot_split_both(x_ref[...], w_ref[...])
    lane = _lane_iota(logits.shape)
    real = lane < N_EXPERTS
    scores = jax.nn.sigmoid(logits)
    choice = jnp.where(real, scores + bias_ref[...], NEG_INF)
    top2 = jnp.where(_group_rank(choice, GROUP_SIZE) < 2, choice, 0.0)
    grp_score = jnp.where(real, _group_sum(top2, GROUP_SIZE), NEG_INF)
    grp_taken = _take_top(grp_score, lane // GROUP_SIZE, TOPK_GROUPS)
    masked = jnp.where(grp_taken & real, choice, NEG_INF)
    taken = _take_top(masked, lane, TOP_K)
    w = jnp.where(taken, scores, 0.0)
    g_ref[...] = w / jnp.sum(w, axis=-1, keepdims=True) * ROUTED_SCALE


def _router(x2, w_router_pad, bias_pad):
    T = x2.shape[0]
    tm = ROW_TILE
    return pl.pallas_call(
        _router_kernel,
        grid=(T // tm,),
        in_specs=[pl.BlockSpec((tm, D_MODEL), lambda i: (i, 0)),
                  pl.BlockSpec(w_router_pad.shape, lambda i: (0, 0)),
                  pl.BlockSpec((1, LANES), lambda i: (0, 0))],
        out_specs=pl.BlockSpec((tm, LANES), lambda i: (i, 0)),
        out_shape=jax.ShapeDtypeStruct((T, LANES), F32),
        compiler_params=_cparams("parallel"),
    )(x2, w_router_pad, bias_pad)


def _moe_kernel(x_ref, gates_ref, wg_ref, wu_ref, wd_ref, g_ref, b_ref, o_ref, xb_sc, acc_sc):
    e = pl.program_id(1)
    last = pl.num_programs(1) - 1

    @pl.when(e == 0)
    def _():
        xb_sc[...] = x_ref[...].astype(BF16)
        acc_sc[...] = jnp.zeros(acc_sc.shape, F32)

    xb = xb_sc[...]
    hg = jnp.dot(xb, wg_ref[0], preferred_element_type=F32)
    hu = jnp.dot(xb, wu_ref[0], preferred_element_type=F32)
    y = _dot(jax.nn.silu(hg) * hu, wd_ref[0])
    gates = gates_ref[...]
    col = jnp.sum(jnp.where(_lane_iota(gates.shape) == e, gates, 0.0), axis=-1, keepdims=True)
    acc_sc[...] += y * jnp.where(e == last, 1.0, col)

    @pl.when(e == last)
    def _():
        o_ref[...] = _layer_norm(DN_ALPHA * x_ref[...] + acc_sc[...], g_ref[...], b_ref[...])


def _moe_ln(x2, gates, wg, wu, wd, g, b):
    T = x2.shape[0]
    tm = MOE_TILE
    ne = wg.shape[0]
    return pl.pallas_call(
        _moe_kernel,
        grid=(T // tm, ne),
        in_specs=[pl.BlockSpec((tm, D_MODEL), lambda i, e: (i, 0)),
                  pl.BlockSpec((tm, LANES), lambda i, e: (i, 0)),
                  pl.BlockSpec((1, D_MODEL, EXPERT_FF), lambda i, e: (e, 0, 0)),
                  pl.BlockSpec((1, D_MODEL, EXPERT_FF), lambda i, e: (e, 0, 0)),
                  pl.BlockSpec((1, EXPERT_FF, D_MODEL), lambda i, e: (e, 0, 0)),
                  pl.BlockSpec((1, D_MODEL), lambda i, e: (0, 0)),
                  pl.BlockSpec((1, D_MODEL), lambda i, e: (0, 0))],
        out_specs=pl.BlockSpec((tm, D_MODEL), lambda i, e: (i, 0)),
        out_shape=jax.ShapeDtypeStruct((T, D_MODEL), F32),
        scratch_shapes=[pltpu.VMEM((tm, D_MODEL), BF16), pltpu.VMEM((tm, D_MODEL), F32)],
        compiler_params=_cparams("parallel", "arbitrary"),
    )(x2, gates, wg, wu, wd, g.reshape(1, -1), b.reshape(1, -1))


def _moe_block(x2, router, router_bias, exp_gate, exp_up, exp_down, sh_gate, sh_up, sh_down, g, b):
    w_router_pad = jnp.pad(router, ((0, 0), (0, LANES - N_EXPERTS)))
    bias_pad = jnp.pad(router_bias, (0, LANES - N_EXPERTS)).reshape(1, LANES)
    gates = _router(x2, w_router_pad, bias_pad)
    wg = jnp.concatenate([exp_gate, sh_gate[None]], 0).astype(BF16)
    wu = jnp.concatenate([exp_up, sh_up[None]], 0).astype(BF16)
    wd = jnp.concatenate([exp_down, sh_down[None]], 0).astype(BF16)
    return _moe_ln(x2, gates, wg, wu, wd, g, b)


def _qkv1_kernel(x_ref, w_ref, tab_ref, q_ref, k_ref, v_ref, kmean_ref):
    xb = x_ref[...].astype(BF16)
    width = MOBA_HEADS * HEAD_DIM
    lane = _lane_iota((xb.shape[0], LANES))
    pair_first = (lane % HEAD_DIM) < (HEAD_DIM // 2)
    cos_p, sin_p = tab_ref[0], tab_ref[1]
    for j in range(width // LANES):
        c = j * LANES
        q = jnp.dot(xb, w_ref[:, c:c + LANES], preferred_element_type=F32)
        q = _rope_lanes(q, cos_p, sin_p, pair_first, 32)
        q_ref[:, c:c + LANES] = (q * (HEAD_DIM ** -0.5)).astype(BF16)
        k = jnp.dot(xb, w_ref[:, width + c:width + c + LANES], preferred_element_type=F32)
        k = _rope_lanes(k, cos_p, sin_p, pair_first, 32)
        k_ref[:, c:c + LANES] = k.astype(BF16)
        kmean_ref[0, :, c:c + LANES] = jnp.mean(k, axis=0, keepdims=True)
        v = jnp.dot(xb, w_ref[:, 2 * width + c:2 * width + c + LANES], preferred_element_type=F32)
        v_ref[:, c:c + LANES] = v.astype(BF16)


def _qkv1(x2, w_qkv, tabs, S):
    T = x2.shape[0]
    tm = MOBA_BLOCK
    ns = S // tm
    width = MOBA_HEADS * HEAD_DIM
    row = pl.BlockSpec((tm, width), lambda i: (i, 0))
    return pl.pallas_call(
        _qkv1_kernel,
        grid=(T // tm,),
        in_specs=[pl.BlockSpec((tm, D_MODEL), lambda i: (i, 0)),
                  pl.BlockSpec(w_qkv.shape, lambda i: (0, 0)),
                  pl.BlockSpec((2, tm, LANES), lambda i: (0, i % ns, 0))],
        out_specs=[row, row, row, pl.BlockSpec((1, 1, width), lambda i: (i, 0, 0))],
        out_shape=[jax.ShapeDtypeStruct((T, width), BF16)] * 3
        + [jax.ShapeDtypeStruct((T // tm, 1, width), F32)],
        compiler_params=_cparams("parallel"),
    )(x2, w_qkv, tabs)


def _moba_select_kernel(q_ref, km_ref, sel_ref, *, n_top, n_blocks):
    i = pl.program_id(1)
    tq = q_ref.shape[1]
    gate = _dot_split_lhs_rhs(q_ref[0], km_ref[0])
    lane = _lane_iota(gate.shape)
    blk = lane % 8
    own = (i * tq + lax.broadcasted_iota(jnp.int32, (tq, 1), 0)) // MOBA_BLOCK
    score = jnp.where((blk < own) & (blk < n_blocks), gate, NEG_INF)
    rank = _group_rank(score, 8)
    chosen = ((score > NEG_INF) & (rank < n_top)) | (blk == own)
    sel_ref[0] = jnp.where(chosen, 1.0, 0.0)


def _dot_split_lhs_rhs(q_bf16, b):
    hi, lo = _split(b)
    return (jnp.dot(q_bf16, hi, preferred_element_type=F32)
            + jnp.dot(q_bf16, lo, preferred_element_type=F32))


def _moba_select(q3, km, n_top, n_blocks):
    B, S, width = q3.shape
    tq = ATTN_TQ
    kern = functools.partial(_moba_select_kernel, n_top=n_top, n_blocks=n_blocks)
    return pl.pallas_call(
        kern,
        grid=(B, S // tq),
        in_specs=[pl.BlockSpec((1, tq, width), lambda b, i: (b, i, 0)),
                  pl.BlockSpec((1, width, LANES), lambda b, i: (b, 0, 0))],
        out_specs=pl.BlockSpec((1, tq, LANES), lambda b, i: (b, i, 0)),
        out_shape=jax.ShapeDtypeStruct((B, S, LANES), F32),
        compiler_params=_cparams("parallel", "parallel"),
    )(q3, km)


def _layer0(x2, B, S, w_in, pe_k, pe_v, cmp_k1, cmp_k2, cmp_v1, cmp_v2,
            q_norm, w_uq, kv_norm, w_ukv, w_out, ln_g, ln_b):
    T = B * S
    G = NSA_KV_HEADS
    w_perm = _permute_columns(w_in, _proj0_columns()).astype(BF16)
    tabs = _rope_tables(S)
    qd = MLA_NOPE_DIM + MLA_ROPE_DIM
    wuq = jnp.pad(w_uq.reshape(MLA_Q_RANK, MLA_HEADS, qd), ((0, 0), (0, 0), (0, LANES - qd)))
    wuq = wuq.reshape(MLA_Q_RANK, MLA_HEADS * LANES).astype(BF16)
    wukv = w_ukv.reshape(MLA_KV_RANK, MLA_HEADS, MLA_NOPE_DIM + MLA_V_DIM)
    wk = jnp.pad(wukv[:, :, :MLA_NOPE_DIM], ((0, 0), (0, 0), (0, LANES - MLA_NOPE_DIM)))
    wk = wk.reshape(MLA_KV_RANK, MLA_HEADS * LANES).astype(BF16)
    wv = wukv[:, :, MLA_NOPE_DIM:].reshape(MLA_KV_RANK, MLA_HEADS * MLA_V_DIM).astype(BF16)

    q, kvs, kvw, kc, vc, gate, qm, km, vm = _proj0(
        x2, w_perm, tabs, q_norm.reshape(1, -1), kv_norm.reshape(1, -1), wuq, wk, wv, S)

    n_chunks = S // NSA_CMP_STRIDE
    chunk_w = NSA_CMP_STRIDE * HEAD_DIM

    def chunks(t):
        t = t.reshape(B, S, G, HEAD_DIM).transpose(0, 2, 1, 3)
        return t.reshape(B * G, n_chunks, chunk_w)

    pe = jnp.stack([pe_k[:NSA_CMP_STRIDE].reshape(-1), pe_k[NSA_CMP_STRIDE:].reshape(-1),
                    pe_v[:NSA_CMP_STRIDE].reshape(-1), pe_v[NSA_CMP_STRIDE:].reshape(-1)])
    pad2 = lambda w: jnp.pad(w, ((0, 0), (0, LANES - HEAD_DIM))).astype(BF16)
    kcmp, vcmp = _compress(chunks(kc), chunks(vc), pe, cmp_k1.astype(BF16), pad2(cmp_k2),
                           cmp_v1.astype(BF16), pad2(cmp_v2), _cmp_rope_tables(n_chunks))

    n_sel = S // NSA_SEL_BLOCK
    n_cmp = (S - NSA_CMP_LEN) // NSA_CMP_STRIDE + 1
    tok = np.arange(n_chunks)[:, None] * NSA_CMP_STRIDE + np.arange(NSA_CMP_LEN)[None, :]
    c2s = (tok[:, :, None] // NSA_SEL_BLOCK == np.arange(LANES)[None, None, :]).sum(1) / NSA_CMP_LEN
    c2s[n_cmp:] = 0.0
    c2s = jnp.asarray(c2s, dtype=BF16)

    q3 = q.reshape(B, S, -1)
    gate3 = gate.reshape(B, S, -1)
    o_cmp, sel = _cmp_attn(q3, kcmp, vcmp, gate3, c2s, n_sel)

    nsa_heads = [(r * HEAD_DIM, HEAD_DIM, 0, HEAD_DIM) for r in range(NSA_GROUP)]
    o_sel = _flash(q3, kvs.reshape(B, S, -1), None, sel, gate3, n_steps=G, q_w=256, k_w=LANES,
                   v_w=None, heads=nsa_heads, sel_block=NSA_SEL_BLOCK, sel_base=lambda hs, h: 0,
                   sel_col=lambda h: h, gate_cols=[NSA_GROUP + r for r in range(NSA_GROUP)])
    o_win = _flash(q3, kvw.reshape(B, S, -1), None, None, gate3, n_steps=G, q_w=256, k_w=LANES,
                   v_w=None, heads=nsa_heads, causal_window=NSA_WINDOW,
                   gate_cols=[2 * NSA_GROUP + r for r in range(NSA_GROUP)])
    mla_heads = [(h * LANES, LANES, h * LANES, h * MLA_V_DIM) for h in range(2)]
    o_mla = _flash(qm.reshape(B, S, -1), km.reshape(B, S, -1), vm.reshape(B, S, -1), None, None,
                   n_steps=MLA_HEADS // 2, q_w=256, k_w=256, v_w=LANES, heads=mla_heads,
                   out_dtype=BF16)

    n_nsa = NSA_HEADS * HEAD_DIM
    w_out_b = w_out.astype(BF16)
    groups = [([o_cmp.reshape(T, -1), o_sel.reshape(T, -1), o_win.reshape(T, -1)], w_out_b[:n_nsa]),
              ([o_mla.reshape(T, -1)], w_out_b[n_nsa:])]
    return _outproj_ln(groups, x2, ln_g, ln_b)


def _layer1(x2, B, S, w_qkv, w_out, ln_g, ln_b):
    T = B * S
    width = MOBA_HEADS * HEAD_DIM
    tabs = _rope_tables(S)[0:2]
    q, k, v, kmean = _qkv1(x2, w_qkv.astype(BF16), tabs, S)
    n_blocks = S // MOBA_BLOCK
    n_top = min(MOBA_TOPK, max(n_blocks - 1, 1))
    km = kmean.reshape(B, n_blocks, width).transpose(0, 2, 1)
    km = jnp.pad(km, ((0, 0), (0, 0), (0, 8 - n_blocks)))
    km = jnp.tile(km, (1, 1, MOBA_HEADS))
    diag = (np.arange(width)[:, None] // HEAD_DIM) == (np.arange(LANES)[None, :] // 8)
    km = jnp.where(jnp.asarray(diag)[None], km, 0.0)
    q3 = q.reshape(B, S, width)
    sel = _moba_select(q3, km, n_top, n_blocks)
    heads = [(h * HEAD_DIM, HEAD_DIM, h * HEAD_DIM, h * HEAD_DIM) for h in range(2)]
    o = _flash(q3, k.reshape(B, S, width), v.reshape(B, S, width), sel, None,
               n_steps=MOBA_HEADS // 2, q_w=LANES, k_w=LANES, v_w=LANES, heads=heads,
               sel_block=MOBA_BLOCK, sel_base=lambda hs, h: 8 * (2 * hs + h),
               sel_col=lambda h: 0, out_dtype=BF16)
    return _outproj_ln([([o.reshape(T, width)], w_out.astype(BF16))], x2, ln_g, ln_b)


def kernel(x, l0_w_in, l0_nsa_pe_k, l0_nsa_pe_v, l0_nsa_cmp_k1, l0_nsa_cmp_k2, l0_nsa_cmp_v1, l0_nsa_cmp_v2, l0_mla_q_norm, l0_mla_w_uq, l0_mla_kv_norm, l0_mla_w_ukv, l0_w_out, l0_ln1_g, l0_ln1_b, l0_router, l0_router_bias, l0_exp_gate, l0_exp_up, l0_exp_down, l0_sh_gate, l0_sh_up, l0_sh_down, l0_ln2_g, l0_ln2_b, l1_w_qkv, l1_w_out, l1_ln1_g, l1_ln1_b, l1_router, l1_router_bias, l1_exp_gate, l1_exp_up, l1_exp_down, l1_sh_gate, l1_sh_up, l1_sh_down, l1_ln2_g, l1_ln2_b):
    B, S, D = x.shape
    x2 = x.reshape(B * S, D)
    x2 = _layer0(x2, B, S, l0_w_in, l0_nsa_pe_k, l0_nsa_pe_v, l0_nsa_cmp_k1, l0_nsa_cmp_k2,
                 l0_nsa_cmp_v1, l0_nsa_cmp_v2, l0_mla_q_norm, l0_mla_w_uq, l0_mla_kv_norm,
                 l0_mla_w_ukv, l0_w_out, l0_ln1_g, l0_ln1_b)
    x2 = _moe_block(x2, l0_router, l0_router_bias, l0_exp_gate, l0_exp_up, l0_exp_down,
                    l0_sh_gate, l0_sh_up, l0_sh_down, l0_ln2_g, l0_ln2_b)
    x2 = _layer1(x2, B, S, l1_w_qkv, l1_w_out, l1_ln1_g, l1_ln1_b)
    x2 = _moe_block(x2, l1_router, l1_router_bias, l1_exp_gate, l1_exp_up, l1_exp_down,
                    l1_sh_gate, l1_sh_up, l1_sh_down, l1_ln2_g, l1_ln2_b)
    return x2.reshape(B, S, D)
```

```python
import functools

import numpy as np
import jax
import jax.numpy as jnp
from jax import lax
from jax.experimental import pallas as pl
from jax.experimental.pallas import tpu as pltpu

F32 = jnp.float32
BF16 = jnp.bfloat16

LANES = 128
VMEM_LIMIT = 48 * 1024 * 1024

D_MODEL = 1024
DEPTH = 2
HEAD_DIM = 64
ROPE_THETA = 10000.0
LN_EPS = 1e-5
RMS_EPS = 1e-6

NSA_HEADS = 8
NSA_KV_HEADS = 2
NSA_GROUP = NSA_HEADS // NSA_KV_HEADS
NSA_CMP_LEN = 32
NSA_CMP_STRIDE = 16
NSA_CMP_HIDDEN = 128
NSA_SEL_BLOCK = 64
NSA_SEL_TOPN = 8
NSA_WINDOW = 512

MLA_HEADS = 8
MLA_Q_RANK = 256
MLA_KV_RANK = 128
MLA_NOPE_DIM = 64
MLA_ROPE_DIM = 32
MLA_V_DIM = 64

MOBA_HEADS = 16
MOBA_BLOCK = 256
MOBA_TOPK = 3

N_EXPERTS = 64
N_GROUPS = 8
GROUP_SIZE = N_EXPERTS // N_GROUPS
TOPK_GROUPS = 4
TOP_K = 8
EXPERT_FF = 256
ROUTED_SCALE = 2.5

DN_ALPHA = (2 * DEPTH) ** 0.25

ROW_TILE = 256
ATTN_TQ = 256
ATTN_TK = 256
MOE_TILE = 512
FLASH_HEADS = 4
_OWN_KV_HEADS = tuple((h, h, h) for h in range(FLASH_HEADS))

NEG_INF = float("-inf")


def _cparams(*sem):
    return pltpu.CompilerParams(dimension_semantics=sem, vmem_limit_bytes=VMEM_LIMIT)


def _dot(a, b):
    return jnp.dot(a.astype(BF16), b.astype(BF16), preferred_element_type=F32)


def _dot_nt(a, b):
    return lax.dot_general(a.astype(BF16), b.astype(BF16), (((1,), (1,)), ((), ())),
                           preferred_element_type=F32)


def _split(a):
    hi = a.astype(BF16)
    lo = (a - hi.astype(F32)).astype(BF16)
    return hi, lo


def _dot_split_lhs(a, b_bf16):
    hi, lo = _split(a)
    return (jnp.dot(hi, b_bf16, preferred_element_type=F32)
            + jnp.dot(lo, b_bf16, preferred_element_type=F32))


def _dot_split_both(a, b):
    ah, al = _split(a)
    bh, bl = _split(b)
    return (jnp.dot(ah, bh, preferred_element_type=F32)
            + jnp.dot(al, bh, preferred_element_type=F32)
            + jnp.dot(ah, bl, preferred_element_type=F32))


def _lane_iota(shape):
    return lax.broadcasted_iota(jnp.int32, shape, len(shape) - 1)


def _rope_lanes(x, cos, sin, first_half, half):
    n = x.shape[-1]
    rot = jnp.where(first_half, -pltpu.roll(x, n - half, 1), pltpu.roll(x, half, 1))
    return x * cos + rot * sin


def _layer_norm(z, g, b):
    mu = jnp.mean(z, axis=-1, keepdims=True)
    zc = z - mu
    var = jnp.mean(zc * zc, axis=-1, keepdims=True)
    return zc * lax.rsqrt(var + LN_EPS) * g + b


def _rms_norm(x, g):
    return x * lax.rsqrt(jnp.mean(x * x, axis=-1, keepdims=True) + RMS_EPS) * g


def _group_rank(x, group):
    n = x.shape[-1]
    pos = _lane_iota(x.shape) % group
    rank = jnp.zeros(x.shape, F32)
    for d in range(1, group):
        lower = pltpu.roll(x, d, 1)
        upper = pltpu.roll(x, n - d, 1)
        rank = rank + jnp.where((pos >= d) & (lower >= x), 1.0, 0.0)
        rank = rank + jnp.where((pos + d < group) & (upper > x), 1.0, 0.0)
    return rank


def _group_sum(x, group):
    n = x.shape[-1]
    pos = _lane_iota(x.shape) % group
    tot = x
    for d in range(1, group):
        tot = tot + jnp.where(pos >= d, pltpu.roll(x, d, 1), 0.0)
        tot = tot + jnp.where(pos + d < group, pltpu.roll(x, n - d, 1), 0.0)
    return tot


def _take_top(x, key, k):
    big = jnp.int32(1 << 30)
    taken = jnp.zeros(x.shape, jnp.bool_)
    for _ in range(k):
        m = jnp.max(x, axis=-1, keepdims=True)
        first = jnp.min(jnp.where(x == m, key, big), axis=-1, keepdims=True)
        hit = key == first
        taken = taken | hit
        x = jnp.where(hit, NEG_INF, x)
    return taken


def _rope_tables(S):
    pos = jnp.arange(S, dtype=F32)[:, None]

    def cs(half):
        inv = ROPE_THETA ** (-jnp.arange(half, dtype=F32) / half)
        ang = pos * inv[None, :]
        return jnp.cos(ang), jnp.sin(ang)

    c32, s32 = cs(HEAD_DIM // 2)
    c16, s16 = cs(MLA_ROPE_DIM // 2)
    one = jnp.ones((S, 1), F32)
    zero = jnp.zeros((S, 1), F32)
    cos_pair = jnp.concatenate([c32] * 4, -1)
    sin_pair = jnp.concatenate([s32] * 4, -1)
    cos_kv = jnp.concatenate([c32, c32, jnp.tile(one, (1, 64))], -1)
    sin_kv = jnp.concatenate([s32, s32, jnp.tile(zero, (1, 64))], -1)
    cos_mla = jnp.concatenate([jnp.tile(one, (1, 64)), c16, c16, jnp.tile(one, (1, 32))], -1)
    sin_mla = jnp.concatenate([jnp.tile(zero, (1, 64)), s16, s16, jnp.tile(zero, (1, 32))], -1)
    return jnp.stack([cos_pair, sin_pair, cos_kv, sin_kv, cos_mla, sin_mla])


def _cmp_rope_tables(n_chunks):
    pos = (jnp.arange(n_chunks, dtype=F32) * NSA_CMP_STRIDE + (NSA_CMP_LEN - 1))[:, None]
    half = HEAD_DIM // 2
    inv = ROPE_THETA ** (-jnp.arange(half, dtype=F32) / half)
    ang = pos * inv[None, :]
    c, s = jnp.cos(ang), jnp.sin(ang)
    one = jnp.ones((n_chunks, 64), F32)
    return jnp.stack([jnp.concatenate([c, c, one], -1), jnp.concatenate([s, s, 0 * one], -1)])


P0_Q = 0
P0_KVS = 512
P0_KVW = 768
P0_KC = 1024
P0_VC = 1152
P0_GATE = 1280
P0_CQ = 1536
P0_CKV = 1792
P0_KR = 1920
P0_W = 2048


def _proj0_columns():
    q_w = NSA_HEADS * HEAD_DIM
    kv0 = q_w
    piece = NSA_KV_HEADS * HEAD_DIM
    gate0 = kv0 + 6 * piece
    cq0 = gate0 + 3 * NSA_HEADS
    ckv0 = cq0 + MLA_Q_RANK
    kr0 = ckv0 + MLA_KV_RANK
    src = -np.ones((P0_W,), np.int64)
    src[P0_Q:P0_Q + q_w] = np.arange(q_w)
    d = np.arange(HEAD_DIM)
    for base, kp, vp in ((P0_KVS, 2, 3), (P0_KVW, 4, 5)):
        for g in range(NSA_KV_HEADS):
            src[base + g * 128 + d] = kv0 + kp * piece + g * HEAD_DIM + d
            src[base + g * 128 + 64 + d] = kv0 + vp * piece + g * HEAD_DIM + d
    src[P0_KC:P0_KC + piece] = kv0 + 0 * piece + np.arange(piece)
    src[P0_VC:P0_VC + piece] = kv0 + 1 * piece + np.arange(piece)
    for g in range(NSA_KV_HEADS):
        for br in range(3):
            for r in range(NSA_GROUP):
                src[P0_GATE + g * 128 + br * NSA_GROUP + r] = gate0 + br * NSA_HEADS + g * NSA_GROUP + r
    src[P0_CQ:P0_CQ + MLA_Q_RANK] = cq0 + np.arange(MLA_Q_RANK)
    src[P0_CKV:P0_CKV + MLA_KV_RANK] = ckv0 + np.arange(MLA_KV_RANK)
    src[P0_KR + 64:P0_KR + 64 + MLA_ROPE_DIM] = kr0 + np.arange(MLA_ROPE_DIM)
    return src


def _permute_columns(w, src):
    cols = jnp.take(w, jnp.asarray(np.maximum(src, 0)), axis=1)
    return jnp.where(jnp.asarray(src >= 0)[None, :], cols, 0.0)


def _block_onehot(pos, block, shape):
    return jnp.where(_lane_iota(shape) == HEAD_DIM + pos // block, 1.0, 0.0)


def _proj0_kernel(x_ref, w_ref, tab_ref, qg_ref, kvg_ref, wuq_ref, wk_ref, wv_ref,
                  q_ref, ks_ref, vs_ref, kw_ref, vw_ref, kc_ref, vc_ref, gate_ref,
                  qm_ref, km_ref, vm_ref, *, seq_tiles):
    tm = x_ref.shape[0]
    xb = x_ref[...].astype(BF16)
    lane = _lane_iota((tm, LANES))
    low = lane < HEAD_DIM
    pair_first = (lane % HEAD_DIM) < (HEAD_DIM // 2)
    mla_first = lane < (MLA_NOPE_DIM + MLA_ROPE_DIM // 2)
    cos_p, sin_p = tab_ref[0], tab_ref[1]
    cos_kv, sin_kv = tab_ref[2], tab_ref[3]
    cos_m, sin_m = tab_ref[4], tab_ref[5]
    pos = (pl.program_id(0) % seq_tiles) * tm + lax.broadcasted_iota(jnp.int32, (tm, 1), 0)
    sel_onehot = _block_onehot(pos, NSA_SEL_BLOCK, (tm, LANES))
    one_lane = jnp.where(lane == HEAD_DIM, 1.0, 0.0)

    def seg(c0, width):
        return jnp.dot(xb, w_ref[:, c0:c0 + width], preferred_element_type=F32)

    q = seg(P0_Q, 512)
    for j in range(4):
        blk = _rope_lanes(q[:, j * 128:(j + 1) * 128], cos_p, sin_p, pair_first, 32)
        q_ref[:, j * 128:(j + 1) * 128] = (blk * (HEAD_DIM ** -0.5)).astype(BF16)
    for c0, k_out, v_out, extra in ((P0_KVS, ks_ref, vs_ref, sel_onehot), (P0_KVW, kw_ref, vw_ref, 0.0)):
        kv = seg(c0, 256)
        for j in range(2):
            blk = _rope_lanes(kv[:, j * 128:(j + 1) * 128], cos_kv, sin_kv, pair_first, 32)
            k_out[:, j * 128:(j + 1) * 128] = jnp.where(low, blk, extra).astype(BF16)
            v_out[:, j * 128:(j + 1) * 128] = jnp.where(low, pltpu.roll(blk, HEAD_DIM, 1), one_lane).astype(BF16)
    kc_ref[...] = seg(P0_KC, 128)
    vc_ref[...] = seg(P0_VC, 128)
    gate_ref[...] = seg(P0_GATE, 256)

    cq = _rms_norm(seg(P0_CQ, 256), qg_ref[...])
    qm = _dot(cq, wuq_ref[...])
    mla_scale = (MLA_NOPE_DIM + MLA_ROPE_DIM) ** -0.5
    for h in range(MLA_HEADS):
        blk = _rope_lanes(qm[:, h * 128:(h + 1) * 128], cos_m, sin_m, mla_first, 16)
        qm_ref[:, h * 128:(h + 1) * 128] = (blk * mla_scale).astype(BF16)
    ckv = _rms_norm(seg(P0_CKV, 128), kvg_ref[...]).astype(BF16)
    kn = jnp.dot(ckv, wk_ref[...], preferred_element_type=F32)
    kpe = _rope_lanes(seg(P0_KR, 128), cos_m, sin_m, mla_first, 16)
    for h in range(MLA_HEADS):
        km_ref[:, h * 128:(h + 1) * 128] = (kn[:, h * 128:(h + 1) * 128] + kpe).astype(BF16)
    vm = jnp.dot(ckv, wv_ref[...], preferred_element_type=F32)
    for h in range(MLA_HEADS):
        vm_ref[:, h * 128:(h + 1) * 128] = jnp.where(low, vm[:, h * 128:(h + 1) * 128], one_lane).astype(BF16)


def _proj0(x2, w_perm, tabs, q_norm, kv_norm, wuq, wk, wv, S):
    T = x2.shape[0]
    tm = ROW_TILE
    ns = S // tm
    row = lambda w: pl.BlockSpec((tm, w), lambda i: (i, 0))
    full = lambda a: pl.BlockSpec(a.shape, lambda i: (0,) * a.ndim)
    widths = (512, 256, 256, 256, 256, 128, 128, 256, 1024, 1024, 1024)
    dtypes = (BF16, BF16, BF16, BF16, BF16, F32, F32, F32, BF16, BF16, BF16)
    return pl.pallas_call(
        functools.partial(_proj0_kernel, seq_tiles=ns),
        grid=(T // tm,),
        in_specs=[row(D_MODEL), full(w_perm),
                  pl.BlockSpec((6, tm, LANES), lambda i: (0, i % ns, 0)),
                  full(q_norm), full(kv_norm), full(wuq), full(wk), full(wv)],
        out_specs=[row(w) for w in widths],
        out_shape=[jax.ShapeDtypeStruct((T, w), d) for w, d in zip(widths, dtypes)],
        compiler_params=_cparams("parallel"),
    )(x2, w_perm, tabs, q_norm, kv_norm, wuq, wk, wv)


def _compress_kernel(kc_ref, vc_ref, pe_ref, w1k_ref, w2k_ref, w1v_ref, w2v_ref, tab_ref,
                     ko_ref, vo_ref):
    half = NSA_CMP_STRIDE * HEAD_DIM
    n = kc_ref.shape[1]

    def mlp(x, pe_lo, pe_hi, w1_ref, w2_ref):
        first = _dot(x + pe_lo, w1_ref[0:half, :])
        second = _dot(x + pe_hi, w1_ref[half:2 * half, :])
        hidden = first + pltpu.roll(second, n - 1, 0)
        return _dot(jax.nn.gelu(hidden), w2_ref[...])

    k = mlp(kc_ref[0], pe_ref[0:1, :], pe_ref[1:2, :], w1k_ref, w2k_ref)
    lane = _lane_iota(k.shape)
    ko_ref[0] = _rope_lanes(k, tab_ref[0], tab_ref[1], lane < HEAD_DIM // 2, 32).astype(BF16)
    vo_ref[0] = mlp(vc_ref[0], pe_ref[2:3, :], pe_ref[3:4, :], w1v_ref, w2v_ref).astype(BF16)


def _compress(kc_chunks, vc_chunks, pe, w1k, w2k, w1v, w2v, ctab):
    n_bg, n, width = kc_chunks.shape
    blk = pl.BlockSpec((1, n, width), lambda i: (i, 0, 0))
    full = lambda a: pl.BlockSpec(a.shape, lambda i: (0,) * a.ndim)
    out = pl.BlockSpec((1, n, LANES), lambda i: (i, 0, 0))
    return pl.pallas_call(
        _compress_kernel,
        grid=(n_bg,),
        in_specs=[blk, blk, full(pe), full(w1k), full(w2k), full(w1v), full(w2v), full(ctab)],
        out_specs=[out, out],
        out_shape=[jax.ShapeDtypeStruct((n_bg, n, LANES), BF16)] * 2,
        compiler_params=_cparams("parallel"),
    )(kc_chunks, vc_chunks, pe, w1k, w2k, w1v, w2v, ctab)


def _cmp_attn_kernel(q_ref, k_ref, v_ref, gate_ref, c2s_ref, o_ref, qa_ref, *, n_sel, n_top):
    qi = pl.program_id(2)
    tq = q_ref.shape[1]
    n = k_ref.shape[1]
    q = q_ref[0]
    k = k_ref[0][:, 0:HEAD_DIM]
    v = v_ref[0][:, 0:HEAD_DIM]
    gates = jax.nn.sigmoid(gate_ref[0])
    pos = qi * tq + lax.broadcasted_iota(jnp.int32, (tq, 1), 0)
    cmp_end = lax.broadcasted_iota(jnp.int32, (1, n), 1) * NSA_CMP_STRIDE + (NSA_CMP_LEN - 1)
    visible = cmp_end <= pos
    p_sum = jnp.zeros((tq, n), F32)
    outs = []
    for r in range(NSA_GROUP):
        s = jnp.where(visible, _dot_nt(q[:, r * HEAD_DIM:(r + 1) * HEAD_DIM], k), NEG_INF)
        m = jnp.max(s, axis=-1, keepdims=True)
        e = jnp.exp(s - jnp.where(m > NEG_INF, m, 0.0))
        den = jnp.sum(e, axis=-1, keepdims=True)
        p = e / jnp.where(den > 0, den, 1.0)
        p_sum = p_sum + p
        outs.append(_dot(p, v) * gates[:, r:r + 1])
    o_ref[0] = jnp.concatenate(outs, axis=-1)

    imp = _dot_split_lhs(p_sum, c2s_ref[...])
    blk = _lane_iota(imp.shape)
    cur = pos // NSA_SEL_BLOCK
    forced = (blk == 0) | (blk == cur) | (blk == cur - 1)
    valid = (blk <= cur) & (blk < n_sel)
    score = jnp.where(valid, jnp.where(forced, jnp.inf, imp), NEG_INF)
    taken = _take_top(score, blk, n_top)
    bias = jnp.where(blk < n_sel, jnp.where(taken & valid, 0.0, -1e30), 0.0)
    bias = pltpu.roll(bias, HEAD_DIM, 1)
    low = blk < HEAD_DIM
    qf = q.astype(F32)
    for r in range(NSA_GROUP):
        x = qf[:, (r // 2) * LANES:(r // 2 + 1) * LANES]
        if r % 2 == 1:
            x = pltpu.roll(x, HEAD_DIM, 1)
        qa_ref[0, :, r * LANES:(r + 1) * LANES] = jnp.where(low, x, bias).astype(BF16)


def _cmp_attn(q3, kcmp, vcmp, gate3, c2s, n_sel):
    B, S, _ = q3.shape
    n = kcmp.shape[1]
    tq = ATTN_TQ
    G = NSA_KV_HEADS
    kern = functools.partial(_cmp_attn_kernel, n_sel=n_sel, n_top=min(NSA_SEL_TOPN, n_sel))
    return pl.pallas_call(
        kern,
        grid=(B, G, S // tq),
        in_specs=[pl.BlockSpec((1, tq, 256), lambda b, g, i: (b, i, g)),
                  pl.BlockSpec((1, n, LANES), lambda b, g, i: (b * G + g, 0, 0)),
                  pl.BlockSpec((1, n, LANES), lambda b, g, i: (b * G + g, 0, 0)),
                  pl.BlockSpec((1, tq, LANES), lambda b, g, i: (b, i, g)),
                  pl.BlockSpec(c2s.shape, lambda b, g, i: (0, 0))],
        out_specs=[pl.BlockSpec((1, tq, 256), lambda b, g, i: (b, i, g)),
                   pl.BlockSpec((1, tq, 512), lambda b, g, i: (b, i, g))],
        out_shape=[jax.ShapeDtypeStruct((B, S, 512), F32),
                   jax.ShapeDtypeStruct((B, S, NSA_HEADS * LANES), BF16)],
        compiler_params=_cparams("parallel", "parallel", "parallel"),
    )(q3, kcmp, vcmp, gate3, c2s)


def _flash_kernel(*refs, heads, window, gate_cols):
    it = iter(refs)
    q_ref, k_ref, v_ref = next(it), next(it), next(it)
    gate_ref = next(it) if gate_cols is not None else None
    o_ref = next(it)
    s_sc, mx_sc, mb_sc, acc_sc = next(it), next(it), next(it), next(it)

    qi = pl.program_id(2)
    tq = q_ref.shape[1]
    tk = tq
    nh = len(heads)

    lane = _lane_iota((tq, LANES))
    low = lane < HEAD_DIM
    q_heads = [q_ref[0, :, g * LANES:(g + 1) * LANES] for (g, _, _) in heads]
    mx_sc[...] = jnp.full(mx_sc.shape, NEG_INF, F32)
    acc_sc[...] = jnp.zeros(acc_sc.shape, F32)

    def score_tile(j, pos_mask):
        start = pl.multiple_of(j * tk, tk)
        k = k_ref[0, pl.ds(start, tk), :]
        for h, (_, kg, _) in enumerate(heads):
            s = _dot_nt(q_heads[h], k[:, kg * LANES:(kg + 1) * LANES])
            if pos_mask is not None:
                s = jnp.where(pos_mask, s, NEG_INF)
            s_sc[h, j] = s
            best = s[:, 0:LANES]
            for c in range(1, tk // LANES):
                best = jnp.maximum(best, s[:, c * LANES:(c + 1) * LANES])
            mx_sc[h] = jnp.maximum(mx_sc[h], best)

    def value_tile(j):
        start = pl.multiple_of(j * tk, tk)
        v = v_ref[0, pl.ds(start, tk), :]
        for h, (_, _, vg) in enumerate(heads):
            mb = mb_sc[h]
            s = s_sc[h, j]
            p = jnp.concatenate([jnp.exp(s[:, c * LANES:(c + 1) * LANES] - mb)
                                 for c in range(tk // LANES)], axis=-1).astype(BF16)
            acc_sc[h] += jnp.dot(p, v[:, vg * LANES:(vg + 1) * LANES], preferred_element_type=F32)

    def for_each_tile(fn_full, fn_masked):
        row = lax.broadcasted_iota(jnp.int32, (tq, tk), 0)
        col = lax.broadcasted_iota(jnp.int32, (tq, tk), 1)
        first_full = 0
        if window is not None:
            back = window // tk
            first_full = jnp.maximum(qi - back + 1, 0)

            @pl.when(qi >= back)
            def _():
                fn_masked(qi - back, col > row)

        def body(j, carry):
            fn_full(j)
            return carry

        lax.fori_loop(first_full, qi, body, 0)
        fn_masked(qi, col <= row)

    for_each_tile(lambda j: score_tile(j, None), score_tile)
    for h in range(nh):
        m = jnp.max(mx_sc[h], axis=-1, keepdims=True)
        mb_sc[h] = jnp.broadcast_to(m, (tq, LANES))
    for_each_tile(value_tile, lambda j, mask: value_tile(j))

    if gate_ref is not None:
        gates = jax.nn.sigmoid(gate_ref[0])
    results = []
    for h in range(nh):
        acc = acc_sc[h]
        o = acc / acc[:, HEAD_DIM:HEAD_DIM + 1]
        if gate_ref is not None:
            c = gate_cols[h]
            o = o * gates[:, c:c + 1]
        results.append(o)
    for pair in range(nh // 2):
        high = pltpu.roll(results[2 * pair + 1], HEAD_DIM, 1)
        o_ref[0, :, pair * LANES:(pair + 1) * LANES] = jnp.where(low, results[2 * pair], high).astype(o_ref.dtype)


def _flash(q, k, v, gate, *, n_steps, q_w, k_w, v_w, heads, window=None, gate_cols=None,
           out_dtype=F32):
    B, S, _ = q.shape
    tq = ATTN_TQ
    nh = len(heads)
    assert nh % 2 == 0 and S % tq == 0 and (window is None or window % tq == 0)
    n_stash = S // tq
    in_specs = [pl.BlockSpec((1, tq, q_w), lambda b, h, i: (b, i, h)),
                pl.BlockSpec((1, S, k_w), lambda b, h, i: (b, 0, h)),
                pl.BlockSpec((1, S, v_w), lambda b, h, i: (b, 0, h))]
    args = [q, k, v]
    if gate is not None:
        in_specs.append(pl.BlockSpec((1, tq, LANES), lambda b, h, i: (b, i, h)))
        args.append(gate)
    kern = functools.partial(_flash_kernel, heads=tuple(heads), window=window, gate_cols=gate_cols)
    return pl.pallas_call(
        kern,
        grid=(B, n_steps, S // tq),
        in_specs=in_specs,
        out_specs=pl.BlockSpec((1, tq, nh * HEAD_DIM), lambda b, h, i: (b, i, h)),
        out_shape=jax.ShapeDtypeStruct((B, S, n_steps * nh * HEAD_DIM), out_dtype),
        scratch_shapes=[pltpu.VMEM((nh, n_stash, tq, tq), F32), pltpu.VMEM((nh, tq, LANES), F32),
                        pltpu.VMEM((nh, tq, LANES), F32), pltpu.VMEM((nh, tq, LANES), F32)],
        compiler_params=_cparams("parallel", "parallel", "arbitrary"),
    )(*args)


def _outproj_kernel(*refs, group_sizes):
    it = iter(refs)
    y = None
    for n_in in group_sizes:
        acts = [next(it)[...].astype(F32) for _ in range(n_in)]
        w_ref = next(it)
        a = acts[0]
        for extra in acts[1:]:
            a = a + extra
        part = _dot(a, w_ref[...])
        y = part if y is None else y + part
    x_ref, g_ref, b_ref, o_ref = next(it), next(it), next(it), next(it)
    o_ref[...] = _layer_norm(DN_ALPHA * x_ref[...] + y, g_ref[...], b_ref[...])


def _outproj_ln(groups, x2, g, b):
    T = x2.shape[0]
    tm = ROW_TILE
    in_specs, args, sizes = [], [], []
    for acts, w in groups:
        for a in acts:
            in_specs.append(pl.BlockSpec((tm, a.shape[1]), lambda i: (i, 0)))
            args.append(a)
        in_specs.append(pl.BlockSpec(w.shape, lambda i: (0, 0)))
        args.append(w)
        sizes.append(len(acts))
    in_specs += [pl.BlockSpec((tm, D_MODEL), lambda i: (i, 0)),
                 pl.BlockSpec((1, D_MODEL), lambda i: (0, 0)),
                 pl.BlockSpec((1, D_MODEL), lambda i: (0, 0))]
    args += [x2, g.reshape(1, -1), b.reshape(1, -1)]
    return pl.pallas_call(
        functools.partial(_outproj_kernel, group_sizes=tuple(sizes)),
        grid=(T // tm,),
        in_specs=in_specs,
        out_specs=pl.BlockSpec((tm, D_MODEL), lambda i: (i, 0)),
        out_shape=jax.ShapeDtypeStruct((T, D_MODEL), F32),
        compiler_params=_cparams("parallel"),
    )(*args)


def _router_kernel(x_ref, w_ref, bias_ref, g_ref):
    logits = _dot_split_both(x_ref[...], w_ref[...])
    lane = _lane_iota(logits.shape)
    real = lane < N_EXPERTS
    scores = jax.nn.sigmoid(logits)
    choice = jnp.where(real, scores + bias_ref[...], NEG_INF)
    top2 = jnp.where(_group_rank(choice, GROUP_SIZE) < 2, choice, 0.0)
    grp_score = jnp.where(real, _group_sum(top2, GROUP_SIZE), NEG_INF)
    grp_taken = _take_top(grp_score, lane // GROUP_SIZE, TOPK_GROUPS)
    masked = jnp.where(grp_taken & real, choice, NEG_INF)
    taken = _take_top(masked, lane, TOP_K)
    w = jnp.where(taken, scores, 0.0)
    g_ref[...] = w / jnp.sum(w, axis=-1, keepdims=True) * ROUTED_SCALE


def _router(x2, w_router_pad, bias_pad):
    T = x2.shape[0]
    tm = ROW_TILE
    return pl.pallas_call(
        _router_kernel,
        grid=(T // tm,),
        in_specs=[pl.BlockSpec((tm, D_MODEL), lambda i: (i, 0)),
                  pl.BlockSpec(w_router_pad.shape, lambda i: (0, 0)),
                  pl.BlockSpec((1, LANES), lambda i: (0, 0))],
        out_specs=pl.BlockSpec((tm, LANES), lambda i: (i, 0)),
        out_shape=jax.ShapeDtypeStruct((T, LANES), F32),
        compiler_params=_cparams("parallel"),
    )(x2, w_router_pad, bias_pad)


def _moe_kernel(x_ref, gates_ref, wg_ref, wu_ref, wd_ref, g_ref, b_ref, o_ref, xb_sc, acc_sc):
    e = pl.program_id(1)
    last = pl.num_programs(1) - 1

    @pl.when(e == 0)
    def _():
        xb_sc[...] = x_ref[...].astype(BF16)
        acc_sc[...] = jnp.zeros(acc_sc.shape, F32)

    xb = xb_sc[...]
    hg = jnp.dot(xb, wg_ref[0], preferred_element_type=F32)
    hu = jnp.dot(xb, wu_ref[0], preferred_element_type=F32)
    y = _dot(jax.nn.silu(hg) * hu, wd_ref[0])
    gates = gates_ref[...]
    col = jnp.sum(jnp.where(_lane_iota(gates.shape) == e, gates, 0.0), axis=-1, keepdims=True)
    acc_sc[...] += y * jnp.where(e == last, 1.0, col)

    @pl.when(e == last)
    def _():
        o_ref[...] = _layer_norm(DN_ALPHA * x_ref[...] + acc_sc[...], g_ref[...], b_ref[...])


def _moe_ln(x2, gates, wg, wu, wd, g, b):
    T = x2.shape[0]
    tm = MOE_TILE
    ne = wg.shape[0]
    return pl.pallas_call(
        _moe_kernel,
        grid=(T // tm, ne),
        in_specs=[pl.BlockSpec((tm, D_MODEL), lambda i, e: (i, 0)),
                  pl.BlockSpec((tm, LANES), lambda i, e: (i, 0)),
                  pl.BlockSpec((1, D_MODEL, EXPERT_FF), lambda i, e: (e, 0, 0)),
                  pl.BlockSpec((1, D_MODEL, EXPERT_FF), lambda i, e: (e, 0, 0)),
                  pl.BlockSpec((1, EXPERT_FF, D_MODEL), lambda i, e: (e, 0, 0)),
                  pl.BlockSpec((1, D_MODEL), lambda i, e: (0, 0)),
                  pl.BlockSpec((1, D_MODEL), lambda i, e: (0, 0))],
        out_specs=pl.BlockSpec((tm, D_MODEL), lambda i, e: (i, 0)),
        out_shape=jax.ShapeDtypeStruct((T, D_MODEL), F32),
        scratch_shapes=[pltpu.VMEM((tm, D_MODEL), BF16), pltpu.VMEM((tm, D_MODEL), F32)],
        compiler_params=_cparams("parallel", "arbitrary"),
    )(x2, gates, wg, wu, wd, g.reshape(1, -1), b.reshape(1, -1))


def _moe_block(x2, router, router_bias, exp_gate, exp_up, exp_down, sh_gate, sh_up, sh_down, g, b):
    w_router_pad = jnp.pad(router, ((0, 0), (0, LANES - N_EXPERTS)))
    bias_pad = jnp.pad(router_bias, (0, LANES - N_EXPERTS)).reshape(1, LANES)
    gates = _router(x2, w_router_pad, bias_pad)
    wg = jnp.concatenate([exp_gate, sh_gate[None]], 0).astype(BF16)
    wu = jnp.concatenate([exp_up, sh_up[None]], 0).astype(BF16)
    wd = jnp.concatenate([exp_down, sh_down[None]], 0).astype(BF16)
    return _moe_ln(x2, gates, wg, wu, wd, g, b)


def _qkv1_kernel(x_ref, w_ref, tab_ref, q_ref, k_ref, v_ref, kmean_ref, *, seq_tiles):
    tm = x_ref.shape[0]
    xb = x_ref[...].astype(BF16)
    width = MOBA_HEADS * HEAD_DIM
    lane = _lane_iota((tm, LANES))
    low = lane < HEAD_DIM
    pair_first = (lane % HEAD_DIM) < (HEAD_DIM // 2)
    cos_p, sin_p = tab_ref[0], tab_ref[1]
    pos = (pl.program_id(0) % seq_tiles) * tm + lax.broadcasted_iota(jnp.int32, (tm, 1), 0)
    onehot = _block_onehot(pos, MOBA_BLOCK, (tm, LANES))
    one_lane = jnp.where(lane == HEAD_DIM, 1.0, 0.0)
    for j in range(width // LANES):
        c = j * LANES
        q = jnp.dot(xb, w_ref[:, c:c + LANES], preferred_element_type=F32)
        q = _rope_lanes(q, cos_p, sin_p, pair_first, 32)
        q_ref[:, c:c + LANES] = (q * (HEAD_DIM ** -0.5)).astype(BF16)
        k = jnp.dot(xb, w_ref[:, width + c:width + c + LANES], preferred_element_type=F32)
        k = _rope_lanes(k, cos_p, sin_p, pair_first, 32)
        k_ref[:, 2 * c:2 * c + LANES] = jnp.where(low, k, onehot).astype(BF16)
        k_ref[:, 2 * c + LANES:2 * c + 2 * LANES] = jnp.where(
            low, pltpu.roll(k, HEAD_DIM, 1), onehot).astype(BF16)
        kmean_ref[0, :, c:c + LANES] = jnp.mean(k, axis=0, keepdims=True)
        v = jnp.dot(xb, w_ref[:, 2 * width + c:2 * width + c + LANES], preferred_element_type=F32)
        v_ref[:, 2 * c:2 * c + LANES] = jnp.where(low, v, one_lane).astype(BF16)
        v_ref[:, 2 * c + LANES:2 * c + 2 * LANES] = jnp.where(
            low, pltpu.roll(v, HEAD_DIM, 1), one_lane).astype(BF16)


def _qkv1(x2, w_qkv, tabs, S):
    T = x2.shape[0]
    tm = MOBA_BLOCK
    ns = S // tm
    width = MOBA_HEADS * HEAD_DIM
    row = lambda w: pl.BlockSpec((tm, w), lambda i: (i, 0))
    return pl.pallas_call(
        functools.partial(_qkv1_kernel, seq_tiles=ns),
        grid=(T // tm,),
        in_specs=[pl.BlockSpec((tm, D_MODEL), lambda i: (i, 0)),
                  pl.BlockSpec(w_qkv.shape, lambda i: (0, 0)),
                  pl.BlockSpec((2, tm, LANES), lambda i: (0, i % ns, 0))],
        out_specs=[row(width), row(2 * width), row(2 * width),
                   pl.BlockSpec((1, 1, width), lambda i: (i, 0, 0))],
        out_shape=[jax.ShapeDtypeStruct((T, width), BF16), jax.ShapeDtypeStruct((T, 2 * width), BF16),
                   jax.ShapeDtypeStruct((T, 2 * width), BF16),
                   jax.ShapeDtypeStruct((T // tm, 1, width), F32)],
        compiler_params=_cparams("parallel"),
    )(x2, w_qkv, tabs)


def _moba_select_kernel(q_ref, km_ref, qa_ref, *, n_top, n_blocks):
    i = pl.program_id(1)
    tq = q_ref.shape[1]
    gate = _dot_split_lhs_rhs(q_ref[0], km_ref[0])
    lane = _lane_iota(gate.shape)
    blk = lane % 8
    own = (i * tq + lax.broadcasted_iota(jnp.int32, (tq, 1), 0)) // MOBA_BLOCK
    score = jnp.where((blk < own) & (blk < n_blocks), gate, NEG_INF)
    rank = _group_rank(score, 8)
    chosen = ((score > NEG_INF) & (rank < n_top)) | (blk == own)
    bias = jnp.where(chosen, 0.0, -1e30)
    low = lane < HEAD_DIM
    in_bias = (lane >= HEAD_DIM) & (lane < HEAD_DIM + 8)
    for pair in range(MOBA_HEADS // 2):
        qf = q_ref[0, :, pair * LANES:(pair + 1) * LANES].astype(F32)
        for half in range(2):
            h = 2 * pair + half
            x = qf if half == 0 else pltpu.roll(qf, HEAD_DIM, 1)
            b = jnp.where(in_bias, pltpu.roll(bias, (HEAD_DIM - 8 * h) % LANES, 1), 0.0)
            qa_ref[0, :, h * LANES:(h + 1) * LANES] = jnp.where(low, x, b).astype(BF16)


def _dot_split_lhs_rhs(q_bf16, b):
    hi, lo = _split(b)
    return (jnp.dot(q_bf16, hi, preferred_element_type=F32)
            + jnp.dot(q_bf16, lo, preferred_element_type=F32))


def _moba_select(q3, km, n_top, n_blocks):
    B, S, width = q3.shape
    tq = ATTN_TQ
    kern = functools.partial(_moba_select_kernel, n_top=n_top, n_blocks=n_blocks)
    return pl.pallas_call(
        kern,
        grid=(B, S // tq),
        in_specs=[pl.BlockSpec((1, tq, width), lambda b, i: (b, i, 0)),
                  pl.BlockSpec((1, width, LANES), lambda b, i: (b, 0, 0))],
        out_specs=pl.BlockSpec((1, tq, MOBA_HEADS * LANES), lambda b, i: (b, i, 0)),
        out_shape=jax.ShapeDtypeStruct((B, S, MOBA_HEADS * LANES), BF16),
        compiler_params=_cparams("parallel", "parallel"),
    )(q3, km)


def _layer0(x2, B, S, w_in, pe_k, pe_v, cmp_k1, cmp_k2, cmp_v1, cmp_v2,
            q_norm, w_uq, kv_norm, w_ukv, w_out, ln_g, ln_b):
    T = B * S
    G = NSA_KV_HEADS
    w_perm = _permute_columns(w_in, _proj0_columns()).astype(BF16)
    tabs = _rope_tables(S)
    qd = MLA_NOPE_DIM + MLA_ROPE_DIM
    wuq = jnp.pad(w_uq.reshape(MLA_Q_RANK, MLA_HEADS, qd), ((0, 0), (0, 0), (0, LANES - qd)))
    wuq = wuq.reshape(MLA_Q_RANK, MLA_HEADS * LANES).astype(BF16)
    wukv = w_ukv.reshape(MLA_KV_RANK, MLA_HEADS, MLA_NOPE_DIM + MLA_V_DIM)
    wk = jnp.pad(wukv[:, :, :MLA_NOPE_DIM], ((0, 0), (0, 0), (0, LANES - MLA_NOPE_DIM)))
    wk = wk.reshape(MLA_KV_RANK, MLA_HEADS * LANES).astype(BF16)
    wv = jnp.pad(wukv[:, :, MLA_NOPE_DIM:], ((0, 0), (0, 0), (0, LANES - MLA_V_DIM)))
    wv = wv.reshape(MLA_KV_RANK, MLA_HEADS * LANES).astype(BF16)

    q, ks, vs, kw, vw, kc, vc, gate, qm, km, vm = _proj0(
        x2, w_perm, tabs, q_norm.reshape(1, -1), kv_norm.reshape(1, -1), wuq, wk, wv, S)

    n_chunks = S // NSA_CMP_STRIDE
    chunk_w = NSA_CMP_STRIDE * HEAD_DIM

    def chunks(t):
        t = t.reshape(B, S, G, HEAD_DIM).transpose(0, 2, 1, 3)
        return t.reshape(B * G, n_chunks, chunk_w)

    pe = jnp.stack([pe_k[:NSA_CMP_STRIDE].reshape(-1), pe_k[NSA_CMP_STRIDE:].reshape(-1),
                    pe_v[:NSA_CMP_STRIDE].reshape(-1), pe_v[NSA_CMP_STRIDE:].reshape(-1)])
    pad2 = lambda w: jnp.pad(w, ((0, 0), (0, LANES - HEAD_DIM))).astype(BF16)
    kcmp, vcmp = _compress(chunks(kc), chunks(vc), pe, cmp_k1.astype(BF16), pad2(cmp_k2),
                           cmp_v1.astype(BF16), pad2(cmp_v2), _cmp_rope_tables(n_chunks))

    n_sel = S // NSA_SEL_BLOCK
    n_cmp = (S - NSA_CMP_LEN) // NSA_CMP_STRIDE + 1
    tok = np.arange(n_chunks)[:, None] * NSA_CMP_STRIDE + np.arange(NSA_CMP_LEN)[None, :]
    c2s = (tok[:, :, None] // NSA_SEL_BLOCK == np.arange(LANES)[None, None, :]).sum(1) / NSA_CMP_LEN
    c2s[n_cmp:] = 0.0
    c2s = jnp.asarray(c2s, dtype=BF16)

    q3 = q.reshape(B, S, -1)
    gate3 = gate.reshape(B, S, -1)
    o_cmp, q_aug = _cmp_attn(q3, kcmp, vcmp, gate3, c2s, n_sel)

    r3 = lambda t: t.reshape(B, S, -1)
    nsa_heads = [(r, 0, 0) for r in range(NSA_GROUP)]
    o_sel = _flash(q_aug, r3(ks), r3(vs), gate3, n_steps=G, q_w=NSA_GROUP * LANES, k_w=LANES,
                   v_w=LANES, heads=nsa_heads, gate_cols=[NSA_GROUP + r for r in range(NSA_GROUP)])
    o_win = _flash(q_aug, r3(kw), r3(vw), gate3, n_steps=G, q_w=NSA_GROUP * LANES, k_w=LANES,
                   v_w=LANES, heads=nsa_heads, window=NSA_WINDOW,
                   gate_cols=[2 * NSA_GROUP + r for r in range(NSA_GROUP)])
    o_mla = _flash(r3(qm), r3(km), r3(vm), None, n_steps=MLA_HEADS // FLASH_HEADS,
                   q_w=FLASH_HEADS * LANES, k_w=FLASH_HEADS * LANES, v_w=FLASH_HEADS * LANES,
                   heads=_OWN_KV_HEADS, out_dtype=BF16)

    n_nsa = NSA_HEADS * HEAD_DIM
    w_out_b = w_out.astype(BF16)
    groups = [([o_cmp.reshape(T, -1), o_sel.reshape(T, -1), o_win.reshape(T, -1)], w_out_b[:n_nsa]),
              ([o_mla.reshape(T, -1)], w_out_b[n_nsa:])]
    return _outproj_ln(groups, x2, ln_g, ln_b)


def _layer1(x2, B, S, w_qkv, w_out, ln_g, ln_b):
    T = B * S
    width = MOBA_HEADS * HEAD_DIM
    tabs = _rope_tables(S)[0:2]
    q, k, v, kmean = _qkv1(x2, w_qkv.astype(BF16), tabs, S)
    n_blocks = S // MOBA_BLOCK
    n_top = min(MOBA_TOPK, max(n_blocks - 1, 1))
    km = kmean.reshape(B, n_blocks, width).transpose(0, 2, 1)
    km = jnp.pad(km, ((0, 0), (0, 0), (0, 8 - n_blocks)))
    km = jnp.tile(km, (1, 1, MOBA_HEADS))
    diag = (np.arange(width)[:, None] // HEAD_DIM) == (np.arange(LANES)[None, :] // 8)
    km = jnp.where(jnp.asarray(diag)[None], km, 0.0)
    q3 = q.reshape(B, S, width)
    q_aug = _moba_select(q3, km, n_top, n_blocks)
    o = _flash(q_aug, k.reshape(B, S, 2 * width), v.reshape(B, S, 2 * width), None,
               n_steps=MOBA_HEADS // FLASH_HEADS, q_w=FLASH_HEADS * LANES, k_w=FLASH_HEADS * LANES,
               v_w=FLASH_HEADS * LANES, heads=_OWN_KV_HEADS, out_dtype=BF16)
    return _outproj_ln([([o.reshape(T, width)], w_out.astype(BF16))], x2, ln_g, ln_b)


def kernel(x, l0_w_in, l0_nsa_pe_k, l0_nsa_pe_v, l0_nsa_cmp_k1, l0_nsa_cmp_k2, l0_nsa_cmp_v1, l0_nsa_cmp_v2, l0_mla_q_norm, l0_mla_w_uq, l0_mla_kv_norm, l0_mla_w_ukv, l0_w_out, l0_ln1_g, l0_ln1_b, l0_router, l0_router_bias, l0_exp_gate, l0_exp_up, l0_exp_down, l0_sh_gate, l0_sh_up, l0_sh_down, l0_ln2_g, l0_ln2_b, l1_w_qkv, l1_w_out, l1_ln1_g, l1_ln1_b, l1_router, l1_router_bias, l1_exp_gate, l1_exp_up, l1_exp_down, l1_sh_gate, l1_sh_up, l1_sh_down, l1_ln2_g, l1_ln2_b):
    B, S, D = x.shape
    x2 = x.reshape(B * S, D)
    x2 = _layer0(x2, B, S, l0_w_in, l0_nsa_pe_k, l0_nsa_pe_v, l0_nsa_cmp_k1, l0_nsa_cmp_k2,
                 l0_nsa_cmp_v1, l0_nsa_cmp_v2, l0_mla_q_norm, l0_mla_w_uq, l0_mla_kv_norm,
                 l0_mla_w_ukv, l0_w_out, l0_ln1_g, l0_ln1_b)
    x2 = _moe_block(x2, l0_router, l0_router_bias, l0_exp_gate, l0_exp_up, l0_exp_down,
                    l0_sh_gate, l0_sh_up, l0_sh_down, l0_ln2_g, l0_ln2_b)
    x2 = _layer1(x2, B, S, l1_w_qkv, l1_w_out, l1_ln1_g, l1_ln1_b)
    x2 = _moe_block(x2, l1_router, l1_router_bias, l1_exp_gate, l1_exp_up, l1_exp_down,
                    l1_sh_gate, l1_sh_up, l1_sh_down, l1_ln2_g, l1_ln2_b)
    return x2.reshape(B, S, D)
```

```python
import functools

import numpy as np
import jax
import jax.numpy as jnp
from jax import lax
from jax.experimental import pallas as pl
from jax.experimental.pallas import tpu as pltpu

F32 = jnp.float32
BF16 = jnp.bfloat16

LANES = 128
VMEM_LIMIT = 48 * 1024 * 1024

D_MODEL = 1024
DEPTH = 2
HEAD_DIM = 64
ROPE_THETA = 10000.0
LN_EPS = 1e-5
RMS_EPS = 1e-6

NSA_HEADS = 8
NSA_KV_HEADS = 2
NSA_GROUP = NSA_HEADS // NSA_KV_HEADS
NSA_CMP_LEN = 32
NSA_CMP_STRIDE = 16
NSA_CMP_HIDDEN = 128
NSA_SEL_BLOCK = 64
NSA_SEL_TOPN = 8
NSA_WINDOW = 512

MLA_HEADS = 8
MLA_Q_RANK = 256
MLA_KV_RANK = 128
MLA_NOPE_DIM = 64
MLA_ROPE_DIM = 32
MLA_V_DIM = 64

MOBA_HEADS = 16
MOBA_BLOCK = 256
MOBA_TOPK = 3

N_EXPERTS = 64
N_GROUPS = 8
GROUP_SIZE = N_EXPERTS // N_GROUPS
TOPK_GROUPS = 4
TOP_K = 8
EXPERT_FF = 256
ROUTED_SCALE = 2.5

DN_ALPHA = (2 * DEPTH) ** 0.25

ROW_TILE = 256
ATTN_TQ = 256
ATTN_TK = 256
MOE_ROWS = 256
FLASH_HEADS = 4
_OWN_KV_HEADS = tuple((h, h, h) for h in range(FLASH_HEADS))

NEG_INF = float("-inf")


def _cparams(*sem):
    return pltpu.CompilerParams(dimension_semantics=sem, vmem_limit_bytes=VMEM_LIMIT)


def _dot(a, b):
    return jnp.dot(a.astype(BF16), b.astype(BF16), preferred_element_type=F32)


def _dot_nt(a, b):
    return lax.dot_general(a.astype(BF16), b.astype(BF16), (((1,), (1,)), ((), ())),
                           preferred_element_type=F32)


def _split(a):
    hi = a.astype(BF16)
    lo = (a - hi.astype(F32)).astype(BF16)
    return hi, lo


def _dot_split_lhs(a, b_bf16):
    hi, lo = _split(a)
    return (jnp.dot(hi, b_bf16, preferred_element_type=F32)
            + jnp.dot(lo, b_bf16, preferred_element_type=F32))


def _dot_split_both(a, b):
    ah, al = _split(a)
    bh, bl = _split(b)
    return (jnp.dot(ah, bh, preferred_element_type=F32)
            + jnp.dot(al, bh, preferred_element_type=F32)
            + jnp.dot(ah, bl, preferred_element_type=F32))


def _lane_iota(shape):
    return lax.broadcasted_iota(jnp.int32, shape, len(shape) - 1)


def _rope_lanes(x, cos, sin, first_half, half):
    n = x.shape[-1]
    rot = jnp.where(first_half, -pltpu.roll(x, n - half, 1), pltpu.roll(x, half, 1))
    return x * cos + rot * sin


def _layer_norm(z, g, b):
    mu = jnp.mean(z, axis=-1, keepdims=True)
    zc = z - mu
    var = jnp.mean(zc * zc, axis=-1, keepdims=True)
    return zc * lax.rsqrt(var + LN_EPS) * g + b


def _rms_norm(x, g):
    return x * lax.rsqrt(jnp.mean(x * x, axis=-1, keepdims=True) + RMS_EPS) * g


def _group_rank(x, group):
    n = x.shape[-1]
    pos = _lane_iota(x.shape) % group
    rank = jnp.zeros(x.shape, F32)
    for d in range(1, group):
        lower = pltpu.roll(x, d, 1)
        upper = pltpu.roll(x, n - d, 1)
        rank = rank + jnp.where((pos >= d) & (lower >= x), 1.0, 0.0)
        rank = rank + jnp.where((pos + d < group) & (upper > x), 1.0, 0.0)
    return rank


def _group_sum(x, group):
    n = x.shape[-1]
    pos = _lane_iota(x.shape) % group
    tot = x
    for d in range(1, group):
        tot = tot + jnp.where(pos >= d, pltpu.roll(x, d, 1), 0.0)
        tot = tot + jnp.where(pos + d < group, pltpu.roll(x, n - d, 1), 0.0)
    return tot


def _take_top(x, key, k):
    big = jnp.int32(1 << 30)
    taken = jnp.zeros(x.shape, jnp.bool_)
    for _ in range(k):
        m = jnp.max(x, axis=-1, keepdims=True)
        first = jnp.min(jnp.where(x == m, key, big), axis=-1, keepdims=True)
        hit = key == first
        taken = taken | hit
        x = jnp.where(hit, NEG_INF, x)
    return taken


def _rope_tables(S):
    pos = jnp.arange(S, dtype=F32)[:, None]

    def cs(half):
        inv = ROPE_THETA ** (-jnp.arange(half, dtype=F32) / half)
        ang = pos * inv[None, :]
        return jnp.cos(ang), jnp.sin(ang)

    c32, s32 = cs(HEAD_DIM // 2)
    c16, s16 = cs(MLA_ROPE_DIM // 2)
    one = jnp.ones((S, 1), F32)
    zero = jnp.zeros((S, 1), F32)
    cos_pair = jnp.concatenate([c32] * 4, -1)
    sin_pair = jnp.concatenate([s32] * 4, -1)
    cos_kv = jnp.concatenate([c32, c32, jnp.tile(one, (1, 64))], -1)
    sin_kv = jnp.concatenate([s32, s32, jnp.tile(zero, (1, 64))], -1)
    cos_mla = jnp.concatenate([jnp.tile(one, (1, 64)), c16, c16, jnp.tile(one, (1, 32))], -1)
    sin_mla = jnp.concatenate([jnp.tile(zero, (1, 64)), s16, s16, jnp.tile(zero, (1, 32))], -1)
    return jnp.stack([cos_pair, sin_pair, cos_kv, sin_kv, cos_mla, sin_mla])


def _cmp_rope_tables(n_chunks):
    pos = (jnp.arange(n_chunks, dtype=F32) * NSA_CMP_STRIDE + (NSA_CMP_LEN - 1))[:, None]
    half = HEAD_DIM // 2
    inv = ROPE_THETA ** (-jnp.arange(half, dtype=F32) / half)
    ang = pos * inv[None, :]
    c, s = jnp.cos(ang), jnp.sin(ang)
    one = jnp.ones((n_chunks, 64), F32)
    return jnp.stack([jnp.concatenate([c, c, one], -1), jnp.concatenate([s, s, 0 * one], -1)])


P0_Q = 0
P0_KVS = 512
P0_KVW = 768
P0_KC = 1024
P0_VC = 1152
P0_GATE = 1280
P0_CQ = 1536
P0_CKV = 1792
P0_KR = 1920
P0_W = 2048


def _proj0_columns():
    q_w = NSA_HEADS * HEAD_DIM
    kv0 = q_w
    piece = NSA_KV_HEADS * HEAD_DIM
    gate0 = kv0 + 6 * piece
    cq0 = gate0 + 3 * NSA_HEADS
    ckv0 = cq0 + MLA_Q_RANK
    kr0 = ckv0 + MLA_KV_RANK
    src = -np.ones((P0_W,), np.int64)
    src[P0_Q:P0_Q + q_w] = np.arange(q_w)
    d = np.arange(HEAD_DIM)
    for base, kp, vp in ((P0_KVS, 2, 3), (P0_KVW, 4, 5)):
        for g in range(NSA_KV_HEADS):
            src[base + g * 128 + d] = kv0 + kp * piece + g * HEAD_DIM + d
            src[base + g * 128 + 64 + d] = kv0 + vp * piece + g * HEAD_DIM + d
    src[P0_KC:P0_KC + piece] = kv0 + 0 * piece + np.arange(piece)
    src[P0_VC:P0_VC + piece] = kv0 + 1 * piece + np.arange(piece)
    for g in range(NSA_KV_HEADS):
        for br in range(3):
            for r in range(NSA_GROUP):
                src[P0_GATE + g * 128 + br * NSA_GROUP + r] = gate0 + br * NSA_HEADS + g * NSA_GROUP + r
    src[P0_CQ:P0_CQ + MLA_Q_RANK] = cq0 + np.arange(MLA_Q_RANK)
    src[P0_CKV:P0_CKV + MLA_KV_RANK] = ckv0 + np.arange(MLA_KV_RANK)
    src[P0_KR + 64:P0_KR + 64 + MLA_ROPE_DIM] = kr0 + np.arange(MLA_ROPE_DIM)
    return src


def _permute_columns(w, src):
    cols = jnp.take(w, jnp.asarray(np.maximum(src, 0)), axis=1)
    return jnp.where(jnp.asarray(src >= 0)[None, :], cols, 0.0)


def _block_onehot(pos, block, shape):
    return jnp.where(_lane_iota(shape) == HEAD_DIM + pos // block, 1.0, 0.0)


def _proj0_kernel(x_ref, w_ref, tab_ref, qg_ref, kvg_ref, wuq_ref, wk_ref, wv_ref,
                  q_ref, ks_ref, vs_ref, kw_ref, vw_ref, kc_ref, vc_ref, gate_ref,
                  qm_ref, km_ref, vm_ref, *, seq_tiles):
    tm = x_ref.shape[0]
    xb = x_ref[...].astype(BF16)
    lane = _lane_iota((tm, LANES))
    low = lane < HEAD_DIM
    pair_first = (lane % HEAD_DIM) < (HEAD_DIM // 2)
    mla_first = lane < (MLA_NOPE_DIM + MLA_ROPE_DIM // 2)
    cos_p, sin_p = tab_ref[0], tab_ref[1]
    cos_kv, sin_kv = tab_ref[2], tab_ref[3]
    cos_m, sin_m = tab_ref[4], tab_ref[5]
    pos = (pl.program_id(0) % seq_tiles) * tm + lax.broadcasted_iota(jnp.int32, (tm, 1), 0)
    sel_onehot = _block_onehot(pos, NSA_SEL_BLOCK, (tm, LANES))
    one_lane = jnp.where(lane == HEAD_DIM, 1.0, 0.0)

    def seg(c0, width):
        return jnp.dot(xb, w_ref[:, c0:c0 + width], preferred_element_type=F32)

    q = seg(P0_Q, 512)
    for j in range(4):
        blk = _rope_lanes(q[:, j * 128:(j + 1) * 128], cos_p, sin_p, pair_first, 32)
        q_ref[:, j * 128:(j + 1) * 128] = (blk * (HEAD_DIM ** -0.5)).astype(BF16)
    for c0, k_out, v_out, extra in ((P0_KVS, ks_ref, vs_ref, sel_onehot), (P0_KVW, kw_ref, vw_ref, 0.0)):
        kv = seg(c0, 256)
        for j in range(2):
            blk = _rope_lanes(kv[:, j * 128:(j + 1) * 128], cos_kv, sin_kv, pair_first, 32)
            k_out[:, j * 128:(j + 1) * 128] = jnp.where(low, blk, extra).astype(BF16)
            v_out[:, j * 128:(j + 1) * 128] = jnp.where(low, pltpu.roll(blk, HEAD_DIM, 1), one_lane).astype(BF16)
    kc_ref[...] = seg(P0_KC, 128)
    vc_ref[...] = seg(P0_VC, 128)
    gate_ref[...] = seg(P0_GATE, 256)

    cq = _rms_norm(seg(P0_CQ, 256), qg_ref[...])
    qm = _dot(cq, wuq_ref[...])
    mla_scale = (MLA_NOPE_DIM + MLA_ROPE_DIM) ** -0.5
    for h in range(MLA_HEADS):
        blk = _rope_lanes(qm[:, h * 128:(h + 1) * 128], cos_m, sin_m, mla_first, 16)
        qm_ref[:, h * 128:(h + 1) * 128] = (blk * mla_scale).astype(BF16)
    ckv = _rms_norm(seg(P0_CKV, 128), kvg_ref[...]).astype(BF16)
    kn = jnp.dot(ckv, wk_ref[...], preferred_element_type=F32)
    kpe = _rope_lanes(seg(P0_KR, 128), cos_m, sin_m, mla_first, 16)
    for h in range(MLA_HEADS):
        km_ref[:, h * 128:(h + 1) * 128] = (kn[:, h * 128:(h + 1) * 128] + kpe).astype(BF16)
    vm = jnp.dot(ckv, wv_ref[...], preferred_element_type=F32)
    for h in range(MLA_HEADS):
        vm_ref[:, h * 128:(h + 1) * 128] = jnp.where(low, vm[:, h * 128:(h + 1) * 128], one_lane).astype(BF16)


def _proj0(x2, w_perm, tabs, q_norm, kv_norm, wuq, wk, wv, S):
    T = x2.shape[0]
    tm = ROW_TILE
    ns = S // tm
    row = lambda w: pl.BlockSpec((tm, w), lambda i: (i, 0))
    full = lambda a: pl.BlockSpec(a.shape, lambda i: (0,) * a.ndim)
    widths = (512, 256, 256, 256, 256, 128, 128, 256, 1024, 1024, 1024)
    dtypes = (BF16, BF16, BF16, BF16, BF16, F32, F32, F32, BF16, BF16, BF16)
    return pl.pallas_call(
        functools.partial(_proj0_kernel, seq_tiles=ns),
        grid=(T // tm,),
        in_specs=[row(D_MODEL), full(w_perm),
                  pl.BlockSpec((6, tm, LANES), lambda i: (0, i % ns, 0)),
                  full(q_norm), full(kv_norm), full(wuq), full(wk), full(wv)],
        out_specs=[row(w) for w in widths],
        out_shape=[jax.ShapeDtypeStruct((T, w), d) for w, d in zip(widths, dtypes)],
        compiler_params=_cparams("parallel"),
    )(x2, w_perm, tabs, q_norm, kv_norm, wuq, wk, wv)


def _compress_kernel(kc_ref, vc_ref, pe_ref, w1k_ref, w2k_ref, w1v_ref, w2v_ref, tab_ref,
                     ko_ref, vo_ref):
    half = NSA_CMP_STRIDE * HEAD_DIM
    n = kc_ref.shape[1]

    def mlp(x, pe_lo, pe_hi, w1_ref, w2_ref):
        first = _dot(x + pe_lo, w1_ref[0:half, :])
        second = _dot(x + pe_hi, w1_ref[half:2 * half, :])
        hidden = first + pltpu.roll(second, n - 1, 0)
        return _dot(jax.nn.gelu(hidden), w2_ref[...])

    k = mlp(kc_ref[0], pe_ref[0:1, :], pe_ref[1:2, :], w1k_ref, w2k_ref)
    lane = _lane_iota(k.shape)
    ko_ref[0] = _rope_lanes(k, tab_ref[0], tab_ref[1], lane < HEAD_DIM // 2, 32).astype(BF16)
    vo_ref[0] = mlp(vc_ref[0], pe_ref[2:3, :], pe_ref[3:4, :], w1v_ref, w2v_ref).astype(BF16)


def _compress(kc_chunks, vc_chunks, pe, w1k, w2k, w1v, w2v, ctab):
    n_bg, n, width = kc_chunks.shape
    blk = pl.BlockSpec((1, n, width), lambda i: (i, 0, 0))
    full = lambda a: pl.BlockSpec(a.shape, lambda i: (0,) * a.ndim)
    out = pl.BlockSpec((1, n, LANES), lambda i: (i, 0, 0))
    return pl.pallas_call(
        _compress_kernel,
        grid=(n_bg,),
        in_specs=[blk, blk, full(pe), full(w1k), full(w2k), full(w1v), full(w2v), full(ctab)],
        out_specs=[out, out],
        out_shape=[jax.ShapeDtypeStruct((n_bg, n, LANES), BF16)] * 2,
        compiler_params=_cparams("parallel"),
    )(kc_chunks, vc_chunks, pe, w1k, w2k, w1v, w2v, ctab)


def _cmp_attn_kernel(q_ref, k_ref, v_ref, gate_ref, c2s_ref, o_ref, qa_ref, *, n_sel, n_top):
    qi = pl.program_id(2)
    tq = q_ref.shape[1]
    n = k_ref.shape[1]
    q = q_ref[0]
    k = k_ref[0][:, 0:HEAD_DIM]
    v = v_ref[0][:, 0:HEAD_DIM]
    gates = jax.nn.sigmoid(gate_ref[0])
    pos = qi * tq + lax.broadcasted_iota(jnp.int32, (tq, 1), 0)
    cmp_end = lax.broadcasted_iota(jnp.int32, (1, n), 1) * NSA_CMP_STRIDE + (NSA_CMP_LEN - 1)
    visible = cmp_end <= pos
    p_sum = jnp.zeros((tq, n), F32)
    outs = []
    for r in range(NSA_GROUP):
        s = jnp.where(visible, _dot_nt(q[:, r * HEAD_DIM:(r + 1) * HEAD_DIM], k), NEG_INF)
        m = jnp.max(s, axis=-1, keepdims=True)
        e = jnp.exp(s - jnp.where(m > NEG_INF, m, 0.0))
        den = jnp.sum(e, axis=-1, keepdims=True)
        p = e / jnp.where(den > 0, den, 1.0)
        p_sum = p_sum + p
        outs.append(_dot(p, v) * gates[:, r:r + 1])
    o_ref[0] = jnp.concatenate(outs, axis=-1)

    imp = _dot_split_lhs(p_sum, c2s_ref[...])
    blk = _lane_iota(imp.shape)
    cur = pos // NSA_SEL_BLOCK
    forced = (blk == 0) | (blk == cur) | (blk == cur - 1)
    valid = (blk <= cur) & (blk < n_sel)
    score = jnp.where(valid, jnp.where(forced, jnp.inf, imp), NEG_INF)
    taken = _take_top(score, blk, n_top)
    bias = jnp.where(blk < n_sel, jnp.where(taken & valid, 0.0, -1e30), 0.0)
    bias = pltpu.roll(bias, HEAD_DIM, 1)
    low = blk < HEAD_DIM
    qf = q.astype(F32)
    for r in range(NSA_GROUP):
        x = qf[:, (r // 2) * LANES:(r // 2 + 1) * LANES]
        if r % 2 == 1:
            x = pltpu.roll(x, HEAD_DIM, 1)
        qa_ref[0, :, r * LANES:(r + 1) * LANES] = jnp.where(low, x, bias).astype(BF16)


def _cmp_attn(q3, kcmp, vcmp, gate3, c2s, n_sel):
    B, S, _ = q3.shape
    n = kcmp.shape[1]
    tq = ATTN_TQ
    G = NSA_KV_HEADS
    kern = functools.partial(_cmp_attn_kernel, n_sel=n_sel, n_top=min(NSA_SEL_TOPN, n_sel))
    return pl.pallas_call(
        kern,
        grid=(B, G, S // tq),
        in_specs=[pl.BlockSpec((1, tq, 256), lambda b, g, i: (b, i, g)),
                  pl.BlockSpec((1, n, LANES), lambda b, g, i: (b * G + g, 0, 0)),
                  pl.BlockSpec((1, n, LANES), lambda b, g, i: (b * G + g, 0, 0)),
                  pl.BlockSpec((1, tq, LANES), lambda b, g, i: (b, i, g)),
                  pl.BlockSpec(c2s.shape, lambda b, g, i: (0, 0))],
        out_specs=[pl.BlockSpec((1, tq, 256), lambda b, g, i: (b, i, g)),
                   pl.BlockSpec((1, tq, 512), lambda b, g, i: (b, i, g))],
        out_shape=[jax.ShapeDtypeStruct((B, S, 512), F32),
                   jax.ShapeDtypeStruct((B, S, NSA_HEADS * LANES), BF16)],
        compiler_params=_cparams("parallel", "parallel", "parallel"),
    )(q3, kcmp, vcmp, gate3, c2s)


def _flash_kernel(*refs, heads, window, gate_cols):
    it = iter(refs)
    q_ref, k_ref, v_ref = next(it), next(it), next(it)
    gate_ref = next(it) if gate_cols is not None else None
    o_ref = next(it)
    s_sc, mx_sc, mb_sc, acc_sc = next(it), next(it), next(it), next(it)

    qi = pl.program_id(2)
    tq = q_ref.shape[1]
    tk = tq
    nh = len(heads)

    lane = _lane_iota((tq, LANES))
    low = lane < HEAD_DIM
    q_heads = [q_ref[0, :, g * LANES:(g + 1) * LANES] for (g, _, _) in heads]
    mx_sc[...] = jnp.full(mx_sc.shape, NEG_INF, F32)
    acc_sc[...] = jnp.zeros(acc_sc.shape, F32)

    def score_tile(j, pos_mask):
        start = pl.multiple_of(j * tk, tk)
        k = k_ref[0, pl.ds(start, tk), :]
        for h, (_, kg, _) in enumerate(heads):
            s = _dot_nt(q_heads[h], k[:, kg * LANES:(kg + 1) * LANES])
            if pos_mask is not None:
                s = jnp.where(pos_mask, s, NEG_INF)
            s_sc[h, j] = s
            best = s[:, 0:LANES]
            for c in range(1, tk // LANES):
                best = jnp.maximum(best, s[:, c * LANES:(c + 1) * LANES])
            mx_sc[h] = jnp.maximum(mx_sc[h], best)

    def value_tile(j):
        start = pl.multiple_of(j * tk, tk)
        v = v_ref[0, pl.ds(start, tk), :]
        for h, (_, _, vg) in enumerate(heads):
            mb = mb_sc[h]
            s = s_sc[h, j]
            p = jnp.concatenate([jnp.exp(s[:, c * LANES:(c + 1) * LANES] - mb)
                                 for c in range(tk // LANES)], axis=-1).astype(BF16)
            acc_sc[h] += jnp.dot(p, v[:, vg * LANES:(vg + 1) * LANES], preferred_element_type=F32)

    def for_each_tile(fn_full, fn_masked):
        row = lax.broadcasted_iota(jnp.int32, (tq, tk), 0)
        col = lax.broadcasted_iota(jnp.int32, (tq, tk), 1)
        first_full = 0
        if window is not None:
            back = window // tk
            first_full = jnp.maximum(qi - back + 1, 0)

            @pl.when(qi >= back)
            def _():
                fn_masked(qi - back, col > row)

        def body(j, carry):
            fn_full(j)
            return carry

        lax.fori_loop(first_full, qi, body, 0)
        fn_masked(qi, col <= row)

    for_each_tile(lambda j: score_tile(j, None), score_tile)
    for h in range(nh):
        m = jnp.max(mx_sc[h], axis=-1, keepdims=True)
        mb_sc[h] = jnp.broadcast_to(m, (tq, LANES))
    for_each_tile(value_tile, lambda j, mask: value_tile(j))

    if gate_ref is not None:
        gates = jax.nn.sigmoid(gate_ref[0])
    results = []
    for h in range(nh):
        acc = acc_sc[h]
        o = acc / acc[:, HEAD_DIM:HEAD_DIM + 1]
        if gate_ref is not None:
            c = gate_cols[h]
            o = o * gates[:, c:c + 1]
        results.append(o)
    for pair in range(nh // 2):
        high = pltpu.roll(results[2 * pair + 1], HEAD_DIM, 1)
        o_ref[0, :, pair * LANES:(pair + 1) * LANES] = jnp.where(low, results[2 * pair], high).astype(o_ref.dtype)


def _flash(q, k, v, gate, *, n_steps, q_w, k_w, v_w, heads, window=None, gate_cols=None,
           out_dtype=F32):
    B, S, _ = q.shape
    tq = ATTN_TQ
    nh = len(heads)
    assert nh % 2 == 0 and S % tq == 0 and (window is None or window % tq == 0)
    n_stash = S // tq
    in_specs = [pl.BlockSpec((1, tq, q_w), lambda b, h, i: (b, i, h)),
                pl.BlockSpec((1, S, k_w), lambda b, h, i: (b, 0, h)),
                pl.BlockSpec((1, S, v_w), lambda b, h, i: (b, 0, h))]
    args = [q, k, v]
    if gate is not None:
        in_specs.append(pl.BlockSpec((1, tq, LANES), lambda b, h, i: (b, i, h)))
        args.append(gate)
    kern = functools.partial(_flash_kernel, heads=tuple(heads), window=window, gate_cols=gate_cols)
    return pl.pallas_call(
        kern,
        grid=(B, n_steps, S // tq),
        in_specs=in_specs,
        out_specs=pl.BlockSpec((1, tq, nh * HEAD_DIM), lambda b, h, i: (b, i, h)),
        out_shape=jax.ShapeDtypeStruct((B, S, n_steps * nh * HEAD_DIM), out_dtype),
        scratch_shapes=[pltpu.VMEM((nh, n_stash, tq, tq), F32), pltpu.VMEM((nh, tq, LANES), F32),
                        pltpu.VMEM((nh, tq, LANES), F32), pltpu.VMEM((nh, tq, LANES), F32)],
        compiler_params=_cparams("parallel", "parallel", "arbitrary"),
    )(*args)


def _outproj_kernel(*refs, group_sizes):
    it = iter(refs)
    y = None
    for n_in in group_sizes:
        acts = [next(it)[...].astype(F32) for _ in range(n_in)]
        w_ref = next(it)
        a = acts[0]
        for extra in acts[1:]:
            a = a + extra
        part = _dot(a, w_ref[...])
        y = part if y is None else y + part
    x_ref, g_ref, b_ref, o_ref = next(it), next(it), next(it), next(it)
    o_ref[...] = _layer_norm(DN_ALPHA * x_ref[...] + y, g_ref[...], b_ref[...])


def _outproj_ln(groups, x2, g, b):
    T = x2.shape[0]
    tm = ROW_TILE
    in_specs, args, sizes = [], [], []
    for acts, w in groups:
        for a in acts:
            in_specs.append(pl.BlockSpec((tm, a.shape[1]), lambda i: (i, 0)))
            args.append(a)
        in_specs.append(pl.BlockSpec(w.shape, lambda i: (0, 0)))
        args.append(w)
        sizes.append(len(acts))
    in_specs += [pl.BlockSpec((tm, D_MODEL), lambda i: (i, 0)),
                 pl.BlockSpec((1, D_MODEL), lambda i: (0, 0)),
                 pl.BlockSpec((1, D_MODEL), lambda i: (0, 0))]
    args += [x2, g.reshape(1, -1), b.reshape(1, -1)]
    return pl.pallas_call(
        functools.partial(_outproj_kernel, group_sizes=tuple(sizes)),
        grid=(T // tm,),
        in_specs=in_specs,
        out_specs=pl.BlockSpec((tm, D_MODEL), lambda i: (i, 0)),
        out_shape=jax.ShapeDtypeStruct((T, D_MODEL), F32),
        compiler_params=_cparams("parallel"),
    )(*args)


def _router_kernel(x_ref, w_ref, bias_ref, tri_ref, e_ref, wsel_ref, rank_ref, cnt_ref, carry_sc):
    @pl.when(pl.program_id(0) == 0)
    def _():
        carry_sc[...] = jnp.zeros(carry_sc.shape, F32)

    logits = _dot_split_both(x_ref[...], w_ref[...])
    lane = _lane_iota(logits.shape)
    real = lane < N_EXPERTS
    scores = jax.nn.sigmoid(logits)
    choice = jnp.where(real, scores + bias_ref[...], NEG_INF)
    top2 = jnp.where(_group_rank(choice, GROUP_SIZE) < 2, choice, 0.0)
    grp_score = jnp.where(real, _group_sum(top2, GROUP_SIZE), NEG_INF)
    grp_taken = _take_top(grp_score, lane // GROUP_SIZE, TOPK_GROUPS)
    masked = jnp.where(grp_taken & real, choice, NEG_INF)

    big = jnp.int32(1 << 30)
    hits = []
    for _ in range(TOP_K):
        m = jnp.max(masked, axis=-1, keepdims=True)
        first = jnp.min(jnp.where(masked == m, lane, big), axis=-1, keepdims=True)
        hit = lane == first
        hits.append(hit)
        masked = jnp.where(hit, NEG_INF, masked)
    taken = hits[0]
    for hit in hits[1:]:
        taken = taken | hit
    w = jnp.where(taken, scores, 0.0)
    w = w / jnp.sum(w, axis=-1, keepdims=True) * ROUTED_SCALE

    chose = jnp.where(taken, 1.0, 0.0)
    rank = jnp.dot(tri_ref[...], chose.astype(BF16), preferred_element_type=F32) + carry_sc[...]
    carry_sc[...] += jnp.sum(chose, axis=0, keepdims=True)
    cnt_ref[...] = carry_sc[...]

    e_out = jnp.zeros(logits.shape, F32)
    w_out = jnp.zeros(logits.shape, F32)
    r_out = jnp.zeros(logits.shape, F32)
    lane_f = lane.astype(F32)
    for k, hit in enumerate(hits):
        pick = lambda v: jnp.sum(jnp.where(hit, v, 0.0), axis=-1, keepdims=True)
        slot = lane == k
        e_out = jnp.where(slot, pick(lane_f), e_out)
        w_out = jnp.where(slot, pick(w), w_out)
        r_out = jnp.where(slot, pick(rank), r_out)
    e_ref[...] = e_out
    wsel_ref[...] = w_out
    rank_ref[...] = r_out


def _router(x2, w_router_pad, bias_pad):
    T = x2.shape[0]
    tm = ROW_TILE
    tri = jnp.asarray(np.tril(np.ones((tm, tm), np.float32), -1), dtype=BF16)
    row = pl.BlockSpec((tm, LANES), lambda i: (i, 0))
    return pl.pallas_call(
        _router_kernel,
        grid=(T // tm,),
        in_specs=[pl.BlockSpec((tm, D_MODEL), lambda i: (i, 0)),
                  pl.BlockSpec(w_router_pad.shape, lambda i: (0, 0)),
                  pl.BlockSpec((1, LANES), lambda i: (0, 0)),
                  pl.BlockSpec((tm, tm), lambda i: (0, 0))],
        out_specs=[row, row, row, pl.BlockSpec((1, LANES), lambda i: (0, 0))],
        out_shape=[jax.ShapeDtypeStruct((T, LANES), F32)] * 3 + [jax.ShapeDtypeStruct((1, LANES), F32)],
        scratch_shapes=[pltpu.VMEM((1, LANES), F32)],
        compiler_params=_cparams("arbitrary"),
    )(x2, w_router_pad, bias_pad, tri)


def _expert_kernel(blk_expert_ref, n_used_ref, tok_first_ref, tok_next_ref, x_hbm,
                   wg_ref, wu_ref, wd_ref, y_ref, xbuf, sem):
    del blk_expert_ref
    i = pl.program_id(0)
    n_used = n_used_ref[0]
    rows = xbuf.shape[1]

    def start_rows(tok_ref, slot):
        for r in range(rows):
            pltpu.make_async_copy(x_hbm.at[pl.ds(tok_ref[0, 0, r], 1), :],
                                  xbuf.at[slot, pl.ds(r, 1), :], sem.at[slot]).start()

    @pl.when((i == 0) & (n_used > 0))
    def _():
        start_rows(tok_first_ref, 0)

    @pl.when(i + 1 < n_used)
    def _():
        start_rows(tok_next_ref, (i + 1) % 2)

    @pl.when(i < n_used)
    def _():
        slot = i % 2
        pltpu.make_async_copy(x_hbm.at[pl.ds(0, rows), :], xbuf.at[slot], sem.at[slot]).wait()
        xb = xbuf[slot].astype(BF16)
        hg = jnp.dot(xb, wg_ref[0], preferred_element_type=F32)
        hu = jnp.dot(xb, wu_ref[0], preferred_element_type=F32)
        y_ref[...] = _dot(jax.nn.silu(hg) * hu, wd_ref[0])

    @pl.when(i >= n_used)
    def _():
        y_ref[...] = jnp.zeros(y_ref.shape, F32)


def _experts(blk_expert, n_used, row_tok, x2, wg, wu, wd):
    n_blocks, _, rows = row_tok.shape
    tok_spec = lambda f: pl.BlockSpec((1, 1, rows), f, memory_space=pltpu.SMEM)
    w_spec = lambda shape: pl.BlockSpec((1,) + shape, lambda i, be, nu: (be[i], 0, 0))
    grid_spec = pltpu.PrefetchScalarGridSpec(
        num_scalar_prefetch=2,
        grid=(n_blocks,),
        in_specs=[tok_spec(lambda i, be, nu: (i, 0, 0)),
                  tok_spec(lambda i, be, nu: (jnp.minimum(i + 1, n_blocks - 1), 0, 0)),
                  pl.BlockSpec(memory_space=pl.ANY),
                  w_spec((D_MODEL, EXPERT_FF)), w_spec((D_MODEL, EXPERT_FF)),
                  w_spec((EXPERT_FF, D_MODEL))],
        out_specs=pl.BlockSpec((rows, D_MODEL), lambda i, be, nu: (i, 0)),
        scratch_shapes=[pltpu.VMEM((2, rows, D_MODEL), F32), pltpu.SemaphoreType.DMA((2,))])
    return pl.pallas_call(
        _expert_kernel,
        grid_spec=grid_spec,
        out_shape=jax.ShapeDtypeStruct((n_blocks * rows, D_MODEL), F32),
        compiler_params=_cparams("arbitrary"),
    )(blk_expert, n_used, row_tok, row_tok, x2, wg, wu, wd)


def _combine_kernel(idx_first_ref, idx_next_ref, y_hbm, x_ref, wsel_ref, wsg_ref, wsu_ref, wsd_ref,
                    g_ref, b_ref, o_ref, ybuf, sem):
    i = pl.program_id(0)
    n = pl.num_programs(0)
    tm = x_ref.shape[0]

    def start_rows(idx_ref, slot):
        def body(r, carry):
            for k in range(TOP_K):
                pltpu.make_async_copy(y_hbm.at[pl.ds(idx_ref[0, 0, k * tm + r], 1), :],
                                      ybuf.at[slot, k, pl.ds(r, 1), :], sem.at[slot]).start()
            return carry
        lax.fori_loop(0, tm, body, 0)

    @pl.when(i == 0)
    def _():
        start_rows(idx_first_ref, 0)

    @pl.when(i + 1 < n)
    def _():
        start_rows(idx_next_ref, (i + 1) % 2)

    slot = i % 2
    for k in range(TOP_K):
        pltpu.make_async_copy(y_hbm.at[pl.ds(0, tm), :], ybuf.at[slot, k], sem.at[slot]).wait()
    x = x_ref[...]
    xb = x.astype(BF16)
    shared = _dot(jax.nn.silu(jnp.dot(xb, wsg_ref[...], preferred_element_type=F32))
                  * jnp.dot(xb, wsu_ref[...], preferred_element_type=F32), wsd_ref[...])
    wsel = wsel_ref[...]
    routed = wsel[:, 0:1] * ybuf[slot, 0]
    for k in range(1, TOP_K):
        routed = routed + wsel[:, k:k + 1] * ybuf[slot, k]
    o_ref[...] = _layer_norm(DN_ALPHA * x + (routed + shared), g_ref[...], b_ref[...])


def _combine(idx, ys, x2, w_sel, wsg, wsu, wsd, g, b):
    T = x2.shape[0]
    tm = ROW_TILE
    n = T // tm
    idx_spec = lambda f: pl.BlockSpec((1, 1, TOP_K * tm), f, memory_space=pltpu.SMEM)
    full = lambda a: pl.BlockSpec(a.shape, lambda i: (0,) * a.ndim)
    return pl.pallas_call(
        _combine_kernel,
        grid=(n,),
        in_specs=[idx_spec(lambda i: (i, 0, 0)),
                  idx_spec(lambda i: (jnp.minimum(i + 1, n - 1), 0, 0)),
                  pl.BlockSpec(memory_space=pl.ANY),
                  pl.BlockSpec((tm, D_MODEL), lambda i: (i, 0)),
                  pl.BlockSpec((tm, LANES), lambda i: (i, 0)),
                  full(wsg), full(wsu), full(wsd),
                  pl.BlockSpec((1, D_MODEL), lambda i: (0, 0)),
                  pl.BlockSpec((1, D_MODEL), lambda i: (0, 0))],
        out_specs=pl.BlockSpec((tm, D_MODEL), lambda i: (i, 0)),
        out_shape=jax.ShapeDtypeStruct((T, D_MODEL), F32),
        scratch_shapes=[pltpu.VMEM((2, TOP_K, tm, D_MODEL), F32), pltpu.SemaphoreType.DMA((2,))],
        compiler_params=_cparams("arbitrary"),
    )(idx, idx, ys, x2, w_sel, wsg, wsu, wsd, g.reshape(1, -1), b.reshape(1, -1))


def _moe_block(x2, router, router_bias, exp_gate, exp_up, exp_down, sh_gate, sh_up, sh_down, g, b):
    T = x2.shape[0]
    rows = MOE_ROWS
    w_router_pad = jnp.pad(router, ((0, 0), (0, LANES - N_EXPERTS)))
    bias_pad = jnp.pad(router_bias, (0, LANES - N_EXPERTS)).reshape(1, LANES)
    e_sel, w_sel, rank, counts = _router(x2, w_router_pad, bias_pad)

    n_blocks = -(-(T * TOP_K + N_EXPERTS * (rows - 1)) // rows)
    counts = counts[0, :N_EXPERTS].astype(jnp.int32)
    padded = (counts + rows - 1) // rows * rows
    p_end = jnp.cumsum(padded)
    p_start = p_end - padded
    dest = jnp.take(p_start, e_sel[:, :TOP_K].astype(jnp.int32)) + rank[:, :TOP_K].astype(jnp.int32)
    tok = jnp.broadcast_to(jnp.arange(T, dtype=jnp.int32)[:, None], (T, TOP_K))
    row_tok = jnp.zeros((n_blocks * rows,), jnp.int32).at[dest.reshape(-1)].set(tok.reshape(-1))
    blk_expert = jnp.minimum(jnp.searchsorted(p_end, jnp.arange(n_blocks) * rows, side='right'),
                             N_EXPERTS - 1).astype(jnp.int32)
    n_used = (p_end[-1:] // rows).astype(jnp.int32)

    ys = _experts(blk_expert, n_used, row_tok.reshape(n_blocks, 1, rows), x2,
                  exp_gate.astype(BF16), exp_up.astype(BF16), exp_down.astype(BF16))
    tm = ROW_TILE
    idx = dest.reshape(T // tm, tm, TOP_K).transpose(0, 2, 1).reshape(T // tm, 1, TOP_K * tm)
    return _combine(idx, ys, x2, w_sel, sh_gate.astype(BF16), sh_up.astype(BF16),
                    sh_down.astype(BF16), g, b)


def _qkv1_kernel(x_ref, w_ref, tab_ref, q_ref, k_ref, v_ref, kmean_ref, *, seq_tiles):
    tm = x_ref.shape[0]
    xb = x_ref[...].astype(BF16)
    width = MOBA_HEADS * HEAD_DIM
    lane = _lane_iota((tm, LANES))
    low = lane < HEAD_DIM
    pair_first = (lane % HEAD_DIM) < (HEAD_DIM // 2)
    cos_p, sin_p = tab_ref[0], tab_ref[1]
    pos = (pl.program_id(0) % seq_tiles) * tm + lax.broadcasted_iota(jnp.int32, (tm, 1), 0)
    onehot = _block_onehot(pos, MOBA_BLOCK, (tm, LANES))
    one_lane = jnp.where(lane == HEAD_DIM, 1.0, 0.0)
    for j in range(width // LANES):
        c = j * LANES
        q = jnp.dot(xb, w_ref[:, c:c + LANES], preferred_element_type=F32)
        q = _rope_lanes(q, cos_p, sin_p, pair_first, 32)
        q_ref[:, c:c + LANES] = (q * (HEAD_DIM ** -0.5)).astype(BF16)
        k = jnp.dot(xb, w_ref[:, width + c:width + c + LANES], preferred_element_type=F32)
        k = _rope_lanes(k, cos_p, sin_p, pair_first, 32)
        k_ref[:, 2 * c:2 * c + LANES] = jnp.where(low, k, onehot).astype(BF16)
        k_ref[:, 2 * c + LANES:2 * c + 2 * LANES] = jnp.where(
            low, pltpu.roll(k, HEAD_DIM, 1), onehot).astype(BF16)
        kmean_ref[0, :, c:c + LANES] = jnp.mean(k, axis=0, keepdims=True)
        v = jnp.dot(xb, w_ref[:, 2 * width + c:2 * width + c + LANES], preferred_element_type=F32)
        v_ref[:, 2 * c:2 * c + LANES] = jnp.where(low, v, one_lane).astype(BF16)
        v_ref[:, 2 * c + LANES:2 * c + 2 * LANES] = jnp.where(
            low, pltpu.roll(v, HEAD_DIM, 1), one_lane).astype(BF16)


def _qkv1(x2, w_qkv, tabs, S):
    T = x2.shape[0]
    tm = MOBA_BLOCK
    ns = S // tm
    width = MOBA_HEADS * HEAD_DIM
    row = lambda w: pl.BlockSpec((tm, w), lambda i: (i, 0))
    return pl.pallas_call(
        functools.partial(_qkv1_kernel, seq_tiles=ns),
        grid=(T // tm,),
        in_specs=[pl.BlockSpec((tm, D_MODEL), lambda i: (i, 0)),
                  pl.BlockSpec(w_qkv.shape, lambda i: (0, 0)),
                  pl.BlockSpec((2, tm, LANES), lambda i: (0, i % ns, 0))],
        out_specs=[row(width), row(2 * width), row(2 * width),
                   pl.BlockSpec((1, 1, width), lambda i: (i, 0, 0))],
        out_shape=[jax.ShapeDtypeStruct((T, width), BF16), jax.ShapeDtypeStruct((T, 2 * width), BF16),
                   jax.ShapeDtypeStruct((T, 2 * width), BF16),
                   jax.ShapeDtypeStruct((T // tm, 1, width), F32)],
        compiler_params=_cparams("parallel"),
    )(x2, w_qkv, tabs)


def _moba_select_kernel(q_ref, km_ref, qa_ref, *, n_top, n_blocks):
    i = pl.program_id(1)
    tq = q_ref.shape[1]
    gate = _dot_split_lhs_rhs(q_ref[0], km_ref[0])
    lane = _lane_iota(gate.shape)
    blk = lane % 8
    own = (i * tq + lax.broadcasted_iota(jnp.int32, (tq, 1), 0)) // MOBA_BLOCK
    score = jnp.where((blk < own) & (blk < n_blocks), gate, NEG_INF)
    rank = _group_rank(score, 8)
    chosen = ((score > NEG_INF) & (rank < n_top)) | (blk == own)
    bias = jnp.where(chosen, 0.0, -1e30)
    low = lane < HEAD_DIM
    in_bias = (lane >= HEAD_DIM) & (lane < HEAD_DIM + 8)
    for pair in range(MOBA_HEADS // 2):
        qf = q_ref[0, :, pair * LANES:(pair + 1) * LANES].astype(F32)
        for half in range(2):
            h = 2 * pair + half
            x = qf if half == 0 else pltpu.roll(qf, HEAD_DIM, 1)
            b = jnp.where(in_bias, pltpu.roll(bias, (HEAD_DIM - 8 * h) % LANES, 1), 0.0)
            qa_ref[0, :, h * LANES:(h + 1) * LANES] = jnp.where(low, x, b).astype(BF16)


def _dot_split_lhs_rhs(q_bf16, b):
    hi, lo = _split(b)
    return (jnp.dot(q_bf16, hi, preferred_element_type=F32)
            + jnp.dot(q_bf16, lo, preferred_element_type=F32))


def _moba_select(q3, km, n_top, n_blocks):
    B, S, width = q3.shape
    tq = ATTN_TQ
    kern = functools.partial(_moba_select_kernel, n_top=n_top, n_blocks=n_blocks)
    return pl.pallas_call(
        kern,
        grid=(B, S // tq),
        in_specs=[pl.BlockSpec((1, tq, width), lambda b, i: (b, i, 0)),
                  pl.BlockSpec((1, width, LANES), lambda b, i: (b, 0, 0))],
        out_specs=pl.BlockSpec((1, tq, MOBA_HEADS * LANES), lambda b, i: (b, i, 0)),
        out_shape=jax.ShapeDtypeStruct((B, S, MOBA_HEADS * LANES), BF16),
        compiler_params=_cparams("parallel", "parallel"),
    )(q3, km)


def _layer0(x2, B, S, w_in, pe_k, pe_v, cmp_k1, cmp_k2, cmp_v1, cmp_v2,
            q_norm, w_uq, kv_norm, w_ukv, w_out, ln_g, ln_b):
    T = B * S
    G = NSA_KV_HEADS
    w_perm = _permute_columns(w_in, _proj0_columns()).astype(BF16)
    tabs = _rope_tables(S)
    qd = MLA_NOPE_DIM + MLA_ROPE_DIM
    wuq = jnp.pad(w_uq.reshape(MLA_Q_RANK, MLA_HEADS, qd), ((0, 0), (0, 0), (0, LANES - qd)))
    wuq = wuq.reshape(MLA_Q_RANK, MLA_HEADS * LANES).astype(BF16)
    wukv = w_ukv.reshape(MLA_KV_RANK, MLA_HEADS, MLA_NOPE_DIM + MLA_V_DIM)
    wk = jnp.pad(wukv[:, :, :MLA_NOPE_DIM], ((0, 0), (0, 0), (0, LANES - MLA_NOPE_DIM)))
    wk = wk.reshape(MLA_KV_RANK, MLA_HEADS * LANES).astype(BF16)
    wv = jnp.pad(wukv[:, :, MLA_NOPE_DIM:], ((0, 0), (0, 0), (0, LANES - MLA_V_DIM)))
    wv = wv.reshape(MLA_KV_RANK, MLA_HEADS * LANES).astype(BF16)

    q, ks, vs, kw, vw, kc, vc, gate, qm, km, vm = _proj0(
        x2, w_perm, tabs, q_norm.reshape(1, -1), kv_norm.reshape(1, -1), wuq, wk, wv, S)

    n_chunks = S // NSA_CMP_STRIDE
    chunk_w = NSA_CMP_STRIDE * HEAD_DIM

    def chunks(t):
        t = t.reshape(B, S, G, HEAD_DIM).transpose(0, 2, 1, 3)
        return t.reshape(B * G, n_chunks, chunk_w)

    pe = jnp.stack([pe_k[:NSA_CMP_STRIDE].reshape(-1), pe_k[NSA_CMP_STRIDE:].reshape(-1),
                    pe_v[:NSA_CMP_STRIDE].reshape(-1), pe_v[NSA_CMP_STRIDE:].reshape(-1)])
    pad2 = lambda w: jnp.pad(w, ((0, 0), (0, LANES - HEAD_DIM))).astype(BF16)
    kcmp, vcmp = _compress(chunks(kc), chunks(vc), pe, cmp_k1.astype(BF16), pad2(cmp_k2),
                           cmp_v1.astype(BF16), pad2(cmp_v2), _cmp_rope_tables(n_chunks))

    n_sel = S // NSA_SEL_BLOCK
    n_cmp = (S - NSA_CMP_LEN) // NSA_CMP_STRIDE + 1
    tok = np.arange(n_chunks)[:, None] * NSA_CMP_STRIDE + np.arange(NSA_CMP_LEN)[None, :]
    c2s = (tok[:, :, None] // NSA_SEL_BLOCK == np.arange(LANES)[None, None, :]).sum(1) / NSA_CMP_LEN
    c2s[n_cmp:] = 0.0
    c2s = jnp.asarray(c2s, dtype=BF16)

    q3 = q.reshape(B, S, -1)
    gate3 = gate.reshape(B, S, -1)
    o_cmp, q_aug = _cmp_attn(q3, kcmp, vcmp, gate3, c2s, n_sel)

    r3 = lambda t: t.reshape(B, S, -1)
    nsa_heads = [(r, 0, 0) for r in range(NSA_GROUP)]
    o_sel = _flash(q_aug, r3(ks), r3(vs), gate3, n_steps=G, q_w=NSA_GROUP * LANES, k_w=LANES,
                   v_w=LANES, heads=nsa_heads, gate_cols=[NSA_GROUP + r for r in range(NSA_GROUP)])
    o_win = _flash(q_aug, r3(kw), r3(vw), gate3, n_steps=G, q_w=NSA_GROUP * LANES, k_w=LANES,
                   v_w=LANES, heads=nsa_heads, window=NSA_WINDOW,
                   gate_cols=[2 * NSA_GROUP + r for r in range(NSA_GROUP)])
    o_mla = _flash(r3(qm), r3(km), r3(vm), None, n_steps=MLA_HEADS // FLASH_HEADS,
                   q_w=FLASH_HEADS * LANES, k_w=FLASH_HEADS * LANES, v_w=FLASH_HEADS * LANES,
                   heads=_OWN_KV_HEADS, out_dtype=BF16)

    n_nsa = NSA_HEADS * HEAD_DIM
    w_out_b = w_out.astype(BF16)
    groups = [([o_cmp.reshape(T, -1), o_sel.reshape(T, -1), o_win.reshape(T, -1)], w_out_b[:n_nsa]),
              ([o_mla.reshape(T, -1)], w_out_b[n_nsa:])]
    return _outproj_ln(groups, x2, ln_g, ln_b)


def _layer1(x2, B, S, w_qkv, w_out, ln_g, ln_b):
    T = B * S
    width = MOBA_HEADS * HEAD_DIM
    tabs = _rope_tables(S)[0:2]
    q, k, v, kmean = _qkv1(x2, w_qkv.astype(BF16), tabs, S)
    n_blocks = S // MOBA_BLOCK
    n_top = min(MOBA_TOPK, max(n_blocks - 1, 1))
    km = kmean.reshape(B, n_blocks, width).transpose(0, 2, 1)
    km = jnp.pad(km, ((0, 0), (0, 0), (0, 8 - n_blocks)))
    km = jnp.tile(km, (1, 1, MOBA_HEADS))
    diag = (np.arange(width)[:, None] // HEAD_DIM) == (np.arange(LANES)[None, :] // 8)
    km = jnp.where(jnp.asarray(diag)[None], km, 0.0)
    q3 = q.reshape(B, S, width)
    q_aug = _moba_select(q3, km, n_top, n_blocks)
    o = _flash(q_aug, k.reshape(B, S, 2 * width), v.reshape(B, S, 2 * width), None,
               n_steps=MOBA_HEADS // FLASH_HEADS, q_w=FLASH_HEADS * LANES, k_w=FLASH_HEADS * LANES,
               v_w=FLASH_HEADS * LANES, heads=_OWN_KV_HEADS, out_dtype=BF16)
    return _outproj_ln([([o.reshape(T, width)], w_out.astype(BF16))], x2, ln_g, ln_b)


def kernel(x, l0_w_in, l0_nsa_pe_k, l0_nsa_pe_v, l0_nsa_cmp_k1, l0_nsa_cmp_k2, l0_nsa_cmp_v1, l0_nsa_cmp_v2, l0_mla_q_norm, l0_mla_w_uq, l0_mla_kv_norm, l0_mla_w_ukv, l0_w_out, l0_ln1_g, l0_ln1_b, l0_router, l0_router_bias, l0_exp_gate, l0_exp_up, l0_exp_down, l0_sh_gate, l0_sh_up, l0_sh_down, l0_ln2_g, l0_ln2_b, l1_w_qkv, l1_w_out, l1_ln1_g, l1_ln1_b, l1_router, l1_router_bias, l1_exp_gate, l1_exp_up, l1_exp_down, l1_sh_gate, l1_sh_up, l1_sh_down, l1_ln2_g, l1_ln2_b):
    B, S, D = x.shape
    x2 = x.reshape(B * S, D)
    x2 = _layer0(x2, B, S, l0_w_in, l0_nsa_pe_k, l0_nsa_pe_v, l0_nsa_cmp_k1, l0_nsa_cmp_k2,
                 l0_nsa_cmp_v1, l0_nsa_cmp_v2, l0_mla_q_norm, l0_mla_w_uq, l0_mla_kv_norm,
                 l0_mla_w_ukv, l0_w_out, l0_ln1_g, l0_ln1_b)
    x2 = _moe_block(x2, l0_router, l0_router_bias, l0_exp_gate, l0_exp_up, l0_exp_down,
                    l0_sh_gate, l0_sh_up, l0_sh_down, l0_ln2_g, l0_ln2_b)
    x2 = _layer1(x2, B, S, l1_w_qkv, l1_w_out, l1_ln1_g, l1_ln1_b)
    x2 = _moe_block(x2, l1_router, l1_router_bias, l1_exp_gate, l1_exp_up, l1_exp_down,
                    l1_sh_gate, l1_sh_up, l1_sh_down, l1_ln2_g, l1_ln2_b)
    return x2.reshape(B, S, D)
```

```python
import functools

import numpy as np
import jax
import jax.numpy as jnp
from jax import lax
from jax.experimental import pallas as pl
from jax.experimental.pallas import tpu as pltpu

F32 = jnp.float32
BF16 = jnp.bfloat16

LANES = 128
VMEM_LIMIT = 48 * 1024 * 1024

D_MODEL = 1024
DEPTH = 2
HEAD_DIM = 64
ROPE_THETA = 10000.0
LN_EPS = 1e-5
RMS_EPS = 1e-6

NSA_HEADS = 8
NSA_KV_HEADS = 2
NSA_GROUP = NSA_HEADS // NSA_KV_HEADS
NSA_CMP_LEN = 32
NSA_CMP_STRIDE = 16
NSA_CMP_HIDDEN = 128
NSA_SEL_BLOCK = 64
NSA_SEL_TOPN = 8
NSA_WINDOW = 512

MLA_HEADS = 8
MLA_Q_RANK = 256
MLA_KV_RANK = 128
MLA_NOPE_DIM = 64
MLA_ROPE_DIM = 32
MLA_V_DIM = 64

MOBA_HEADS = 16
MOBA_BLOCK = 256
MOBA_TOPK = 3

N_EXPERTS = 64
N_GROUPS = 8
GROUP_SIZE = N_EXPERTS // N_GROUPS
TOPK_GROUPS = 4
TOP_K = 8
EXPERT_FF = 256
ROUTED_SCALE = 2.5

DN_ALPHA = (2 * DEPTH) ** 0.25

ROW_TILE = 256
ATTN_TQ = 256
ATTN_TK = 256
MOE_TILE = 512
MOE_EXPERTS_PER_STEP = 5
FLASH_HEADS = 4
_OWN_KV_HEADS = tuple((h, h, h) for h in range(FLASH_HEADS))

NEG_INF = float("-inf")


def _cparams(*sem):
    return pltpu.CompilerParams(dimension_semantics=sem, vmem_limit_bytes=VMEM_LIMIT)


def _dot(a, b):
    return jnp.dot(a.astype(BF16), b.astype(BF16), preferred_element_type=F32)


def _dot_nt(a, b):
    return lax.dot_general(a.astype(BF16), b.astype(BF16), (((1,), (1,)), ((), ())),
                           preferred_element_type=F32)


def _split(a):
    hi = a.astype(BF16)
    lo = (a - hi.astype(F32)).astype(BF16)
    return hi, lo


def _dot_split_lhs(a, b_bf16):
    hi, lo = _split(a)
    return (jnp.dot(hi, b_bf16, preferred_element_type=F32)
            + jnp.dot(lo, b_bf16, preferred_element_type=F32))


def _dot_split_both(a, b):
    ah, al = _split(a)
    bh, bl = _split(b)
    return (jnp.dot(ah, bh, preferred_element_type=F32)
            + jnp.dot(al, bh, preferred_element_type=F32)
            + jnp.dot(ah, bl, preferred_element_type=F32))


def _lane_iota(shape):
    return lax.broadcasted_iota(jnp.int32, shape, len(shape) - 1)


def _rope_lanes(x, cos, sin, first_half, half):
    n = x.shape[-1]
    rot = jnp.where(first_half, -pltpu.roll(x, n - half, 1), pltpu.roll(x, half, 1))
    return x * cos + rot * sin


def _layer_norm(z, g, b):
    mu = jnp.mean(z, axis=-1, keepdims=True)
    zc = z - mu
    var = jnp.mean(zc * zc, axis=-1, keepdims=True)
    return zc * lax.rsqrt(var + LN_EPS) * g + b


def _rms_norm(x, g):
    return x * lax.rsqrt(jnp.mean(x * x, axis=-1, keepdims=True) + RMS_EPS) * g


def _group_rank(x, group):
    n = x.shape[-1]
    pos = _lane_iota(x.shape) % group
    rank = jnp.zeros(x.shape, F32)
    for d in range(1, group):
        lower = pltpu.roll(x, d, 1)
        upper = pltpu.roll(x, n - d, 1)
        rank = rank + jnp.where((pos >= d) & (lower >= x), 1.0, 0.0)
        rank = rank + jnp.where((pos + d < group) & (upper > x), 1.0, 0.0)
    return rank


def _group_sum(x, group):
    n = x.shape[-1]
    pos = _lane_iota(x.shape) % group
    tot = x
    for d in range(1, group):
        tot = tot + jnp.where(pos >= d, pltpu.roll(x, d, 1), 0.0)
        tot = tot + jnp.where(pos + d < group, pltpu.roll(x, n - d, 1), 0.0)
    return tot


def _take_top(x, key, k):
    big = jnp.int32(1 << 30)
    taken = jnp.zeros(x.shape, jnp.bool_)
    for _ in range(k):
        m = jnp.max(x, axis=-1, keepdims=True)
        first = jnp.min(jnp.where(x == m, key, big), axis=-1, keepdims=True)
        hit = key == first
        taken = taken | hit
        x = jnp.where(hit, NEG_INF, x)
    return taken


def _rope_tables(S):
    pos = jnp.arange(S, dtype=F32)[:, None]

    def cs(half):
        inv = ROPE_THETA ** (-jnp.arange(half, dtype=F32) / half)
        ang = pos * inv[None, :]
        return jnp.cos(ang), jnp.sin(ang)

    c32, s32 = cs(HEAD_DIM // 2)
    c16, s16 = cs(MLA_ROPE_DIM // 2)
    one = jnp.ones((S, 1), F32)
    zero = jnp.zeros((S, 1), F32)
    cos_pair = jnp.concatenate([c32] * 4, -1)
    sin_pair = jnp.concatenate([s32] * 4, -1)
    cos_kv = jnp.concatenate([c32, c32, jnp.tile(one, (1, 64))], -1)
    sin_kv = jnp.concatenate([s32, s32, jnp.tile(zero, (1, 64))], -1)
    cos_mla = jnp.concatenate([jnp.tile(one, (1, 64)), c16, c16, jnp.tile(one, (1, 32))], -1)
    sin_mla = jnp.concatenate([jnp.tile(zero, (1, 64)), s16, s16, jnp.tile(zero, (1, 32))], -1)
    return jnp.stack([cos_pair, sin_pair, cos_kv, sin_kv, cos_mla, sin_mla])


def _cmp_rope_tables(n_chunks):
    pos = (jnp.arange(n_chunks, dtype=F32) * NSA_CMP_STRIDE + (NSA_CMP_LEN - 1))[:, None]
    half = HEAD_DIM // 2
    inv = ROPE_THETA ** (-jnp.arange(half, dtype=F32) / half)
    ang = pos * inv[None, :]
    c, s = jnp.cos(ang), jnp.sin(ang)
    one = jnp.ones((n_chunks, 64), F32)
    return jnp.stack([jnp.concatenate([c, c, one], -1), jnp.concatenate([s, s, 0 * one], -1)])


P0_Q = 0
P0_KVS = 512
P0_KVW = 768
P0_KC = 1024
P0_VC = 1152
P0_GATE = 1280
P0_CQ = 1536
P0_CKV = 1792
P0_KR = 1920
P0_W = 2048


def _proj0_columns():
    q_w = NSA_HEADS * HEAD_DIM
    kv0 = q_w
    piece = NSA_KV_HEADS * HEAD_DIM
    gate0 = kv0 + 6 * piece
    cq0 = gate0 + 3 * NSA_HEADS
    ckv0 = cq0 + MLA_Q_RANK
    kr0 = ckv0 + MLA_KV_RANK
    src = -np.ones((P0_W,), np.int64)
    src[P0_Q:P0_Q + q_w] = np.arange(q_w)
    d = np.arange(HEAD_DIM)
    for base, kp, vp in ((P0_KVS, 2, 3), (P0_KVW, 4, 5)):
        for g in range(NSA_KV_HEADS):
            src[base + g * 128 + d] = kv0 + kp * piece + g * HEAD_DIM + d
            src[base + g * 128 + 64 + d] = kv0 + vp * piece + g * HEAD_DIM + d
    src[P0_KC:P0_KC + piece] = kv0 + 0 * piece + np.arange(piece)
    src[P0_VC:P0_VC + piece] = kv0 + 1 * piece + np.arange(piece)
    for g in range(NSA_KV_HEADS):
        for br in range(3):
            for r in range(NSA_GROUP):
                src[P0_GATE + g * 128 + br * NSA_GROUP + r] = gate0 + br * NSA_HEADS + g * NSA_GROUP + r
    src[P0_CQ:P0_CQ + MLA_Q_RANK] = cq0 + np.arange(MLA_Q_RANK)
    src[P0_CKV:P0_CKV + MLA_KV_RANK] = ckv0 + np.arange(MLA_KV_RANK)
    src[P0_KR + 64:P0_KR + 64 + MLA_ROPE_DIM] = kr0 + np.arange(MLA_ROPE_DIM)
    return src


def _permute_columns(w, src):
    cols = jnp.take(w, jnp.asarray(np.maximum(src, 0)), axis=1)
    return jnp.where(jnp.asarray(src >= 0)[None, :], cols, 0.0)


def _block_onehot(pos, block, shape):
    return jnp.where(_lane_iota(shape) == HEAD_DIM + pos // block, 1.0, 0.0)


def _proj0_kernel(x_ref, w_ref, tab_ref, qg_ref, kvg_ref, wuq_ref, wk_ref, wv_ref,
                  q_ref, ks_ref, vs_ref, kw_ref, vw_ref, kc_ref, vc_ref, gate_ref,
                  qm_ref, km_ref, vm_ref, *, seq_tiles):
    tm = x_ref.shape[0]
    xb = x_ref[...].astype(BF16)
    lane = _lane_iota((tm, LANES))
    low = lane < HEAD_DIM
    pair_first = (lane % HEAD_DIM) < (HEAD_DIM // 2)
    mla_first = lane < (MLA_NOPE_DIM + MLA_ROPE_DIM // 2)
    cos_p, sin_p = tab_ref[0], tab_ref[1]
    cos_kv, sin_kv = tab_ref[2], tab_ref[3]
    cos_m, sin_m = tab_ref[4], tab_ref[5]
    pos = (pl.program_id(0) % seq_tiles) * tm + lax.broadcasted_iota(jnp.int32, (tm, 1), 0)
    sel_onehot = _block_onehot(pos, NSA_SEL_BLOCK, (tm, LANES))
    one_lane = jnp.where(lane == HEAD_DIM, 1.0, 0.0)

    def seg(c0, width):
        return jnp.dot(xb, w_ref[:, c0:c0 + width], preferred_element_type=F32)

    q = seg(P0_Q, 512)
    for j in range(4):
        blk = _rope_lanes(q[:, j * 128:(j + 1) * 128], cos_p, sin_p, pair_first, 32)
        q_ref[:, j * 128:(j + 1) * 128] = (blk * (HEAD_DIM ** -0.5)).astype(BF16)
    for c0, k_out, v_out, extra in ((P0_KVS, ks_ref, vs_ref, sel_onehot), (P0_KVW, kw_ref, vw_ref, 0.0)):
        kv = seg(c0, 256)
        for j in range(2):
            blk = _rope_lanes(kv[:, j * 128:(j + 1) * 128], cos_kv, sin_kv, pair_first, 32)
            k_out[:, j * 128:(j + 1) * 128] = jnp.where(low, blk, extra).astype(BF16)
            v_out[:, j * 128:(j + 1) * 128] = jnp.where(low, pltpu.roll(blk, HEAD_DIM, 1), one_lane).astype(BF16)
    kc_ref[...] = seg(P0_KC, 128)
    vc_ref[...] = seg(P0_VC, 128)
    gate_ref[...] = seg(P0_GATE, 256)

    cq = _rms_norm(seg(P0_CQ, 256), qg_ref[...])
    qm = _dot(cq, wuq_ref[...])
    mla_scale = (MLA_NOPE_DIM + MLA_ROPE_DIM) ** -0.5
    for h in range(MLA_HEADS):
        blk = _rope_lanes(qm[:, h * 128:(h + 1) * 128], cos_m, sin_m, mla_first, 16)
        qm_ref[:, h * 128:(h + 1) * 128] = (blk * mla_scale).astype(BF16)
    ckv = _rms_norm(seg(P0_CKV, 128), kvg_ref[...]).astype(BF16)
    kn = jnp.dot(ckv, wk_ref[...], preferred_element_type=F32)
    kpe = _rope_lanes(seg(P0_KR, 128), cos_m, sin_m, mla_first, 16)
    for h in range(MLA_HEADS):
        km_ref[:, h * 128:(h + 1) * 128] = (kn[:, h * 128:(h + 1) * 128] + kpe).astype(BF16)
    vm = jnp.dot(ckv, wv_ref[...], preferred_element_type=F32)
    for h in range(MLA_HEADS):
        vm_ref[:, h * 128:(h + 1) * 128] = jnp.where(low, vm[:, h * 128:(h + 1) * 128], one_lane).astype(BF16)


def _proj0(x2, w_perm, tabs, q_norm, kv_norm, wuq, wk, wv, S):
    T = x2.shape[0]
    tm = ROW_TILE
    ns = S // tm
    row = lambda w: pl.BlockSpec((tm, w), lambda i: (i, 0))
    full = lambda a: pl.BlockSpec(a.shape, lambda i: (0,) * a.ndim)
    widths = (512, 256, 256, 256, 256, 128, 128, 256, 1024, 1024, 1024)
    dtypes = (BF16, BF16, BF16, BF16, BF16, F32, F32, F32, BF16, BF16, BF16)
    return pl.pallas_call(
        functools.partial(_proj0_kernel, seq_tiles=ns),
        grid=(T // tm,),
        in_specs=[row(D_MODEL), full(w_perm),
                  pl.BlockSpec((6, tm, LANES), lambda i: (0, i % ns, 0)),
                  full(q_norm), full(kv_norm), full(wuq), full(wk), full(wv)],
        out_specs=[row(w) for w in widths],
        out_shape=[jax.ShapeDtypeStruct((T, w), d) for w, d in zip(widths, dtypes)],
        compiler_params=_cparams("parallel"),
    )(x2, w_perm, tabs, q_norm, kv_norm, wuq, wk, wv)


def _compress_kernel(kc_ref, vc_ref, pe_ref, w1k_ref, w2k_ref, w1v_ref, w2v_ref, tab_ref,
                     ko_ref, vo_ref):
    half = NSA_CMP_STRIDE * HEAD_DIM
    n = kc_ref.shape[1]

    def mlp(x, pe_lo, pe_hi, w1_ref, w2_ref):
        first = _dot(x + pe_lo, w1_ref[0:half, :])
        second = _dot(x + pe_hi, w1_ref[half:2 * half, :])
        hidden = first + pltpu.roll(second, n - 1, 0)
        return _dot(jax.nn.gelu(hidden), w2_ref[...])

    k = mlp(kc_ref[0], pe_ref[0:1, :], pe_ref[1:2, :], w1k_ref, w2k_ref)
    lane = _lane_iota(k.shape)
    ko_ref[0] = _rope_lanes(k, tab_ref[0], tab_ref[1], lane < HEAD_DIM // 2, 32).astype(BF16)
    vo_ref[0] = mlp(vc_ref[0], pe_ref[2:3, :], pe_ref[3:4, :], w1v_ref, w2v_ref).astype(BF16)


def _compress(kc_chunks, vc_chunks, pe, w1k, w2k, w1v, w2v, ctab):
    n_bg, n, width = kc_chunks.shape
    blk = pl.BlockSpec((1, n, width), lambda i: (i, 0, 0))
    full = lambda a: pl.BlockSpec(a.shape, lambda i: (0,) * a.ndim)
    out = pl.BlockSpec((1, n, LANES), lambda i: (i, 0, 0))
    return pl.pallas_call(
        _compress_kernel,
        grid=(n_bg,),
        in_specs=[blk, blk, full(pe), full(w1k), full(w2k), full(w1v), full(w2v), full(ctab)],
        out_specs=[out, out],
        out_shape=[jax.ShapeDtypeStruct((n_bg, n, LANES), BF16)] * 2,
        compiler_params=_cparams("parallel"),
    )(kc_chunks, vc_chunks, pe, w1k, w2k, w1v, w2v, ctab)


def _cmp_attn_kernel(q_ref, k_ref, v_ref, gate_ref, c2s_ref, o_ref, qa_ref, *, n_sel, n_top):
    qi = pl.program_id(2)
    tq = q_ref.shape[1]
    n = k_ref.shape[1]
    q = q_ref[0]
    k = k_ref[0][:, 0:HEAD_DIM]
    v = v_ref[0][:, 0:HEAD_DIM]
    gates = jax.nn.sigmoid(gate_ref[0])
    pos = qi * tq + lax.broadcasted_iota(jnp.int32, (tq, 1), 0)
    cmp_end = lax.broadcasted_iota(jnp.int32, (1, n), 1) * NSA_CMP_STRIDE + (NSA_CMP_LEN - 1)
    visible = cmp_end <= pos
    p_sum = jnp.zeros((tq, n), F32)
    outs = []
    for r in range(NSA_GROUP):
        s = jnp.where(visible, _dot_nt(q[:, r * HEAD_DIM:(r + 1) * HEAD_DIM], k), NEG_INF)
        m = jnp.max(s, axis=-1, keepdims=True)
        e = jnp.exp(s - jnp.where(m > NEG_INF, m, 0.0))
        den = jnp.sum(e, axis=-1, keepdims=True)
        p = e / jnp.where(den > 0, den, 1.0)
        p_sum = p_sum + p
        outs.append(_dot(p, v) * gates[:, r:r + 1])
    o_ref[0] = jnp.concatenate(outs, axis=-1)

    imp = _dot_split_lhs(p_sum, c2s_ref[...])
    blk = _lane_iota(imp.shape)
    cur = pos // NSA_SEL_BLOCK
    forced = (blk == 0) | (blk == cur) | (blk == cur - 1)
    valid = (blk <= cur) & (blk < n_sel)
    score = jnp.where(valid, jnp.where(forced, jnp.inf, imp), NEG_INF)
    taken = _take_top(score, blk, n_top)
    bias = jnp.where(blk < n_sel, jnp.where(taken & valid, 0.0, -1e30), 0.0)
    bias = pltpu.roll(bias, HEAD_DIM, 1)
    low = blk < HEAD_DIM
    qf = q.astype(F32)
    for r in range(NSA_GROUP):
        x = qf[:, (r // 2) * LANES:(r // 2 + 1) * LANES]
        if r % 2 == 1:
            x = pltpu.roll(x, HEAD_DIM, 1)
        qa_ref[0, :, r * LANES:(r + 1) * LANES] = jnp.where(low, x, bias).astype(BF16)


def _cmp_attn(q3, kcmp, vcmp, gate3, c2s, n_sel):
    B, S, _ = q3.shape
    n = kcmp.shape[1]
    tq = ATTN_TQ
    G = NSA_KV_HEADS
    kern = functools.partial(_cmp_attn_kernel, n_sel=n_sel, n_top=min(NSA_SEL_TOPN, n_sel))
    return pl.pallas_call(
        kern,
        grid=(B, G, S // tq),
        in_specs=[pl.BlockSpec((1, tq, 256), lambda b, g, i: (b, i, g)),
                  pl.BlockSpec((1, n, LANES), lambda b, g, i: (b * G + g, 0, 0)),
                  pl.BlockSpec((1, n, LANES), lambda b, g, i: (b * G + g, 0, 0)),
                  pl.BlockSpec((1, tq, LANES), lambda b, g, i: (b, i, g)),
                  pl.BlockSpec(c2s.shape, lambda b, g, i: (0, 0))],
        out_specs=[pl.BlockSpec((1, tq, 256), lambda b, g, i: (b, i, g)),
                   pl.BlockSpec((1, tq, 512), lambda b, g, i: (b, i, g))],
        out_shape=[jax.ShapeDtypeStruct((B, S, 512), F32),
                   jax.ShapeDtypeStruct((B, S, NSA_HEADS * LANES), BF16)],
        compiler_params=_cparams("parallel", "parallel", "parallel"),
    )(q3, kcmp, vcmp, gate3, c2s)


def _flash_kernel(*refs, heads, window, gate_cols):
    it = iter(refs)
    q_ref, k_ref, v_ref = next(it), next(it), next(it)
    gate_ref = next(it) if gate_cols is not None else None
    o_ref = next(it)
    s_sc, mx_sc, mb_sc, acc_sc = next(it), next(it), next(it), next(it)

    qi = pl.program_id(2)
    tq = q_ref.shape[1]
    tk = tq
    nh = len(heads)

    lane = _lane_iota((tq, LANES))
    low = lane < HEAD_DIM
    q_heads = [q_ref[0, :, g * LANES:(g + 1) * LANES] for (g, _, _) in heads]
    mx_sc[...] = jnp.full(mx_sc.shape, NEG_INF, F32)
    acc_sc[...] = jnp.zeros(acc_sc.shape, F32)

    def score_tile(j, pos_mask):
        start = pl.multiple_of(j * tk, tk)
        k = k_ref[0, pl.ds(start, tk), :]
        for h, (_, kg, _) in enumerate(heads):
            s = _dot_nt(q_heads[h], k[:, kg * LANES:(kg + 1) * LANES])
            if pos_mask is not None:
                s = jnp.where(pos_mask, s, NEG_INF)
            s_sc[h, j] = s
            best = s[:, 0:LANES]
            for c in range(1, tk // LANES):
                best = jnp.maximum(best, s[:, c * LANES:(c + 1) * LANES])
            mx_sc[h] = jnp.maximum(mx_sc[h], best)

    def value_tile(j):
        start = pl.multiple_of(j * tk, tk)
        v = v_ref[0, pl.ds(start, tk), :]
        for h, (_, _, vg) in enumerate(heads):
            mb = mb_sc[h]
            s = s_sc[h, j]
            p = jnp.concatenate([jnp.exp(s[:, c * LANES:(c + 1) * LANES] - mb)
                                 for c in range(tk // LANES)], axis=-1).astype(BF16)
            acc_sc[h] += jnp.dot(p, v[:, vg * LANES:(vg + 1) * LANES], preferred_element_type=F32)

    def for_each_tile(fn_full, fn_masked):
        row = lax.broadcasted_iota(jnp.int32, (tq, tk), 0)
        col = lax.broadcasted_iota(jnp.int32, (tq, tk), 1)
        first_full = 0
        if window is not None:
            back = window // tk
            first_full = jnp.maximum(qi - back + 1, 0)

            @pl.when(qi >= back)
            def _():
                fn_masked(qi - back, col > row)

        def body(j, carry):
            fn_full(j)
            return carry

        lax.fori_loop(first_full, qi, body, 0)
        fn_masked(qi, col <= row)

    for_each_tile(lambda j: score_tile(j, None), score_tile)
    for h in range(nh):
        m = jnp.max(mx_sc[h], axis=-1, keepdims=True)
        mb_sc[h] = jnp.broadcast_to(m, (tq, LANES))
    for_each_tile(value_tile, lambda j, mask: value_tile(j))

    if gate_ref is not None:
        gates = jax.nn.sigmoid(gate_ref[0])
    results = []
    for h in range(nh):
        acc = acc_sc[h]
        o = acc / acc[:, HEAD_DIM:HEAD_DIM + 1]
        if gate_ref is not None:
            c = gate_cols[h]
            o = o * gates[:, c:c + 1]
        results.append(o)
    for pair in range(nh // 2):
        high = pltpu.roll(results[2 * pair + 1], HEAD_DIM, 1)
        o_ref[0, :, pair * LANES:(pair + 1) * LANES] = jnp.where(low, results[2 * pair], high).astype(o_ref.dtype)


def _flash(q, k, v, gate, *, n_steps, q_w, k_w, v_w, heads, window=None, gate_cols=None,
           out_dtype=F32):
    B, S, _ = q.shape
    tq = ATTN_TQ
    nh = len(heads)
    assert nh % 2 == 0 and S % tq == 0 and (window is None or window % tq == 0)
    n_stash = S // tq
    in_specs = [pl.BlockSpec((1, tq, q_w), lambda b, h, i: (b, i, h)),
                pl.BlockSpec((1, S, k_w), lambda b, h, i: (b, 0, h)),
                pl.BlockSpec((1, S, v_w), lambda b, h, i: (b, 0, h))]
    args = [q, k, v]
    if gate is not None:
        in_specs.append(pl.BlockSpec((1, tq, LANES), lambda b, h, i: (b, i, h)))
        args.append(gate)
    kern = functools.partial(_flash_kernel, heads=tuple(heads), window=window, gate_cols=gate_cols)
    return pl.pallas_call(
        kern,
        grid=(B, n_steps, S // tq),
        in_specs=in_specs,
        out_specs=pl.BlockSpec((1, tq, nh * HEAD_DIM), lambda b, h, i: (b, i, h)),
        out_shape=jax.ShapeDtypeStruct((B, S, n_steps * nh * HEAD_DIM), out_dtype),
        scratch_shapes=[pltpu.VMEM((nh, n_stash, tq, tq), F32), pltpu.VMEM((nh, tq, LANES), F32),
                        pltpu.VMEM((nh, tq, LANES), F32), pltpu.VMEM((nh, tq, LANES), F32)],
        compiler_params=_cparams("parallel", "parallel", "arbitrary"),
    )(*args)


def _outproj_kernel(*refs, group_sizes):
    it = iter(refs)
    y = None
    for n_in in group_sizes:
        acts = [next(it)[...].astype(F32) for _ in range(n_in)]
        w_ref = next(it)
        a = acts[0]
        for extra in acts[1:]:
            a = a + extra
        part = _dot(a, w_ref[...])
        y = part if y is None else y + part
    x_ref, g_ref, b_ref, o_ref = next(it), next(it), next(it), next(it)
    o_ref[...] = _layer_norm(DN_ALPHA * x_ref[...] + y, g_ref[...], b_ref[...])


def _outproj_ln(groups, x2, g, b):
    T = x2.shape[0]
    tm = ROW_TILE
    in_specs, args, sizes = [], [], []
    for acts, w in groups:
        for a in acts:
            in_specs.append(pl.BlockSpec((tm, a.shape[1]), lambda i: (i, 0)))
            args.append(a)
        in_specs.append(pl.BlockSpec(w.shape, lambda i: (0, 0)))
        args.append(w)
        sizes.append(len(acts))
    in_specs += [pl.BlockSpec((tm, D_MODEL), lambda i: (i, 0)),
                 pl.BlockSpec((1, D_MODEL), lambda i: (0, 0)),
                 pl.BlockSpec((1, D_MODEL), lambda i: (0, 0))]
    args += [x2, g.reshape(1, -1), b.reshape(1, -1)]
    return pl.pallas_call(
        functools.partial(_outproj_kernel, group_sizes=tuple(sizes)),
        grid=(T // tm,),
        in_specs=in_specs,
        out_specs=pl.BlockSpec((tm, D_MODEL), lambda i: (i, 0)),
        out_shape=jax.ShapeDtypeStruct((T, D_MODEL), F32),
        compiler_params=_cparams("parallel"),
    )(*args)


def _router_kernel(x_ref, w_ref, bias_ref, g_ref):
    logits = _dot_split_both(x_ref[...], w_ref[...])
    lane = _lane_iota(logits.shape)
    real = lane < N_EXPERTS
    scores = jax.nn.sigmoid(logits)
    choice = jnp.where(real, scores + bias_ref[...], NEG_INF)
    top2 = jnp.where(_group_rank(choice, GROUP_SIZE) < 2, choice, 0.0)
    grp_score = jnp.where(real, _group_sum(top2, GROUP_SIZE), NEG_INF)
    grp_taken = _take_top(grp_score, lane // GROUP_SIZE, TOPK_GROUPS)
    masked = jnp.where(grp_taken & real, choice, NEG_INF)
    taken = _take_top(masked, lane, TOP_K)
    w = jnp.where(taken, scores, 0.0)
    w = w / jnp.sum(w, axis=-1, keepdims=True) * ROUTED_SCALE
    g_ref[...] = jnp.where(lane == N_EXPERTS, 1.0, w)


def _router(x2, w_router_pad, bias_pad):
    T = x2.shape[0]
    tm = ROW_TILE
    return pl.pallas_call(
        _router_kernel,
        grid=(T // tm,),
        in_specs=[pl.BlockSpec((tm, D_MODEL), lambda i: (i, 0)),
                  pl.BlockSpec(w_router_pad.shape, lambda i: (0, 0)),
                  pl.BlockSpec((1, LANES), lambda i: (0, 0))],
        out_specs=pl.BlockSpec((tm, LANES), lambda i: (i, 0)),
        out_shape=jax.ShapeDtypeStruct((T, LANES), F32),
        compiler_params=_cparams("parallel"),
    )(x2, w_router_pad, bias_pad)


def _moe_kernel(x_ref, gates_ref, wg_ref, wu_ref, wd_ref, g_ref, b_ref, o_ref, xb_sc, acc_sc):
    step = pl.program_id(1)
    last = pl.num_programs(1) - 1
    per = wg_ref.shape[0]

    @pl.when(step == 0)
    def _():
        xb_sc[...] = x_ref[...].astype(BF16)
        acc_sc[...] = jnp.zeros(acc_sc.shape, F32)

    xb = xb_sc[...]
    gates = gates_ref[...]
    lane = _lane_iota(gates.shape)
    hidden = []
    for j in range(per):
        hg = jnp.dot(xb, wg_ref[j], preferred_element_type=F32)
        hu = jnp.dot(xb, wu_ref[j], preferred_element_type=F32)
        col = jnp.sum(jnp.where(lane == step * per + j, gates, 0.0), axis=-1, keepdims=True)
        hidden.append((jax.nn.silu(hg) * hu * col).astype(BF16))
    h = jnp.concatenate(hidden, axis=-1)
    wd = wd_ref[...].reshape(per * EXPERT_FF, D_MODEL)
    acc_sc[...] += jnp.dot(h, wd, preferred_element_type=F32)

    @pl.when(step == last)
    def _():
        o_ref[...] = _layer_norm(DN_ALPHA * x_ref[...] + acc_sc[...], g_ref[...], b_ref[...])


def _moe_ln(x2, gates, wg, wu, wd, g, b):
    T = x2.shape[0]
    tm = MOE_TILE
    per = MOE_EXPERTS_PER_STEP
    n_steps = wg.shape[0] // per
    assert n_steps * per == wg.shape[0]
    return pl.pallas_call(
        _moe_kernel,
        grid=(T // tm, n_steps),
        in_specs=[pl.BlockSpec((tm, D_MODEL), lambda i, e: (i, 0)),
                  pl.BlockSpec((tm, LANES), lambda i, e: (i, 0)),
                  pl.BlockSpec((per, D_MODEL, EXPERT_FF), lambda i, e: (e, 0, 0)),
                  pl.BlockSpec((per, D_MODEL, EXPERT_FF), lambda i, e: (e, 0, 0)),
                  pl.BlockSpec((per, EXPERT_FF, D_MODEL), lambda i, e: (e, 0, 0)),
                  pl.BlockSpec((1, D_MODEL), lambda i, e: (0, 0)),
                  pl.BlockSpec((1, D_MODEL), lambda i, e: (0, 0))],
        out_specs=pl.BlockSpec((tm, D_MODEL), lambda i, e: (i, 0)),
        out_shape=jax.ShapeDtypeStruct((T, D_MODEL), F32),
        scratch_shapes=[pltpu.VMEM((tm, D_MODEL), BF16), pltpu.VMEM((tm, D_MODEL), F32)],
        compiler_params=_cparams("parallel", "arbitrary"),
    )(x2, gates, wg, wu, wd, g.reshape(1, -1), b.reshape(1, -1))


def _moe_block(x2, router, router_bias, exp_gate, exp_up, exp_down, sh_gate, sh_up, sh_down, g, b):
    w_router_pad = jnp.pad(router, ((0, 0), (0, LANES - N_EXPERTS)))
    bias_pad = jnp.pad(router_bias, (0, LANES - N_EXPERTS)).reshape(1, LANES)
    gates = _router(x2, w_router_pad, bias_pad)
    wg = jnp.concatenate([exp_gate, sh_gate[None]], 0).astype(BF16)
    wu = jnp.concatenate([exp_up, sh_up[None]], 0).astype(BF16)
    wd = jnp.concatenate([exp_down, sh_down[None]], 0).astype(BF16)
    return _moe_ln(x2, gates, wg, wu, wd, g, b)


def _qkv1_kernel(x_ref, w_ref, tab_ref, q_ref, k_ref, v_ref, kmean_ref, *, seq_tiles):
    tm = x_ref.shape[0]
    xb = x_ref[...].astype(BF16)
    width = MOBA_HEADS * HEAD_DIM
    lane = _lane_iota((tm, LANES))
    low = lane < HEAD_DIM
    pair_first = (lane % HEAD_DIM) < (HEAD_DIM // 2)
    cos_p, sin_p = tab_ref[0], tab_ref[1]
    pos = (pl.program_id(0) % seq_tiles) * tm + lax.broadcasted_iota(jnp.int32, (tm, 1), 0)
    onehot = _block_onehot(pos, MOBA_BLOCK, (tm, LANES))
    one_lane = jnp.where(lane == HEAD_DIM, 1.0, 0.0)
    for j in range(width // LANES):
        c = j * LANES
        q = jnp.dot(xb, w_ref[:, c:c + LANES], preferred_element_type=F32)
        q = _rope_lanes(q, cos_p, sin_p, pair_first, 32)
        q_ref[:, c:c + LANES] = (q * (HEAD_DIM ** -0.5)).astype(BF16)
        k = jnp.dot(xb, w_ref[:, width + c:width + c + LANES], preferred_element_type=F32)
        k = _rope_lanes(k, cos_p, sin_p, pair_first, 32)
        k_ref[:, 2 * c:2 * c + LANES] = jnp.where(low, k, onehot).astype(BF16)
        k_ref[:, 2 * c + LANES:2 * c + 2 * LANES] = jnp.where(
            low, pltpu.roll(k, HEAD_DIM, 1), onehot).astype(BF16)
        kmean_ref[0, :, c:c + LANES] = jnp.mean(k, axis=0, keepdims=True)
        v = jnp.dot(xb, w_ref[:, 2 * width + c:2 * width + c + LANES], preferred_element_type=F32)
        v_ref[:, 2 * c:2 * c + LANES] = jnp.where(low, v, one_lane).astype(BF16)
        v_ref[:, 2 * c + LANES:2 * c + 2 * LANES] = jnp.where(
            low, pltpu.roll(v, HEAD_DIM, 1), one_lane).astype(BF16)


def _qkv1(x2, w_qkv, tabs, S):
    T = x2.shape[0]
    tm = MOBA_BLOCK
    ns = S // tm
    width = MOBA_HEADS * HEAD_DIM
    row = lambda w: pl.BlockSpec((tm, w), lambda i: (i, 0))
    return pl.pallas_call(
        functools.partial(_qkv1_kernel, seq_tiles=ns),
        grid=(T // tm,),
        in_specs=[pl.BlockSpec((tm, D_MODEL), lambda i: (i, 0)),
                  pl.BlockSpec(w_qkv.shape, lambda i: (0, 0)),
                  pl.BlockSpec((2, tm, LANES), lambda i: (0, i % ns, 0))],
        out_specs=[row(width), row(2 * width), row(2 * width),
                   pl.BlockSpec((1, 1, width), lambda i: (i, 0, 0))],
        out_shape=[jax.ShapeDtypeStruct((T, width), BF16), jax.ShapeDtypeStruct((T, 2 * width), BF16),
                   jax.ShapeDtypeStruct((T, 2 * width), BF16),
                   jax.ShapeDtypeStruct((T // tm, 1, width), F32)],
        compiler_params=_cparams("parallel"),
    )(x2, w_qkv, tabs)


def _moba_select_kernel(q_ref, km_ref, qa_ref, *, n_top, n_blocks):
    i = pl.program_id(1)
    tq = q_ref.shape[1]
    gate = _dot_split_lhs_rhs(q_ref[0], km_ref[0])
    lane = _lane_iota(gate.shape)
    blk = lane % 8
    own = (i * tq + lax.broadcasted_iota(jnp.int32, (tq, 1), 0)) // MOBA_BLOCK
    score = jnp.where((blk < own) & (blk < n_blocks), gate, NEG_INF)
    rank = _group_rank(score, 8)
    chosen = ((score > NEG_INF) & (rank < n_top)) | (blk == own)
    bias = jnp.where(chosen, 0.0, -1e30)
    low = lane < HEAD_DIM
    in_bias = (lane >= HEAD_DIM) & (lane < HEAD_DIM + 8)
    for pair in range(MOBA_HEADS // 2):
        qf = q_ref[0, :, pair * LANES:(pair + 1) * LANES].astype(F32)
        for half in range(2):
            h = 2 * pair + half
            x = qf if half == 0 else pltpu.roll(qf, HEAD_DIM, 1)
            b = jnp.where(in_bias, pltpu.roll(bias, (HEAD_DIM - 8 * h) % LANES, 1), 0.0)
            qa_ref[0, :, h * LANES:(h + 1) * LANES] = jnp.where(low, x, b).astype(BF16)


def _dot_split_lhs_rhs(q_bf16, b):
    hi, lo = _split(b)
    return (jnp.dot(q_bf16, hi, preferred_element_type=F32)
            + jnp.dot(q_bf16, lo, preferred_element_type=F32))


def _moba_select(q3, km, n_top, n_blocks):
    B, S, width = q3.shape
    tq = ATTN_TQ
    kern = functools.partial(_moba_select_kernel, n_top=n_top, n_blocks=n_blocks)
    return pl.pallas_call(
        kern,
        grid=(B, S // tq),
        in_specs=[pl.BlockSpec((1, tq, width), lambda b, i: (b, i, 0)),
                  pl.BlockSpec((1, width, LANES), lambda b, i: (b, 0, 0))],
        out_specs=pl.BlockSpec((1, tq, MOBA_HEADS * LANES), lambda b, i: (b, i, 0)),
        out_shape=jax.ShapeDtypeStruct((B, S, MOBA_HEADS * LANES), BF16),
        compiler_params=_cparams("parallel", "parallel"),
    )(q3, km)


def _layer0(x2, B, S, w_in, pe_k, pe_v, cmp_k1, cmp_k2, cmp_v1, cmp_v2,
            q_norm, w_uq, kv_norm, w_ukv, w_out, ln_g, ln_b):
    T = B * S
    G = NSA_KV_HEADS
    w_perm = _permute_columns(w_in, _proj0_columns()).astype(BF16)
    tabs = _rope_tables(S)
    qd = MLA_NOPE_DIM + MLA_ROPE_DIM
    wuq = jnp.pad(w_uq.reshape(MLA_Q_RANK, MLA_HEADS, qd), ((0, 0), (0, 0), (0, LANES - qd)))
    wuq = wuq.reshape(MLA_Q_RANK, MLA_HEADS * LANES).astype(BF16)
    wukv = w_ukv.reshape(MLA_KV_RANK, MLA_HEADS, MLA_NOPE_DIM + MLA_V_DIM)
    wk = jnp.pad(wukv[:, :, :MLA_NOPE_DIM], ((0, 0), (0, 0), (0, LANES - MLA_NOPE_DIM)))
    wk = wk.reshape(MLA_KV_RANK, MLA_HEADS * LANES).astype(BF16)
    wv = jnp.pad(wukv[:, :, MLA_NOPE_DIM:], ((0, 0), (0, 0), (0, LANES - MLA_V_DIM)))
    wv = wv.reshape(MLA_KV_RANK, MLA_HEADS * LANES).astype(BF16)

    q, ks, vs, kw, vw, kc, vc, gate, qm, km, vm = _proj0(
        x2, w_perm, tabs, q_norm.reshape(1, -1), kv_norm.reshape(1, -1), wuq, wk, wv, S)

    n_chunks = S // NSA_CMP_STRIDE
    chunk_w = NSA_CMP_STRIDE * HEAD_DIM

    def chunks(t):
        t = t.reshape(B, S, G, HEAD_DIM).transpose(0, 2, 1, 3)
        return t.reshape(B * G, n_chunks, chunk_w)

    pe = jnp.stack([pe_k[:NSA_CMP_STRIDE].reshape(-1), pe_k[NSA_CMP_STRIDE:].reshape(-1),
                    pe_v[:NSA_CMP_STRIDE].reshape(-1), pe_v[NSA_CMP_STRIDE:].reshape(-1)])
    pad2 = lambda w: jnp.pad(w, ((0, 0), (0, LANES - HEAD_DIM))).astype(BF16)
    kcmp, vcmp = _compress(chunks(kc), chunks(vc), pe, cmp_k1.astype(BF16), pad2(cmp_k2),
                           cmp_v1.astype(BF16), pad2(cmp_v2), _cmp_rope_tables(n_chunks))

    n_sel = S // NSA_SEL_BLOCK
    n_cmp = (S - NSA_CMP_LEN) // NSA_CMP_STRIDE + 1
    tok = np.arange(n_chunks)[:, None] * NSA_CMP_STRIDE + np.arange(NSA_CMP_LEN)[None, :]
    c2s = (tok[:, :, None] // NSA_SEL_BLOCK == np.arange(LANES)[None, None, :]).sum(1) / NSA_CMP_LEN
    c2s[n_cmp:] = 0.0
    c2s = jnp.asarray(c2s, dtype=BF16)

    q3 = q.reshape(B, S, -1)
    gate3 = gate.reshape(B, S, -1)
    o_cmp, q_aug = _cmp_attn(q3, kcmp, vcmp, gate3, c2s, n_sel)

    r3 = lambda t: t.reshape(B, S, -1)
    nsa_heads = [(r, 0, 0) for r in range(NSA_GROUP)]
    o_sel = _flash(q_aug, r3(ks), r3(vs), gate3, n_steps=G, q_w=NSA_GROUP * LANES, k_w=LANES,
                   v_w=LANES, heads=nsa_heads, gate_cols=[NSA_GROUP + r for r in range(NSA_GROUP)])
    o_win = _flash(q_aug, r3(kw), r3(vw), gate3, n_steps=G, q_w=NSA_GROUP * LANES, k_w=LANES,
                   v_w=LANES, heads=nsa_heads, window=NSA_WINDOW,
                   gate_cols=[2 * NSA_GROUP + r for r in range(NSA_GROUP)])
    o_mla = _flash(r3(qm), r3(km), r3(vm), None, n_steps=MLA_HEADS // FLASH_HEADS,
                   q_w=FLASH_HEADS * LANES, k_w=FLASH_HEADS * LANES, v_w=FLASH_HEADS * LANES,
                   heads=_OWN_KV_HEADS, out_dtype=BF16)

    n_nsa = NSA_HEADS * HEAD_DIM
    w_out_b = w_out.astype(BF16)
    groups = [([o_cmp.reshape(T, -1), o_sel.reshape(T, -1), o_win.reshape(T, -1)], w_out_b[:n_nsa]),
              ([o_mla.reshape(T, -1)], w_out_b[n_nsa:])]
    return _outproj_ln(groups, x2, ln_g, ln_b)


def _layer1(x2, B, S, w_qkv, w_out, ln_g, ln_b):
    T = B * S
    width = MOBA_HEADS * HEAD_DIM
    tabs = _rope_tables(S)[0:2]
    q, k, v, kmean = _qkv1(x2, w_qkv.astype(BF16), tabs, S)
    n_blocks = S // MOBA_BLOCK
    n_top = min(MOBA_TOPK, max(n_blocks - 1, 1))
    km = kmean.reshape(B, n_blocks, width).transpose(0, 2, 1)
    km = jnp.pad(km, ((0, 0), (0, 0), (0, 8 - n_blocks)))
    km = jnp.tile(km, (1, 1, MOBA_HEADS))
    diag = (np.arange(width)[:, None] // HEAD_DIM) == (np.arange(LANES)[None, :] // 8)
    km = jnp.where(jnp.asarray(diag)[None], km, 0.0)
    q3 = q.reshape(B, S, width)
    q_aug = _moba_select(q3, km, n_top, n_blocks)
    o = _flash(q_aug, k.reshape(B, S, 2 * width), v.reshape(B, S, 2 * width), None,
               n_steps=MOBA_HEADS // FLASH_HEADS, q_w=FLASH_HEADS * LANES, k_w=FLASH_HEADS * LANES,
               v_w=FLASH_HEADS * LANES, heads=_OWN_KV_HEADS, out_dtype=BF16)
    return _outproj_ln([([o.reshape(T, width)], w_out.astype(BF16))], x2, ln_g, ln_b)


def kernel(x, l0_w_in, l0_nsa_pe_k, l0_nsa_pe_v, l0_nsa_cmp_k1, l0_nsa_cmp_k2, l0_nsa_cmp_v1, l0_nsa_cmp_v2, l0_mla_q_norm, l0_mla_w_uq, l0_mla_kv_norm, l0_mla_w_ukv, l0_w_out, l0_ln1_g, l0_ln1_b, l0_router, l0_router_bias, l0_exp_gate, l0_exp_up, l0_exp_down, l0_sh_gate, l0_sh_up, l0_sh_down, l0_ln2_g, l0_ln2_b, l1_w_qkv, l1_w_out, l1_ln1_g, l1_ln1_b, l1_router, l1_router_bias, l1_exp_gate, l1_exp_up, l1_exp_down, l1_sh_gate, l1_sh_up, l1_sh_down, l1_ln2_g, l1_ln2_b):
    B, S, D = x.shape
    x2 = x.reshape(B * S, D)
    x2 = _layer0(x2, B, S, l0_w_in, l0_nsa_pe_k, l0_nsa_pe_v, l0_nsa_cmp_k1, l0_nsa_cmp_k2,
                 l0_nsa_cmp_v1, l0_nsa_cmp_v2, l0_mla_q_norm, l0_mla_w_uq, l0_mla_kv_norm,
                 l0_mla_w_ukv, l0_w_out, l0_ln1_g, l0_ln1_b)
    x2 = _moe_block(x2, l0_router, l0_router_bias, l0_exp_gate, l0_exp_up, l0_exp_down,
                    l0_sh_gate, l0_sh_up, l0_sh_down, l0_ln2_g, l0_ln2_b)
    x2 = _layer1(x2, B, S, l1_w_qkv, l1_w_out, l1_ln1_g, l1_ln1_b)
    x2 = _moe_block(x2, l1_router, l1_router_bias, l1_exp_gate, l1_exp_up, l1_exp_down,
                    l1_sh_gate, l1_sh_up, l1_sh_down, l1_ln2_g, l1_ln2_b)
    return x2.reshape(B, S, D)
```

```python
import functools

import numpy as np
import jax
import jax.numpy as jnp
from jax import lax
from jax.experimental import pallas as pl
from jax.experimental.pallas import tpu as pltpu

F32 = jnp.float32
BF16 = jnp.bfloat16

LANES = 128
VMEM_LIMIT = 48 * 1024 * 1024

D_MODEL = 1024
DEPTH = 2
HEAD_DIM = 64
ROPE_THETA = 10000.0
LN_EPS = 1e-5
RMS_EPS = 1e-6

NSA_HEADS = 8
NSA_KV_HEADS = 2
NSA_GROUP = NSA_HEADS // NSA_KV_HEADS
NSA_CMP_LEN = 32
NSA_CMP_STRIDE = 16
NSA_CMP_HIDDEN = 128
NSA_SEL_BLOCK = 64
NSA_SEL_TOPN = 8
NSA_WINDOW = 512

MLA_HEADS = 8
MLA_Q_RANK = 256
MLA_KV_RANK = 128
MLA_NOPE_DIM = 64
MLA_ROPE_DIM = 32
MLA_V_DIM = 64

MOBA_HEADS = 16
MOBA_BLOCK = 256
MOBA_TOPK = 3

N_EXPERTS = 64
N_GROUPS = 8
GROUP_SIZE = N_EXPERTS // N_GROUPS
TOPK_GROUPS = 4
TOP_K = 8
EXPERT_FF = 256
ROUTED_SCALE = 2.5

DN_ALPHA = (2 * DEPTH) ** 0.25

ROW_TILE = 512
ATTN_TQ = 256
ATTN_TK = 256
MOE_TILE = 512
MOE_EXPERTS_PER_STEP = 5
FLASH_HEADS = 4
_OWN_KV_HEADS = tuple((h, h, h) for h in range(FLASH_HEADS))

NEG_INF = float("-inf")


def _cparams(*sem):
    return pltpu.CompilerParams(dimension_semantics=sem, vmem_limit_bytes=VMEM_LIMIT)


def _dot(a, b):
    return jnp.dot(a.astype(BF16), b.astype(BF16), preferred_element_type=F32)


def _dot_nt(a, b):
    return lax.dot_general(a.astype(BF16), b.astype(BF16), (((1,), (1,)), ((), ())),
                           preferred_element_type=F32)


def _split(a):
    hi = a.astype(BF16)
    lo = (a - hi.astype(F32)).astype(BF16)
    return hi, lo


def _dot_split_lhs(a, b_bf16):
    hi, lo = _split(a)
    return (jnp.dot(hi, b_bf16, preferred_element_type=F32)
            + jnp.dot(lo, b_bf16, preferred_element_type=F32))


def _dot_split_both(a, b):
    ah, al = _split(a)
    bh, bl = _split(b)
    return (jnp.dot(ah, bh, preferred_element_type=F32)
            + jnp.dot(al, bh, preferred_element_type=F32)
            + jnp.dot(ah, bl, preferred_element_type=F32))


def _lane_iota(shape):
    return lax.broadcasted_iota(jnp.int32, shape, len(shape) - 1)


def _rope_lanes(x, cos, sin, first_half, half):
    n = x.shape[-1]
    rot = jnp.where(first_half, -pltpu.roll(x, n - half, 1), pltpu.roll(x, half, 1))
    return x * cos + rot * sin


def _layer_norm(z, g, b):
    mu = jnp.mean(z, axis=-1, keepdims=True)
    zc = z - mu
    var = jnp.mean(zc * zc, axis=-1, keepdims=True)
    return zc * lax.rsqrt(var + LN_EPS) * g + b


def _rms_norm(x, g):
    return x * lax.rsqrt(jnp.mean(x * x, axis=-1, keepdims=True) + RMS_EPS) * g


def _group_rank(x, group):
    n = x.shape[-1]
    pos = _lane_iota(x.shape) % group
    rank = jnp.zeros(x.shape, F32)
    for d in range(1, group):
        lower = pltpu.roll(x, d, 1)
        upper = pltpu.roll(x, n - d, 1)
        rank = rank + jnp.where((pos >= d) & (lower >= x), 1.0, 0.0)
        rank = rank + jnp.where((pos + d < group) & (upper > x), 1.0, 0.0)
    return rank


def _group_sum(x, group):
    n = x.shape[-1]
    pos = _lane_iota(x.shape) % group
    tot = x
    for d in range(1, group):
        tot = tot + jnp.where(pos >= d, pltpu.roll(x, d, 1), 0.0)
        tot = tot + jnp.where(pos + d < group, pltpu.roll(x, n - d, 1), 0.0)
    return tot


def _take_top(x, key, k):
    big = jnp.int32(1 << 30)
    taken = jnp.zeros(x.shape, jnp.bool_)
    for _ in range(k):
        m = jnp.max(x, axis=-1, keepdims=True)
        first = jnp.min(jnp.where(x == m, key, big), axis=-1, keepdims=True)
        hit = key == first
        taken = taken | hit
        x = jnp.where(hit, NEG_INF, x)
    return taken


def _rope_tables(S):
    pos = jnp.arange(S, dtype=F32)[:, None]

    def cs(half):
        inv = ROPE_THETA ** (-jnp.arange(half, dtype=F32) / half)
        ang = pos * inv[None, :]
        return jnp.cos(ang), jnp.sin(ang)

    c32, s32 = cs(HEAD_DIM // 2)
    c16, s16 = cs(MLA_ROPE_DIM // 2)
    one = jnp.ones((S, 1), F32)
    zero = jnp.zeros((S, 1), F32)
    cos_pair = jnp.concatenate([c32] * 4, -1)
    sin_pair = jnp.concatenate([s32] * 4, -1)
    cos_kv = jnp.concatenate([c32, c32, jnp.tile(one, (1, 64))], -1)
    sin_kv = jnp.concatenate([s32, s32, jnp.tile(zero, (1, 64))], -1)
    cos_mla = jnp.concatenate([jnp.tile(one, (1, 64)), c16, c16, jnp.tile(one, (1, 32))], -1)
    sin_mla = jnp.concatenate([jnp.tile(zero, (1, 64)), s16, s16, jnp.tile(zero, (1, 32))], -1)
    return jnp.stack([cos_pair, sin_pair, cos_kv, sin_kv, cos_mla, sin_mla])


def _cmp_rope_tables(n_chunks):
    pos = (jnp.arange(n_chunks, dtype=F32) * NSA_CMP_STRIDE + (NSA_CMP_LEN - 1))[:, None]
    half = HEAD_DIM // 2
    inv = ROPE_THETA ** (-jnp.arange(half, dtype=F32) / half)
    ang = pos * inv[None, :]
    c, s = jnp.cos(ang), jnp.sin(ang)
    one = jnp.ones((n_chunks, 64), F32)
    return jnp.stack([jnp.concatenate([c, c, one], -1), jnp.concatenate([s, s, 0 * one], -1)])


P0_Q = 0
P0_KVS = 512
P0_KVW = 768
P0_KC = 1024
P0_VC = 1152
P0_GATE = 1280
P0_CQ = 1536
P0_CKV = 1792
P0_KR = 1920
P0_W = 2048


def _proj0_columns():
    q_w = NSA_HEADS * HEAD_DIM
    kv0 = q_w
    piece = NSA_KV_HEADS * HEAD_DIM
    gate0 = kv0 + 6 * piece
    cq0 = gate0 + 3 * NSA_HEADS
    ckv0 = cq0 + MLA_Q_RANK
    kr0 = ckv0 + MLA_KV_RANK
    src = -np.ones((P0_W,), np.int64)
    src[P0_Q:P0_Q + q_w] = np.arange(q_w)
    d = np.arange(HEAD_DIM)
    for base, kp, vp in ((P0_KVS, 2, 3), (P0_KVW, 4, 5)):
        for g in range(NSA_KV_HEADS):
            src[base + g * 128 + d] = kv0 + kp * piece + g * HEAD_DIM + d
            src[base + g * 128 + 64 + d] = kv0 + vp * piece + g * HEAD_DIM + d
    src[P0_KC:P0_KC + piece] = kv0 + 0 * piece + np.arange(piece)
    src[P0_VC:P0_VC + piece] = kv0 + 1 * piece + np.arange(piece)
    for g in range(NSA_KV_HEADS):
        for br in range(3):
            for r in range(NSA_GROUP):
                src[P0_GATE + g * 128 + br * NSA_GROUP + r] = gate0 + br * NSA_HEADS + g * NSA_GROUP + r
    src[P0_CQ:P0_CQ + MLA_Q_RANK] = cq0 + np.arange(MLA_Q_RANK)
    src[P0_CKV:P0_CKV + MLA_KV_RANK] = ckv0 + np.arange(MLA_KV_RANK)
    src[P0_KR + 64:P0_KR + 64 + MLA_ROPE_DIM] = kr0 + np.arange(MLA_ROPE_DIM)
    return src


def _permute_columns(w, src):
    cols = jnp.take(w, jnp.asarray(np.maximum(src, 0)), axis=1)
    return jnp.where(jnp.asarray(src >= 0)[None, :], cols, 0.0)


def _block_onehot(pos, block, shape):
    return jnp.where(_lane_iota(shape) == HEAD_DIM + pos // block, 1.0, 0.0)


def _proj0_kernel(x_ref, w_ref, tab_ref, qg_ref, kvg_ref, wuq_ref, wk_ref, wv_ref,
                  q_ref, ks_ref, vs_ref, kw_ref, vw_ref, kc_ref, vc_ref, gate_ref,
                  qm_ref, km_ref, vm_ref, *, seq_tiles):
    tm = x_ref.shape[0]
    xb = x_ref[...].astype(BF16)
    lane = _lane_iota((tm, LANES))
    low = lane < HEAD_DIM
    pair_first = (lane % HEAD_DIM) < (HEAD_DIM // 2)
    mla_first = lane < (MLA_NOPE_DIM + MLA_ROPE_DIM // 2)
    cos_p, sin_p = tab_ref[0], tab_ref[1]
    cos_kv, sin_kv = tab_ref[2], tab_ref[3]
    cos_m, sin_m = tab_ref[4], tab_ref[5]
    pos = (pl.program_id(0) % seq_tiles) * tm + lax.broadcasted_iota(jnp.int32, (tm, 1), 0)
    sel_onehot = _block_onehot(pos, NSA_SEL_BLOCK, (tm, LANES))
    one_lane = jnp.where(lane == HEAD_DIM, 1.0, 0.0)

    def seg(c0, width):
        return jnp.dot(xb, w_ref[:, c0:c0 + width], preferred_element_type=F32)

    q = seg(P0_Q, 512)
    for j in range(4):
        blk = _rope_lanes(q[:, j * 128:(j + 1) * 128], cos_p, sin_p, pair_first, 32)
        q_ref[:, j * 128:(j + 1) * 128] = (blk * (HEAD_DIM ** -0.5)).astype(BF16)
    for c0, k_out, v_out, extra in ((P0_KVS, ks_ref, vs_ref, sel_onehot), (P0_KVW, kw_ref, vw_ref, 0.0)):
        kv = seg(c0, 256)
        for j in range(2):
            blk = _rope_lanes(kv[:, j * 128:(j + 1) * 128], cos_kv, sin_kv, pair_first, 32)
            k_out[:, j * 128:(j + 1) * 128] = jnp.where(low, blk, extra).astype(BF16)
            v_out[:, j * 128:(j + 1) * 128] = jnp.where(low, pltpu.roll(blk, HEAD_DIM, 1), one_lane).astype(BF16)
    kc_ref[...] = seg(P0_KC, 128)
    vc_ref[...] = seg(P0_VC, 128)
    gate_ref[...] = seg(P0_GATE, 256)

    cq = _rms_norm(seg(P0_CQ, 256), qg_ref[...])
    qm = _dot(cq, wuq_ref[...])
    mla_scale = (MLA_NOPE_DIM + MLA_ROPE_DIM) ** -0.5
    for h in range(MLA_HEADS):
        blk = _rope_lanes(qm[:, h * 128:(h + 1) * 128], cos_m, sin_m, mla_first, 16)
        qm_ref[:, h * 128:(h + 1) * 128] = (blk * mla_scale).astype(BF16)
    ckv = _rms_norm(seg(P0_CKV, 128), kvg_ref[...]).astype(BF16)
    kn = jnp.dot(ckv, wk_ref[...], preferred_element_type=F32)
    kpe = _rope_lanes(seg(P0_KR, 128), cos_m, sin_m, mla_first, 16)
    for h in range(MLA_HEADS):
        km_ref[:, h * 128:(h + 1) * 128] = (kn[:, h * 128:(h + 1) * 128] + kpe).astype(BF16)
    vm = jnp.dot(ckv, wv_ref[...], preferred_element_type=F32)
    for h in range(MLA_HEADS):
        vm_ref[:, h * 128:(h + 1) * 128] = jnp.where(low, vm[:, h * 128:(h + 1) * 128], one_lane).astype(BF16)


def _proj0(x2, w_perm, tabs, q_norm, kv_norm, wuq, wk, wv, S):
    T = x2.shape[0]
    tm = ROW_TILE
    ns = S // tm
    row = lambda w: pl.BlockSpec((tm, w), lambda i: (i, 0))
    full = lambda a: pl.BlockSpec(a.shape, lambda i: (0,) * a.ndim)
    widths = (512, 256, 256, 256, 256, 128, 128, 256, 1024, 1024, 1024)
    dtypes = (BF16, BF16, BF16, BF16, BF16, F32, F32, F32, BF16, BF16, BF16)
    return pl.pallas_call(
        functools.partial(_proj0_kernel, seq_tiles=ns),
        grid=(T // tm,),
        in_specs=[row(D_MODEL), full(w_perm),
                  pl.BlockSpec((6, tm, LANES), lambda i: (0, i % ns, 0)),
                  full(q_norm), full(kv_norm), full(wuq), full(wk), full(wv)],
        out_specs=[row(w) for w in widths],
        out_shape=[jax.ShapeDtypeStruct((T, w), d) for w, d in zip(widths, dtypes)],
        compiler_params=_cparams("parallel"),
    )(x2, w_perm, tabs, q_norm, kv_norm, wuq, wk, wv)


def _compress_kernel(kc_ref, vc_ref, pe_ref, w1k_ref, w2k_ref, w1v_ref, w2v_ref, tab_ref,
                     ko_ref, vo_ref):
    half = NSA_CMP_STRIDE * HEAD_DIM
    n = kc_ref.shape[1]

    def mlp(x, pe_lo, pe_hi, w1_ref, w2_ref):
        first = _dot(x + pe_lo, w1_ref[0:half, :])
        second = _dot(x + pe_hi, w1_ref[half:2 * half, :])
        hidden = first + pltpu.roll(second, n - 1, 0)
        return _dot(jax.nn.gelu(hidden), w2_ref[...])

    k = mlp(kc_ref[0], pe_ref[0:1, :], pe_ref[1:2, :], w1k_ref, w2k_ref)
    lane = _lane_iota(k.shape)
    ko_ref[0] = _rope_lanes(k, tab_ref[0], tab_ref[1], lane < HEAD_DIM // 2, 32).astype(BF16)
    vo_ref[0] = mlp(vc_ref[0], pe_ref[2:3, :], pe_ref[3:4, :], w1v_ref, w2v_ref).astype(BF16)


def _compress(kc_chunks, vc_chunks, pe, w1k, w2k, w1v, w2v, ctab):
    n_bg, n, width = kc_chunks.shape
    blk = pl.BlockSpec((1, n, width), lambda i: (i, 0, 0))
    full = lambda a: pl.BlockSpec(a.shape, lambda i: (0,) * a.ndim)
    out = pl.BlockSpec((1, n, LANES), lambda i: (i, 0, 0))
    return pl.pallas_call(
        _compress_kernel,
        grid=(n_bg,),
        in_specs=[blk, blk, full(pe), full(w1k), full(w2k), full(w1v), full(w2v), full(ctab)],
        out_specs=[out, out],
        out_shape=[jax.ShapeDtypeStruct((n_bg, n, LANES), BF16)] * 2,
        compiler_params=_cparams("parallel"),
    )(kc_chunks, vc_chunks, pe, w1k, w2k, w1v, w2v, ctab)


def _cmp_attn_kernel(q_ref, k_ref, v_ref, gate_ref, c2s_ref, o_ref, qa_ref, *, n_sel, n_top):
    qi = pl.program_id(2)
    tq = q_ref.shape[1]
    n = k_ref.shape[1]
    q = q_ref[0]
    k = k_ref[0][:, 0:HEAD_DIM]
    v = v_ref[0][:, 0:HEAD_DIM]
    gates = jax.nn.sigmoid(gate_ref[0])
    pos = qi * tq + lax.broadcasted_iota(jnp.int32, (tq, 1), 0)
    cmp_end = lax.broadcasted_iota(jnp.int32, (1, n), 1) * NSA_CMP_STRIDE + (NSA_CMP_LEN - 1)
    visible = cmp_end <= pos
    p_sum = jnp.zeros((tq, n), F32)
    outs = []
    for r in range(NSA_GROUP):
        s = jnp.where(visible, _dot_nt(q[:, r * HEAD_DIM:(r + 1) * HEAD_DIM], k), NEG_INF)
        m = jnp.max(s, axis=-1, keepdims=True)
        e = jnp.exp(s - jnp.where(m > NEG_INF, m, 0.0))
        den = jnp.sum(e, axis=-1, keepdims=True)
        p = e / jnp.where(den > 0, den, 1.0)
        p_sum = p_sum + p
        outs.append(_dot(p, v) * gates[:, r:r + 1])
    o_ref[0] = jnp.concatenate(outs, axis=-1)

    imp = _dot_split_lhs(p_sum, c2s_ref[...])
    blk = _lane_iota(imp.shape)
    cur = pos // NSA_SEL_BLOCK
    forced = (blk == 0) | (blk == cur) | (blk == cur - 1)
    valid = (blk <= cur) & (blk < n_sel)
    score = jnp.where(valid, jnp.where(forced, jnp.inf, imp), NEG_INF)
    taken = _take_top(score, blk, n_top)
    bias = jnp.where(blk < n_sel, jnp.where(taken & valid, 0.0, -1e30), 0.0)
    bias = pltpu.roll(bias, HEAD_DIM, 1)
    low = blk < HEAD_DIM
    qf = q.astype(F32)
    for r in range(NSA_GROUP):
        x = qf[:, (r // 2) * LANES:(r // 2 + 1) * LANES]
        if r % 2 == 1:
            x = pltpu.roll(x, HEAD_DIM, 1)
        qa_ref[0, :, r * LANES:(r + 1) * LANES] = jnp.where(low, x, bias).astype(BF16)


def _cmp_attn(q3, kcmp, vcmp, gate3, c2s, n_sel):
    B, S, _ = q3.shape
    n = kcmp.shape[1]
    tq = ATTN_TQ
    G = NSA_KV_HEADS
    kern = functools.partial(_cmp_attn_kernel, n_sel=n_sel, n_top=min(NSA_SEL_TOPN, n_sel))
    return pl.pallas_call(
        kern,
        grid=(B, G, S // tq),
        in_specs=[pl.BlockSpec((1, tq, 256), lambda b, g, i: (b, i, g)),
                  pl.BlockSpec((1, n, LANES), lambda b, g, i: (b * G + g, 0, 0)),
                  pl.BlockSpec((1, n, LANES), lambda b, g, i: (b * G + g, 0, 0)),
                  pl.BlockSpec((1, tq, LANES), lambda b, g, i: (b, i, g)),
                  pl.BlockSpec(c2s.shape, lambda b, g, i: (0, 0))],
        out_specs=[pl.BlockSpec((1, tq, 256), lambda b, g, i: (b, i, g)),
                   pl.BlockSpec((1, tq, 512), lambda b, g, i: (b, i, g))],
        out_shape=[jax.ShapeDtypeStruct((B, S, 512), F32),
                   jax.ShapeDtypeStruct((B, S, NSA_HEADS * LANES), BF16)],
        compiler_params=_cparams("parallel", "parallel", "parallel"),
    )(q3, kcmp, vcmp, gate3, c2s)


def _flash_kernel(*refs, heads, window, gate_cols):
    it = iter(refs)
    q_ref, k_ref, v_ref = next(it), next(it), next(it)
    gate_ref = next(it) if gate_cols is not None else None
    o_ref = next(it)
    s_sc, mx_sc, mb_sc, acc_sc = next(it), next(it), next(it), next(it)

    qi = pl.program_id(2)
    tq = q_ref.shape[1]
    tk = tq
    nh = len(heads)

    lane = _lane_iota((tq, LANES))
    low = lane < HEAD_DIM
    q_heads = [q_ref[0, :, g * LANES:(g + 1) * LANES] for (g, _, _) in heads]
    mx_sc[...] = jnp.full(mx_sc.shape, NEG_INF, F32)
    acc_sc[...] = jnp.zeros(acc_sc.shape, F32)

    def score_tile(j, pos_mask):
        start = pl.multiple_of(j * tk, tk)
        k = k_ref[0, pl.ds(start, tk), :]
        for h, (_, kg, _) in enumerate(heads):
            s = _dot_nt(q_heads[h], k[:, kg * LANES:(kg + 1) * LANES])
            if pos_mask is not None:
                s = jnp.where(pos_mask, s, NEG_INF)
            s_sc[h, j] = s
            best = s[:, 0:LANES]
            for c in range(1, tk // LANES):
                best = jnp.maximum(best, s[:, c * LANES:(c + 1) * LANES])
            mx_sc[h] = jnp.maximum(mx_sc[h], best)

    def value_tile(j):
        start = pl.multiple_of(j * tk, tk)
        v = v_ref[0, pl.ds(start, tk), :]
        for h, (_, _, vg) in enumerate(heads):
            mb = mb_sc[h]
            s = s_sc[h, j]
            p = jnp.concatenate([jnp.exp(s[:, c * LANES:(c + 1) * LANES] - mb)
                                 for c in range(tk // LANES)], axis=-1).astype(BF16)
            acc_sc[h] += jnp.dot(p, v[:, vg * LANES:(vg + 1) * LANES], preferred_element_type=F32)

    def for_each_tile(fn_full, fn_masked):
        row = lax.broadcasted_iota(jnp.int32, (tq, tk), 0)
        col = lax.broadcasted_iota(jnp.int32, (tq, tk), 1)
        first_full = 0
        if window is not None:
            back = window // tk
            first_full = jnp.maximum(qi - back + 1, 0)

            @pl.when(qi >= back)
            def _():
                fn_masked(qi - back, col > row)

        def body(j, carry):
            fn_full(j)
            return carry

        lax.fori_loop(first_full, qi, body, 0)
        fn_masked(qi, col <= row)

    for_each_tile(lambda j: score_tile(j, None), score_tile)
    for h in range(nh):
        m = jnp.max(mx_sc[h], axis=-1, keepdims=True)
        mb_sc[h] = jnp.broadcast_to(m, (tq, LANES))
    for_each_tile(value_tile, lambda j, mask: value_tile(j))

    if gate_ref is not None:
        gates = jax.nn.sigmoid(gate_ref[0])
    results = []
    for h in range(nh):
        acc = acc_sc[h]
        o = acc / acc[:, HEAD_DIM:HEAD_DIM + 1]
        if gate_ref is not None:
            c = gate_cols[h]
            o = o * gates[:, c:c + 1]
        results.append(o)
    for pair in range(nh // 2):
        high = pltpu.roll(results[2 * pair + 1], HEAD_DIM, 1)
        o_ref[0, :, pair * LANES:(pair + 1) * LANES] = jnp.where(low, results[2 * pair], high).astype(o_ref.dtype)


def _flash(q, k, v, gate, *, n_steps, q_w, k_w, v_w, heads, window=None, gate_cols=None,
           out_dtype=F32):
    B, S, _ = q.shape
    tq = ATTN_TQ
    nh = len(heads)
    assert nh % 2 == 0 and S % tq == 0 and (window is None or window % tq == 0)
    n_stash = S // tq
    in_specs = [pl.BlockSpec((1, tq, q_w), lambda b, h, i: (b, i, h)),
                pl.BlockSpec((1, S, k_w), lambda b, h, i: (b, 0, h)),
                pl.BlockSpec((1, S, v_w), lambda b, h, i: (b, 0, h))]
    args = [q, k, v]
    if gate is not None:
        in_specs.append(pl.BlockSpec((1, tq, LANES), lambda b, h, i: (b, i, h)))
        args.append(gate)
    kern = functools.partial(_flash_kernel, heads=tuple(heads), window=window, gate_cols=gate_cols)
    return pl.pallas_call(
        kern,
        grid=(B, n_steps, S // tq),
        in_specs=in_specs,
        out_specs=pl.BlockSpec((1, tq, nh * HEAD_DIM), lambda b, h, i: (b, i, h)),
        out_shape=jax.ShapeDtypeStruct((B, S, n_steps * nh * HEAD_DIM), out_dtype),
        scratch_shapes=[pltpu.VMEM((nh, n_stash, tq, tq), F32), pltpu.VMEM((nh, tq, LANES), F32),
                        pltpu.VMEM((nh, tq, LANES), F32), pltpu.VMEM((nh, tq, LANES), F32)],
        compiler_params=_cparams("parallel", "parallel", "arbitrary"),
    )(*args)


def _outproj_kernel(*refs, group_sizes):
    it = iter(refs)
    y = None
    for n_in in group_sizes:
        acts = [next(it)[...].astype(F32) for _ in range(n_in)]
        w_ref = next(it)
        a = acts[0]
        for extra in acts[1:]:
            a = a + extra
        part = _dot(a, w_ref[...])
        y = part if y is None else y + part
    x_ref, g_ref, b_ref, o_ref = next(it), next(it), next(it), next(it)
    o_ref[...] = _layer_norm(DN_ALPHA * x_ref[...] + y, g_ref[...], b_ref[...])


def _outproj_ln(groups, x2, g, b):
    T = x2.shape[0]
    tm = ROW_TILE
    in_specs, args, sizes = [], [], []
    for acts, w in groups:
        for a in acts:
            in_specs.append(pl.BlockSpec((tm, a.shape[1]), lambda i: (i, 0)))
            args.append(a)
        in_specs.append(pl.BlockSpec(w.shape, lambda i: (0, 0)))
        args.append(w)
        sizes.append(len(acts))
    in_specs += [pl.BlockSpec((tm, D_MODEL), lambda i: (i, 0)),
                 pl.BlockSpec((1, D_MODEL), lambda i: (0, 0)),
                 pl.BlockSpec((1, D_MODEL), lambda i: (0, 0))]
    args += [x2, g.reshape(1, -1), b.reshape(1, -1)]
    return pl.pallas_call(
        functools.partial(_outproj_kernel, group_sizes=tuple(sizes)),
        grid=(T // tm,),
        in_specs=in_specs,
        out_specs=pl.BlockSpec((tm, D_MODEL), lambda i: (i, 0)),
        out_shape=jax.ShapeDtypeStruct((T, D_MODEL), F32),
        compiler_params=_cparams("parallel"),
    )(*args)


def _router_kernel(x_ref, w_ref, bias_ref, g_ref):
    logits = _dot_split_both(x_ref[...], w_ref[...])
    lane = _lane_iota(logits.shape)
    real = lane < N_EXPERTS
    scores = jax.nn.sigmoid(logits)
    choice = jnp.where(real, scores + bias_ref[...], NEG_INF)
    top2 = jnp.where(_group_rank(choice, GROUP_SIZE) < 2, choice, 0.0)
    grp_score = jnp.where(real, _group_sum(top2, GROUP_SIZE), NEG_INF)
    grp_taken = _take_top(grp_score, lane // GROUP_SIZE, TOPK_GROUPS)
    masked = jnp.where(grp_taken & real, choice, NEG_INF)
    taken = _take_top(masked, lane, TOP_K)
    w = jnp.where(taken, scores, 0.0)
    w = w / jnp.sum(w, axis=-1, keepdims=True) * ROUTED_SCALE
    g_ref[...] = jnp.where(lane == N_EXPERTS, 1.0, w)


def _router(x2, w_router_pad, bias_pad):
    T = x2.shape[0]
    tm = ROW_TILE
    return pl.pallas_call(
        _router_kernel,
        grid=(T // tm,),
        in_specs=[pl.BlockSpec((tm, D_MODEL), lambda i: (i, 0)),
                  pl.BlockSpec(w_router_pad.shape, lambda i: (0, 0)),
                  pl.BlockSpec((1, LANES), lambda i: (0, 0))],
        out_specs=pl.BlockSpec((tm, LANES), lambda i: (i, 0)),
        out_shape=jax.ShapeDtypeStruct((T, LANES), F32),
        compiler_params=_cparams("parallel"),
    )(x2, w_router_pad, bias_pad)


def _moe_kernel(x_ref, gates_ref, wg_ref, wu_ref, wd_ref, g_ref, b_ref, o_ref, xb_sc, acc_sc):
    step = pl.program_id(1)
    last = pl.num_programs(1) - 1
    per = wg_ref.shape[0]

    @pl.when(step == 0)
    def _():
        xb_sc[...] = x_ref[...].astype(BF16)
        acc_sc[...] = jnp.zeros(acc_sc.shape, F32)

    xb = xb_sc[...]
    gates = gates_ref[...]
    lane = _lane_iota(gates.shape)
    hidden = []
    for j in range(per):
        hg = jnp.dot(xb, wg_ref[j], preferred_element_type=F32)
        hu = jnp.dot(xb, wu_ref[j], preferred_element_type=F32)
        col = jnp.sum(jnp.where(lane == step * per + j, gates, 0.0), axis=-1, keepdims=True)
        hidden.append((jax.nn.silu(hg) * hu * col).astype(BF16))
    h = jnp.concatenate(hidden, axis=-1)
    wd = wd_ref[...].reshape(per * EXPERT_FF, D_MODEL)
    acc_sc[...] += jnp.dot(h, wd, preferred_element_type=F32)

    @pl.when(step == last)
    def _():
        o_ref[...] = _layer_norm(DN_ALPHA * x_ref[...] + acc_sc[...], g_ref[...], b_ref[...])


def _moe_ln(x2, gates, wg, wu, wd, g, b):
    T = x2.shape[0]
    tm = MOE_TILE
    per = MOE_EXPERTS_PER_STEP
    n_steps = wg.shape[0] // per
    assert n_steps * per == wg.shape[0]
    return pl.pallas_call(
        _moe_kernel,
        grid=(T // tm, n_steps),
        in_specs=[pl.BlockSpec((tm, D_MODEL), lambda i, e: (i, 0)),
                  pl.BlockSpec((tm, LANES), lambda i, e: (i, 0)),
                  pl.BlockSpec((per, D_MODEL, EXPERT_FF), lambda i, e: (e, 0, 0)),
                  pl.BlockSpec((per, D_MODEL, EXPERT_FF), lambda i, e: (e, 0, 0)),
                  pl.BlockSpec((per, EXPERT_FF, D_MODEL), lambda i, e: (e, 0, 0)),
                  pl.BlockSpec((1, D_MODEL), lambda i, e: (0, 0)),
                  pl.BlockSpec((1, D_MODEL), lambda i, e: (0, 0))],
        out_specs=pl.BlockSpec((tm, D_MODEL), lambda i, e: (i, 0)),
        out_shape=jax.ShapeDtypeStruct((T, D_MODEL), F32),
        scratch_shapes=[pltpu.VMEM((tm, D_MODEL), BF16), pltpu.VMEM((tm, D_MODEL), F32)],
        compiler_params=_cparams("parallel", "arbitrary"),
    )(x2, gates, wg, wu, wd, g.reshape(1, -1), b.reshape(1, -1))


def _moe_block(x2, router, router_bias, exp_gate, exp_up, exp_down, sh_gate, sh_up, sh_down, g, b):
    w_router_pad = jnp.pad(router, ((0, 0), (0, LANES - N_EXPERTS)))
    bias_pad = jnp.pad(router_bias, (0, LANES - N_EXPERTS)).reshape(1, LANES)
    gates = _router(x2, w_router_pad, bias_pad)
    wg = jnp.concatenate([exp_gate, sh_gate[None]], 0).astype(BF16)
    wu = jnp.concatenate([exp_up, sh_up[None]], 0).astype(BF16)
    wd = jnp.concatenate([exp_down, sh_down[None]], 0).astype(BF16)
    return _moe_ln(x2, gates, wg, wu, wd, g, b)


def _qkv1_kernel(x_ref, w_ref, tab_ref, q_ref, k_ref, v_ref, kmean_ref, *, seq_tiles):
    tm = x_ref.shape[0]
    xb = x_ref[...].astype(BF16)
    width = MOBA_HEADS * HEAD_DIM
    lane = _lane_iota((tm, LANES))
    low = lane < HEAD_DIM
    pair_first = (lane % HEAD_DIM) < (HEAD_DIM // 2)
    cos_p, sin_p = tab_ref[0], tab_ref[1]
    pos = (pl.program_id(0) % seq_tiles) * tm + lax.broadcasted_iota(jnp.int32, (tm, 1), 0)
    onehot = _block_onehot(pos, MOBA_BLOCK, (tm, LANES))
    one_lane = jnp.where(lane == HEAD_DIM, 1.0, 0.0)
    wide = 2 * LANES
    for j in range(width // wide):
        q4 = jnp.dot(xb, w_ref[:, j * wide:(j + 1) * wide], preferred_element_type=F32)
        k4 = jnp.dot(xb, w_ref[:, width + j * wide:width + (j + 1) * wide], preferred_element_type=F32)
        v4 = jnp.dot(xb, w_ref[:, 2 * width + j * wide:2 * width + (j + 1) * wide],
                     preferred_element_type=F32)
        for half in range(2):
            c = j * wide + half * LANES
            part = slice(half * LANES, (half + 1) * LANES)
            q = _rope_lanes(q4[:, part], cos_p, sin_p, pair_first, 32)
            q_ref[:, c:c + LANES] = (q * (HEAD_DIM ** -0.5)).astype(BF16)
            k = _rope_lanes(k4[:, part], cos_p, sin_p, pair_first, 32)
            k_ref[:, 2 * c:2 * c + LANES] = jnp.where(low, k, onehot).astype(BF16)
            k_ref[:, 2 * c + LANES:2 * c + 2 * LANES] = jnp.where(
                low, pltpu.roll(k, HEAD_DIM, 1), onehot).astype(BF16)
            for blk in range(tm // MOBA_BLOCK):
                kmean_ref[blk, :, c:c + LANES] = jnp.mean(
                    k[blk * MOBA_BLOCK:(blk + 1) * MOBA_BLOCK], axis=0, keepdims=True)
            v = v4[:, part]
            v_ref[:, 2 * c:2 * c + LANES] = jnp.where(low, v, one_lane).astype(BF16)
            v_ref[:, 2 * c + LANES:2 * c + 2 * LANES] = jnp.where(
                low, pltpu.roll(v, HEAD_DIM, 1), one_lane).astype(BF16)


def _qkv1(x2, w_qkv, tabs, S):
    T = x2.shape[0]
    tm = ROW_TILE
    assert tm % MOBA_BLOCK == 0
    ns = S // tm
    width = MOBA_HEADS * HEAD_DIM
    row = lambda w: pl.BlockSpec((tm, w), lambda i: (i, 0))
    return pl.pallas_call(
        functools.partial(_qkv1_kernel, seq_tiles=ns),
        grid=(T // tm,),
        in_specs=[pl.BlockSpec((tm, D_MODEL), lambda i: (i, 0)),
                  pl.BlockSpec(w_qkv.shape, lambda i: (0, 0)),
                  pl.BlockSpec((2, tm, LANES), lambda i: (0, i % ns, 0))],
        out_specs=[row(width), row(2 * width), row(2 * width),
                   pl.BlockSpec((tm // MOBA_BLOCK, 1, width), lambda i: (i, 0, 0))],
        out_shape=[jax.ShapeDtypeStruct((T, width), BF16), jax.ShapeDtypeStruct((T, 2 * width), BF16),
                   jax.ShapeDtypeStruct((T, 2 * width), BF16),
                   jax.ShapeDtypeStruct((T // MOBA_BLOCK, 1, width), F32)],
        compiler_params=_cparams("parallel"),
    )(x2, w_qkv, tabs)


def _moba_select_kernel(q_ref, km_ref, qa_ref, *, n_top, n_blocks):
    i = pl.program_id(1)
    tq = q_ref.shape[1]
    gate = _dot_split_lhs_rhs(q_ref[0], km_ref[0])
    lane = _lane_iota(gate.shape)
    blk = lane % 8
    own = (i * tq + lax.broadcasted_iota(jnp.int32, (tq, 1), 0)) // MOBA_BLOCK
    score = jnp.where((blk < own) & (blk < n_blocks), gate, NEG_INF)
    rank = _group_rank(score, 8)
    chosen = ((score > NEG_INF) & (rank < n_top)) | (blk == own)
    bias = jnp.where(chosen, 0.0, -1e30)
    low = lane < HEAD_DIM
    in_bias = (lane >= HEAD_DIM) & (lane < HEAD_DIM + 8)
    for pair in range(MOBA_HEADS // 2):
        qf = q_ref[0, :, pair * LANES:(pair + 1) * LANES].astype(F32)
        for half in range(2):
            h = 2 * pair + half
            x = qf if half == 0 else pltpu.roll(qf, HEAD_DIM, 1)
            b = jnp.where(in_bias, pltpu.roll(bias, (HEAD_DIM - 8 * h) % LANES, 1), 0.0)
            qa_ref[0, :, h * LANES:(h + 1) * LANES] = jnp.where(low, x, b).astype(BF16)


def _dot_split_lhs_rhs(q_bf16, b):
    hi, lo = _split(b)
    return (jnp.dot(q_bf16, hi, preferred_element_type=F32)
            + jnp.dot(q_bf16, lo, preferred_element_type=F32))


def _moba_select(q3, km, n_top, n_blocks):
    B, S, width = q3.shape
    tq = ATTN_TQ
    kern = functools.partial(_moba_select_kernel, n_top=n_top, n_blocks=n_blocks)
    return pl.pallas_call(
        kern,
        grid=(B, S // tq),
        in_specs=[pl.BlockSpec((1, tq, width), lambda b, i: (b, i, 0)),
                  pl.BlockSpec((1, width, LANES), lambda b, i: (b, 0, 0))],
        out_specs=pl.BlockSpec((1, tq, MOBA_HEADS * LANES), lambda b, i: (b, i, 0)),
        out_shape=jax.ShapeDtypeStruct((B, S, MOBA_HEADS * LANES), BF16),
        compiler_params=_cparams("parallel", "parallel"),
    )(q3, km)


def _layer0(x2, B, S, w_in, pe_k, pe_v, cmp_k1, cmp_k2, cmp_v1, cmp_v2,
            q_norm, w_uq, kv_norm, w_ukv, w_out, ln_g, ln_b):
    T = B * S
    G = NSA_KV_HEADS
    w_perm = _permute_columns(w_in, _proj0_columns()).astype(BF16)
    tabs = _rope_tables(S)
    qd = MLA_NOPE_DIM + MLA_ROPE_DIM
    wuq = jnp.pad(w_uq.reshape(MLA_Q_RANK, MLA_HEADS, qd), ((0, 0), (0, 0), (0, LANES - qd)))
    wuq = wuq.reshape(MLA_Q_RANK, MLA_HEADS * LANES).astype(BF16)
    wukv = w_ukv.reshape(MLA_KV_RANK, MLA_HEADS, MLA_NOPE_DIM + MLA_V_DIM)
    wk = jnp.pad(wukv[:, :, :MLA_NOPE_DIM], ((0, 0), (0, 0), (0, LANES - MLA_NOPE_DIM)))
    wk = wk.reshape(MLA_KV_RANK, MLA_HEADS * LANES).astype(BF16)
    wv = jnp.pad(wukv[:, :, MLA_NOPE_DIM:], ((0, 0), (0, 0), (0, LANES - MLA_V_DIM)))
    wv = wv.reshape(MLA_KV_RANK, MLA_HEADS * LANES).astype(BF16)

    q, ks, vs, kw, vw, kc, vc, gate, qm, km, vm = _proj0(
        x2, w_perm, tabs, q_norm.reshape(1, -1), kv_norm.reshape(1, -1), wuq, wk, wv, S)

    n_chunks = S // NSA_CMP_STRIDE
    chunk_w = NSA_CMP_STRIDE * HEAD_DIM

    def chunks(t):
        t = t.reshape(B, S, G, HEAD_DIM).transpose(0, 2, 1, 3)
        return t.reshape(B * G, n_chunks, chunk_w)

    pe = jnp.stack([pe_k[:NSA_CMP_STRIDE].reshape(-1), pe_k[NSA_CMP_STRIDE:].reshape(-1),
                    pe_v[:NSA_CMP_STRIDE].reshape(-1), pe_v[NSA_CMP_STRIDE:].reshape(-1)])
    pad2 = lambda w: jnp.pad(w, ((0, 0), (0, LANES - HEAD_DIM))).astype(BF16)
    kcmp, vcmp = _compress(chunks(kc), chunks(vc), pe, cmp_k1.astype(BF16), pad2(cmp_k2),
                           cmp_v1.astype(BF16), pad2(cmp_v2), _cmp_rope_tables(n_chunks))

    n_sel = S // NSA_SEL_BLOCK
    n_cmp = (S - NSA_CMP_LEN) // NSA_CMP_STRIDE + 1
    tok = np.arange(n_chunks)[:, None] * NSA_CMP_STRIDE + np.arange(NSA_CMP_LEN)[None, :]
    c2s = (tok[:, :, None] // NSA_SEL_BLOCK == np.arange(LANES)[None, None, :]).sum(1) / NSA_CMP_LEN
    c2s[n_cmp:] = 0.0
    c2s = jnp.asarray(c2s, dtype=BF16)

    q3 = q.reshape(B, S, -1)
    gate3 = gate.reshape(B, S, -1)
    o_cmp, q_aug = _cmp_attn(q3, kcmp, vcmp, gate3, c2s, n_sel)

    r3 = lambda t: t.reshape(B, S, -1)
    nsa_heads = [(r, 0, 0) for r in range(NSA_GROUP)]
    o_sel = _flash(q_aug, r3(ks), r3(vs), gate3, n_steps=G, q_w=NSA_GROUP * LANES, k_w=LANES,
                   v_w=LANES, heads=nsa_heads, gate_cols=[NSA_GROUP + r for r in range(NSA_GROUP)])
    o_win = _flash(q_aug, r3(kw), r3(vw), gate3, n_steps=G, q_w=NSA_GROUP * LANES, k_w=LANES,
                   v_w=LANES, heads=nsa_heads, window=NSA_WINDOW,
                   gate_cols=[2 * NSA_GROUP + r for r in range(NSA_GROUP)])
    o_mla = _flash(r3(qm), r3(km), r3(vm), None, n_steps=MLA_HEADS // FLASH_HEADS,
                   q_w=FLASH_HEADS * LANES, k_w=FLASH_HEADS * LANES, v_w=FLASH_HEADS * LANES,
                   heads=_OWN_KV_HEADS, out_dtype=BF16)

    n_nsa = NSA_HEADS * HEAD_DIM
    w_out_b = w_out.astype(BF16)
    groups = [([o_cmp.reshape(T, -1), o_sel.reshape(T, -1), o_win.reshape(T, -1)], w_out_b[:n_nsa]),
              ([o_mla.reshape(T, -1)], w_out_b[n_nsa:])]
    return _outproj_ln(groups, x2, ln_g, ln_b)


def _layer1(x2, B, S, w_qkv, w_out, ln_g, ln_b):
    T = B * S
    width = MOBA_HEADS * HEAD_DIM
    tabs = _rope_tables(S)[0:2]
    q, k, v, kmean = _qkv1(x2, w_qkv.astype(BF16), tabs, S)
    n_blocks = S // MOBA_BLOCK
    n_top = min(MOBA_TOPK, max(n_blocks - 1, 1))
    km = kmean.reshape(B, n_blocks, width).transpose(0, 2, 1)
    km = jnp.pad(km, ((0, 0), (0, 0), (0, 8 - n_blocks)))
    km = jnp.tile(km, (1, 1, MOBA_HEADS))
    diag = (np.arange(width)[:, None] // HEAD_DIM) == (np.arange(LANES)[None, :] // 8)
    km = jnp.where(jnp.asarray(diag)[None], km, 0.0)
    q3 = q.reshape(B, S, width)
    q_aug = _moba_select(q3, km, n_top, n_blocks)
    o = _flash(q_aug, k.reshape(B, S, 2 * width), v.reshape(B, S, 2 * width), None,
               n_steps=MOBA_HEADS // FLASH_HEADS, q_w=FLASH_HEADS * LANES, k_w=FLASH_HEADS * LANES,
               v_w=FLASH_HEADS * LANES, heads=_OWN_KV_HEADS, out_dtype=BF16)
    return _outproj_ln([([o.reshape(T, width)], w_out.astype(BF16))], x2, ln_g, ln_b)


def kernel(x, l0_w_in, l0_nsa_pe_k, l0_nsa_pe_v, l0_nsa_cmp_k1, l0_nsa_cmp_k2, l0_nsa_cmp_v1, l0_nsa_cmp_v2, l0_mla_q_norm, l0_mla_w_uq, l0_mla_kv_norm, l0_mla_w_ukv, l0_w_out, l0_ln1_g, l0_ln1_b, l0_router, l0_router_bias, l0_exp_gate, l0_exp_up, l0_exp_down, l0_sh_gate, l0_sh_up, l0_sh_down, l0_ln2_g, l0_ln2_b, l1_w_qkv, l1_w_out, l1_ln1_g, l1_ln1_b, l1_router, l1_router_bias, l1_exp_gate, l1_exp_up, l1_exp_down, l1_sh_gate, l1_sh_up, l1_sh_down, l1_ln2_g, l1_ln2_b):
    B, S, D = x.shape
    x2 = x.reshape(B * S, D)
    x2 = _layer0(x2, B, S, l0_w_in, l0_nsa_pe_k, l0_nsa_pe_v, l0_nsa_cmp_k1, l0_nsa_cmp_k2,
                 l0_nsa_cmp_v1, l0_nsa_cmp_v2, l0_mla_q_norm, l0_mla_w_uq, l0_mla_kv_norm,
                 l0_mla_w_ukv, l0_w_out, l0_ln1_g, l0_ln1_b)
    x2 = _moe_block(x2, l0_router, l0_router_bias, l0_exp_gate, l0_exp_up, l0_exp_down,
                    l0_sh_gate, l0_sh_up, l0_sh_down, l0_ln2_g, l0_ln2_b)
    x2 = _layer1(x2, B, S, l1_w_qkv, l1_w_out, l1_ln1_g, l1_ln1_b)
    x2 = _moe_block(x2, l1_router, l1_router_bias, l1_exp_gate, l1_exp_up, l1_exp_down,
                    l1_sh_gate, l1_sh_up, l1_sh_down, l1_ln2_g, l1_ln2_b)
    return x2.reshape(B, S, D)
```

```python
import functools

import numpy as np
import jax
import jax.numpy as jnp
from jax import lax
from jax.experimental import pallas as pl
from jax.experimental.pallas import tpu as pltpu

F32 = jnp.float32
BF16 = jnp.bfloat16

LANES = 128
VMEM_LIMIT = 48 * 1024 * 1024

D_MODEL = 1024
DEPTH = 2
HEAD_DIM = 64
ROPE_THETA = 10000.0
LN_EPS = 1e-5
RMS_EPS = 1e-6

NSA_HEADS = 8
NSA_KV_HEADS = 2
NSA_GROUP = NSA_HEADS // NSA_KV_HEADS
NSA_CMP_LEN = 32
NSA_CMP_STRIDE = 16
NSA_CMP_HIDDEN = 128
NSA_SEL_BLOCK = 64
NSA_SEL_TOPN = 8
NSA_WINDOW = 512

MLA_HEADS = 8
MLA_Q_RANK = 256
MLA_KV_RANK = 128
MLA_NOPE_DIM = 64
MLA_ROPE_DIM = 32
MLA_V_DIM = 64

MOBA_HEADS = 16
MOBA_BLOCK = 256
MOBA_TOPK = 3

N_EXPERTS = 64
N_GROUPS = 8
GROUP_SIZE = N_EXPERTS // N_GROUPS
TOPK_GROUPS = 4
TOP_K = 8
EXPERT_FF = 256
ROUTED_SCALE = 2.5

DN_ALPHA = (2 * DEPTH) ** 0.25

ROW_TILE = 512
ATTN_TQ = 256
MOE_TILE = 512
MOE_EXPERTS_PER_STEP = 5
FLASH_HEADS = 4
_OWN_KV_HEADS = tuple((h, h, h) for h in range(FLASH_HEADS))

NEG_INF = float("-inf")


def _cparams(*sem):
    return pltpu.CompilerParams(dimension_semantics=sem, vmem_limit_bytes=VMEM_LIMIT)


def _dot(a, b):
    return jnp.dot(a.astype(BF16), b.astype(BF16), preferred_element_type=F32)


def _dot_nt(a, b):
    return lax.dot_general(a.astype(BF16), b.astype(BF16), (((1,), (1,)), ((), ())),
                           preferred_element_type=F32)


def _split(a):
    hi = a.astype(BF16)
    lo = (a - hi.astype(F32)).astype(BF16)
    return hi, lo


def _dot_split_lhs(a, b_bf16):
    hi, lo = _split(a)
    return (jnp.dot(hi, b_bf16, preferred_element_type=F32)
            + jnp.dot(lo, b_bf16, preferred_element_type=F32))


def _dot_split_both(a, b):
    ah, al = _split(a)
    bh, bl = _split(b)
    return (jnp.dot(ah, bh, preferred_element_type=F32)
            + jnp.dot(al, bh, preferred_element_type=F32)
            + jnp.dot(ah, bl, preferred_element_type=F32))


def _lane_iota(shape):
    return lax.broadcasted_iota(jnp.int32, shape, len(shape) - 1)


def _rope_lanes(x, cos, sin, first_half, half):
    n = x.shape[-1]
    rot = jnp.where(first_half, -pltpu.roll(x, n - half, 1), pltpu.roll(x, half, 1))
    return x * cos + rot * sin


def _layer_norm(z, g, b):
    mu = jnp.mean(z, axis=-1, keepdims=True)
    zc = z - mu
    var = jnp.mean(zc * zc, axis=-1, keepdims=True)
    return zc * lax.rsqrt(var + LN_EPS) * g + b


def _rms_norm(x, g):
    return x * lax.rsqrt(jnp.mean(x * x, axis=-1, keepdims=True) + RMS_EPS) * g


def _group_rank(x, group):
    n = x.shape[-1]
    pos = _lane_iota(x.shape) % group
    rank = jnp.zeros(x.shape, F32)
    for d in range(1, group):
        lower = pltpu.roll(x, d, 1)
        upper = pltpu.roll(x, n - d, 1)
        rank = rank + jnp.where((pos >= d) & (lower >= x), 1.0, 0.0)
        rank = rank + jnp.where((pos + d < group) & (upper > x), 1.0, 0.0)
    return rank


def _group_sum(x, group):
    n = x.shape[-1]
    pos = _lane_iota(x.shape) % group
    tot = x
    for d in range(1, group):
        tot = tot + jnp.where(pos >= d, pltpu.roll(x, d, 1), 0.0)
        tot = tot + jnp.where(pos + d < group, pltpu.roll(x, n - d, 1), 0.0)
    return tot


def _take_top(x, key, k):
    big = jnp.int32(1 << 30)
    taken = jnp.zeros(x.shape, jnp.bool_)
    for _ in range(k):
        m = jnp.max(x, axis=-1, keepdims=True)
        first = jnp.min(jnp.where(x == m, key, big), axis=-1, keepdims=True)
        hit = key == first
        taken = taken | hit
        x = jnp.where(hit, NEG_INF, x)
    return taken


def _rope_tables(S):
    pos = jnp.arange(S, dtype=F32)[:, None]

    def cs(half):
        inv = ROPE_THETA ** (-jnp.arange(half, dtype=F32) / half)
        ang = pos * inv[None, :]
        return jnp.cos(ang), jnp.sin(ang)

    c32, s32 = cs(HEAD_DIM // 2)
    c16, s16 = cs(MLA_ROPE_DIM // 2)
    one = jnp.ones((S, 1), F32)
    zero = jnp.zeros((S, 1), F32)
    cos_pair = jnp.concatenate([c32] * 4, -1)
    sin_pair = jnp.concatenate([s32] * 4, -1)
    cos_kv = jnp.concatenate([c32, c32, jnp.tile(one, (1, 64))], -1)
    sin_kv = jnp.concatenate([s32, s32, jnp.tile(zero, (1, 64))], -1)
    cos_mla = jnp.concatenate([jnp.tile(one, (1, 64)), c16, c16, jnp.tile(one, (1, 32))], -1)
    sin_mla = jnp.concatenate([jnp.tile(zero, (1, 64)), s16, s16, jnp.tile(zero, (1, 32))], -1)
    return jnp.stack([cos_pair, sin_pair, cos_kv, sin_kv, cos_mla, sin_mla])


def _cmp_rope_tables(n_chunks):
    pos = (jnp.arange(n_chunks, dtype=F32) * NSA_CMP_STRIDE + (NSA_CMP_LEN - 1))[:, None]
    half = HEAD_DIM // 2
    inv = ROPE_THETA ** (-jnp.arange(half, dtype=F32) / half)
    ang = pos * inv[None, :]
    c, s = jnp.cos(ang), jnp.sin(ang)
    one = jnp.ones((n_chunks, 64), F32)
    return jnp.stack([jnp.concatenate([c, c, one], -1), jnp.concatenate([s, s, 0 * one], -1)])


P0_Q = 0
P0_KVS = 512
P0_KVW = 768
P0_KC = 1024
P0_VC = 1152
P0_GATE = 1280
P0_CQ = 1536
P0_CKV = 1792
P0_KR = 1920
P0_W = 2048


def _proj0_columns():
    q_w = NSA_HEADS * HEAD_DIM
    kv0 = q_w
    piece = NSA_KV_HEADS * HEAD_DIM
    gate0 = kv0 + 6 * piece
    cq0 = gate0 + 3 * NSA_HEADS
    ckv0 = cq0 + MLA_Q_RANK
    kr0 = ckv0 + MLA_KV_RANK
    src = -np.ones((P0_W,), np.int64)
    src[P0_Q:P0_Q + q_w] = np.arange(q_w)
    d = np.arange(HEAD_DIM)
    for base, kp, vp in ((P0_KVS, 2, 3), (P0_KVW, 4, 5)):
        for g in range(NSA_KV_HEADS):
            src[base + g * 128 + d] = kv0 + kp * piece + g * HEAD_DIM + d
            src[base + g * 128 + 64 + d] = kv0 + vp * piece + g * HEAD_DIM + d
    src[P0_KC:P0_KC + piece] = kv0 + 0 * piece + np.arange(piece)
    src[P0_VC:P0_VC + piece] = kv0 + 1 * piece + np.arange(piece)
    for g in range(NSA_KV_HEADS):
        for br in range(3):
            for r in range(NSA_GROUP):
                src[P0_GATE + g * 128 + br * NSA_GROUP + r] = gate0 + br * NSA_HEADS + g * NSA_GROUP + r
    src[P0_CQ:P0_CQ + MLA_Q_RANK] = cq0 + np.arange(MLA_Q_RANK)
    src[P0_CKV:P0_CKV + MLA_KV_RANK] = ckv0 + np.arange(MLA_KV_RANK)
    src[P0_KR + 64:P0_KR + 64 + MLA_ROPE_DIM] = kr0 + np.arange(MLA_ROPE_DIM)
    return src


def _permute_columns(w, src):
    cols = jnp.take(w, jnp.asarray(np.maximum(src, 0)), axis=1)
    return jnp.where(jnp.asarray(src >= 0)[None, :], cols, 0.0)


def _block_onehot(pos, block, shape):
    return jnp.where(_lane_iota(shape) == HEAD_DIM + pos // block, 1.0, 0.0)


def _proj0_kernel(x_ref, w_ref, tab_ref, qg_ref, kvg_ref, wuq_ref, wk_ref, wv_ref,
                  q_ref, ks_ref, vs_ref, kw_ref, vw_ref, kc_ref, vc_ref, gate_ref,
                  qm_ref, km_ref, vm_ref, *, seq_tiles):
    tm = x_ref.shape[0]
    xb = x_ref[...].astype(BF16)
    lane = _lane_iota((tm, LANES))
    low = lane < HEAD_DIM
    pair_first = (lane % HEAD_DIM) < (HEAD_DIM // 2)
    mla_first = lane < (MLA_NOPE_DIM + MLA_ROPE_DIM // 2)
    cos_p, sin_p = tab_ref[0], tab_ref[1]
    cos_kv, sin_kv = tab_ref[2], tab_ref[3]
    cos_m, sin_m = tab_ref[4], tab_ref[5]
    pos = (pl.program_id(0) % seq_tiles) * tm + lax.broadcasted_iota(jnp.int32, (tm, 1), 0)
    sel_onehot = _block_onehot(pos, NSA_SEL_BLOCK, (tm, LANES))
    one_lane = jnp.where(lane == HEAD_DIM, 1.0, 0.0)

    def seg(c0, width):
        return jnp.dot(xb, w_ref[:, c0:c0 + width], preferred_element_type=F32)

    q = seg(P0_Q, 512)
    for j in range(4):
        blk = _rope_lanes(q[:, j * 128:(j + 1) * 128], cos_p, sin_p, pair_first, 32)
        q_ref[:, j * 128:(j + 1) * 128] = (blk * (HEAD_DIM ** -0.5)).astype(BF16)
    for c0, k_out, v_out, extra in ((P0_KVS, ks_ref, vs_ref, sel_onehot), (P0_KVW, kw_ref, vw_ref, 0.0)):
        kv = seg(c0, 256)
        for j in range(2):
            blk = _rope_lanes(kv[:, j * 128:(j + 1) * 128], cos_kv, sin_kv, pair_first, 32)
            k_out[:, j * 128:(j + 1) * 128] = jnp.where(low, blk, extra).astype(BF16)
            v_out[:, j * 128:(j + 1) * 128] = jnp.where(low, pltpu.roll(blk, HEAD_DIM, 1), one_lane).astype(BF16)
    kc_ref[...] = seg(P0_KC, 128)
    vc_ref[...] = seg(P0_VC, 128)
    gate_ref[...] = seg(P0_GATE, 256)

    cq = _rms_norm(seg(P0_CQ, 256), qg_ref[...])
    qm = _dot(cq, wuq_ref[...])
    mla_scale = (MLA_NOPE_DIM + MLA_ROPE_DIM) ** -0.5
    for h in range(MLA_HEADS):
        blk = _rope_lanes(qm[:, h * 128:(h + 1) * 128], cos_m, sin_m, mla_first, 16)
        qm_ref[:, h * 128:(h + 1) * 128] = (blk * mla_scale).astype(BF16)
    ckv = _rms_norm(seg(P0_CKV, 128), kvg_ref[...]).astype(BF16)
    kn = jnp.dot(ckv, wk_ref[...], preferred_element_type=F32)
    kpe = _rope_lanes(seg(P0_KR, 128), cos_m, sin_m, mla_first, 16)
    for h in range(MLA_HEADS):
        km_ref[:, h * 128:(h + 1) * 128] = (kn[:, h * 128:(h + 1) * 128] + kpe).astype(BF16)
    vm = jnp.dot(ckv, wv_ref[...], preferred_element_type=F32)
    for h in range(MLA_HEADS):
        vm_ref[:, h * 128:(h + 1) * 128] = jnp.where(low, vm[:, h * 128:(h + 1) * 128], one_lane).astype(BF16)


def _proj0(x2, w_perm, tabs, q_norm, kv_norm, wuq, wk, wv, S):
    T = x2.shape[0]
    tm = ROW_TILE
    ns = S // tm
    row = lambda w: pl.BlockSpec((tm, w), lambda i: (i, 0))
    full = lambda a: pl.BlockSpec(a.shape, lambda i: (0,) * a.ndim)
    widths = (512, 256, 256, 256, 256, 128, 128, 256, 1024, 1024, 1024)
    dtypes = (BF16, BF16, BF16, BF16, BF16, F32, F32, F32, BF16, BF16, BF16)
    return pl.pallas_call(
        functools.partial(_proj0_kernel, seq_tiles=ns),
        grid=(T // tm,),
        in_specs=[row(D_MODEL), full(w_perm),
                  pl.BlockSpec((6, tm, LANES), lambda i: (0, i % ns, 0)),
                  full(q_norm), full(kv_norm), full(wuq), full(wk), full(wv)],
        out_specs=[row(w) for w in widths],
        out_shape=[jax.ShapeDtypeStruct((T, w), d) for w, d in zip(widths, dtypes)],
        compiler_params=_cparams("parallel"),
    )(x2, w_perm, tabs, q_norm, kv_norm, wuq, wk, wv)


def _compress_kernel(kc_ref, vc_ref, pe_ref, w1k_ref, w2k_ref, w1v_ref, w2v_ref, tab_ref,
                     ko_ref, vo_ref):
    half = NSA_CMP_STRIDE * HEAD_DIM
    n = kc_ref.shape[1]

    def mlp(x, pe_lo, pe_hi, w1_ref, w2_ref):
        first = _dot(x + pe_lo, w1_ref[0:half, :])
        second = _dot(x + pe_hi, w1_ref[half:2 * half, :])
        hidden = first + pltpu.roll(second, n - 1, 0)
        return _dot(jax.nn.gelu(hidden), w2_ref[...])

    k = mlp(kc_ref[0], pe_ref[0:1, :], pe_ref[1:2, :], w1k_ref, w2k_ref)
    lane = _lane_iota(k.shape)
    ko_ref[0] = _rope_lanes(k, tab_ref[0], tab_ref[1], lane < HEAD_DIM // 2, 32).astype(BF16)
    vo_ref[0] = mlp(vc_ref[0], pe_ref[2:3, :], pe_ref[3:4, :], w1v_ref, w2v_ref).astype(BF16)


def _compress(kc_chunks, vc_chunks, pe, w1k, w2k, w1v, w2v, ctab):
    n_bg, n, width = kc_chunks.shape
    blk = pl.BlockSpec((1, n, width), lambda i: (i, 0, 0))
    full = lambda a: pl.BlockSpec(a.shape, lambda i: (0,) * a.ndim)
    out = pl.BlockSpec((1, n, LANES), lambda i: (i, 0, 0))
    return pl.pallas_call(
        _compress_kernel,
        grid=(n_bg,),
        in_specs=[blk, blk, full(pe), full(w1k), full(w2k), full(w1v), full(w2v), full(ctab)],
        out_specs=[out, out],
        out_shape=[jax.ShapeDtypeStruct((n_bg, n, LANES), BF16)] * 2,
        compiler_params=_cparams("parallel"),
    )(kc_chunks, vc_chunks, pe, w1k, w2k, w1v, w2v, ctab)


def _cmp_attn_kernel(q_ref, k_ref, v_ref, gate_ref, c2s_ref, o_ref, qa_ref, *, n_sel, n_top):
    qi = pl.program_id(2)
    tq = q_ref.shape[1]
    n = k_ref.shape[1]
    q = q_ref[0]
    k = k_ref[0][:, 0:HEAD_DIM]
    v = v_ref[0][:, 0:HEAD_DIM]
    gates = jax.nn.sigmoid(gate_ref[0])
    pos = qi * tq + lax.broadcasted_iota(jnp.int32, (tq, 1), 0)
    cmp_end = lax.broadcasted_iota(jnp.int32, (1, n), 1) * NSA_CMP_STRIDE + (NSA_CMP_LEN - 1)
    visible = cmp_end <= pos
    p_sum = jnp.zeros((tq, n), F32)
    outs = []
    for r in range(NSA_GROUP):
        s = jnp.where(visible, _dot_nt(q[:, r * HEAD_DIM:(r + 1) * HEAD_DIM], k), NEG_INF)
        m = jnp.max(s, axis=-1, keepdims=True)
        e = jnp.exp(s - jnp.where(m > NEG_INF, m, 0.0))
        den = jnp.sum(e, axis=-1, keepdims=True)
        p = e / jnp.where(den > 0, den, 1.0)
        p_sum = p_sum + p
        outs.append(_dot(p, v) * gates[:, r:r + 1])
    o_ref[0] = jnp.concatenate(outs, axis=-1)

    imp = _dot_split_lhs(p_sum, c2s_ref[...])
    blk = _lane_iota(imp.shape)
    cur = pos // NSA_SEL_BLOCK
    forced = (blk == 0) | (blk == cur) | (blk == cur - 1)
    valid = (blk <= cur) & (blk < n_sel)
    free = jnp.where(valid & ~forced, imp, NEG_INF)
    taken = forced | _take_top(free, blk, n_top - 3)
    bias = jnp.where(blk < n_sel, jnp.where(taken & valid, 0.0, -1e30), 0.0)
    bias = pltpu.roll(bias, HEAD_DIM, 1)
    low = blk < HEAD_DIM
    qf = q.astype(F32)
    for r in range(NSA_GROUP):
        x = qf[:, (r // 2) * LANES:(r // 2 + 1) * LANES]
        if r % 2 == 1:
            x = pltpu.roll(x, HEAD_DIM, 1)
        qa_ref[0, :, r * LANES:(r + 1) * LANES] = jnp.where(low, x, bias).astype(BF16)


def _cmp_attn(q3, kcmp, vcmp, gate3, c2s, n_sel):
    B, S, _ = q3.shape
    n = kcmp.shape[1]
    tq = ROW_TILE
    G = NSA_KV_HEADS
    assert n_sel >= 3 and NSA_SEL_TOPN >= 3
    kern = functools.partial(_cmp_attn_kernel, n_sel=n_sel, n_top=min(NSA_SEL_TOPN, n_sel))
    return pl.pallas_call(
        kern,
        grid=(B, G, S // tq),
        in_specs=[pl.BlockSpec((1, tq, 256), lambda b, g, i: (b, i, g)),
                  pl.BlockSpec((1, n, LANES), lambda b, g, i: (b * G + g, 0, 0)),
                  pl.BlockSpec((1, n, LANES), lambda b, g, i: (b * G + g, 0, 0)),
                  pl.BlockSpec((1, tq, LANES), lambda b, g, i: (b, i, g)),
                  pl.BlockSpec(c2s.shape, lambda b, g, i: (0, 0))],
        out_specs=[pl.BlockSpec((1, tq, 256), lambda b, g, i: (b, i, g)),
                   pl.BlockSpec((1, tq, 512), lambda b, g, i: (b, i, g))],
        out_shape=[jax.ShapeDtypeStruct((B, S, 512), F32),
                   jax.ShapeDtypeStruct((B, S, NSA_HEADS * LANES), BF16)],
        compiler_params=_cparams("parallel", "parallel", "parallel"),
    )(q3, kcmp, vcmp, gate3, c2s)


def _flash_kernel(*refs, heads, window, gate_cols):
    it = iter(refs)
    q_ref, k_ref, v_ref = next(it), next(it), next(it)
    gate_ref = next(it) if gate_cols is not None else None
    o_ref = next(it)
    s_sc, mx_sc, mb_sc, acc_sc = next(it), next(it), next(it), next(it)

    qi = pl.program_id(2)
    tq = q_ref.shape[1]
    tk = tq
    nh = len(heads)

    lane = _lane_iota((tq, LANES))
    low = lane < HEAD_DIM
    q_heads = [q_ref[0, :, g * LANES:(g + 1) * LANES] for (g, _, _) in heads]
    mx_sc[...] = jnp.full(mx_sc.shape, NEG_INF, F32)
    acc_sc[...] = jnp.zeros(acc_sc.shape, F32)

    def score_tile(j, pos_mask):
        start = pl.multiple_of(j * tk, tk)
        k = k_ref[0, pl.ds(start, tk), :]
        for h, (_, kg, _) in enumerate(heads):
            s = _dot_nt(q_heads[h], k[:, kg * LANES:(kg + 1) * LANES])
            if pos_mask is not None:
                s = jnp.where(pos_mask, s, NEG_INF)
            s_sc[h, j] = s
            best = s[:, 0:LANES]
            for c in range(1, tk // LANES):
                best = jnp.maximum(best, s[:, c * LANES:(c + 1) * LANES])
            mx_sc[h] = jnp.maximum(mx_sc[h], best)

    def value_tile(j):
        start = pl.multiple_of(j * tk, tk)
        v = v_ref[0, pl.ds(start, tk), :]
        for h, (_, _, vg) in enumerate(heads):
            mb = mb_sc[h]
            s = s_sc[h, j]
            p = jnp.concatenate([jnp.exp(s[:, c * LANES:(c + 1) * LANES] - mb)
                                 for c in range(tk // LANES)], axis=-1).astype(BF16)
            acc_sc[h] += jnp.dot(p, v[:, vg * LANES:(vg + 1) * LANES], preferred_element_type=F32)

    def for_each_tile(fn_full, fn_masked):
        row = lax.broadcasted_iota(jnp.int32, (tq, tk), 0)
        col = lax.broadcasted_iota(jnp.int32, (tq, tk), 1)
        first_full = 0
        if window is not None:
            back = window // tk
            first_full = jnp.maximum(qi - back + 1, 0)

            @pl.when(qi >= back)
            def _():
                fn_masked(qi - back, col > row)

        def body(j, carry):
            fn_full(j)
            return carry

        lax.fori_loop(first_full, qi, body, 0)
        fn_masked(qi, col <= row)

    for_each_tile(lambda j: score_tile(j, None), score_tile)
    for h in range(nh):
        m = jnp.max(mx_sc[h], axis=-1, keepdims=True)
        mb_sc[h] = jnp.broadcast_to(m, (tq, LANES))
    for_each_tile(value_tile, lambda j, mask: value_tile(j))

    if gate_ref is not None:
        gates = jax.nn.sigmoid(gate_ref[0])
    results = []
    for h in range(nh):
        acc = acc_sc[h]
        o = acc / acc[:, HEAD_DIM:HEAD_DIM + 1]
        if gate_ref is not None:
            c = gate_cols[h]
            o = o * gates[:, c:c + 1]
        results.append(o)
    for pair in range(nh // 2):
        high = pltpu.roll(results[2 * pair + 1], HEAD_DIM, 1)
        o_ref[0, :, pair * LANES:(pair + 1) * LANES] = jnp.where(low, results[2 * pair], high).astype(o_ref.dtype)


def _flash(q, k, v, gate, *, n_steps, q_w, k_w, v_w, heads, window=None, gate_cols=None,
           out_dtype=F32):
    B, S, _ = q.shape
    tq = ATTN_TQ
    nh = len(heads)
    assert nh % 2 == 0 and S % tq == 0 and (window is None or window % tq == 0)
    n_stash = S // tq
    in_specs = [pl.BlockSpec((1, tq, q_w), lambda b, h, i: (b, i, h)),
                pl.BlockSpec((1, S, k_w), lambda b, h, i: (b, 0, h)),
                pl.BlockSpec((1, S, v_w), lambda b, h, i: (b, 0, h))]
    args = [q, k, v]
    if gate is not None:
        in_specs.append(pl.BlockSpec((1, tq, LANES), lambda b, h, i: (b, i, h)))
        args.append(gate)
    kern = functools.partial(_flash_kernel, heads=tuple(heads), window=window, gate_cols=gate_cols)
    return pl.pallas_call(
        kern,
        grid=(B, n_steps, S // tq),
        in_specs=in_specs,
        out_specs=pl.BlockSpec((1, tq, nh * HEAD_DIM), lambda b, h, i: (b, i, h)),
        out_shape=jax.ShapeDtypeStruct((B, S, n_steps * nh * HEAD_DIM), out_dtype),
        scratch_shapes=[pltpu.VMEM((nh, n_stash, tq, tq), F32), pltpu.VMEM((nh, tq, LANES), F32),
                        pltpu.VMEM((nh, tq, LANES), F32), pltpu.VMEM((nh, tq, LANES), F32)],
        compiler_params=_cparams("parallel", "parallel", "arbitrary"),
    )(*args)


def _outproj_kernel(*refs, group_sizes):
    it = iter(refs)
    y = None
    for n_in in group_sizes:
        acts = [next(it)[...].astype(F32) for _ in range(n_in)]
        w_ref = next(it)
        a = acts[0]
        for extra in acts[1:]:
            a = a + extra
        part = _dot(a, w_ref[...])
        y = part if y is None else y + part
    x_ref, g_ref, b_ref, o_ref = next(it), next(it), next(it), next(it)
    o_ref[...] = _layer_norm(DN_ALPHA * x_ref[...] + y, g_ref[...], b_ref[...])


def _outproj_ln(groups, x2, g, b):
    T = x2.shape[0]
    tm = ROW_TILE
    in_specs, args, sizes = [], [], []
    for acts, w in groups:
        for a in acts:
            in_specs.append(pl.BlockSpec((tm, a.shape[1]), lambda i: (i, 0)))
            args.append(a)
        in_specs.append(pl.BlockSpec(w.shape, lambda i: (0, 0)))
        args.append(w)
        sizes.append(len(acts))
    in_specs += [pl.BlockSpec((tm, D_MODEL), lambda i: (i, 0)),
                 pl.BlockSpec((1, D_MODEL), lambda i: (0, 0)),
                 pl.BlockSpec((1, D_MODEL), lambda i: (0, 0))]
    args += [x2, g.reshape(1, -1), b.reshape(1, -1)]
    return pl.pallas_call(
        functools.partial(_outproj_kernel, group_sizes=tuple(sizes)),
        grid=(T // tm,),
        in_specs=in_specs,
        out_specs=pl.BlockSpec((tm, D_MODEL), lambda i: (i, 0)),
        out_shape=jax.ShapeDtypeStruct((T, D_MODEL), F32),
        compiler_params=_cparams("parallel"),
    )(*args)


def _router_kernel(x_ref, w_ref, bias_ref, g_ref):
    logits = _dot_split_both(x_ref[...], w_ref[...])
    lane = _lane_iota(logits.shape)
    real = lane < N_EXPERTS
    scores = jax.nn.sigmoid(logits)
    choice = jnp.where(real, scores + bias_ref[...], NEG_INF)
    top2 = jnp.where(_group_rank(choice, GROUP_SIZE) < 2, choice, 0.0)
    grp_score = jnp.where(real, _group_sum(top2, GROUP_SIZE), NEG_INF)
    grp_taken = _take_top(grp_score, lane // GROUP_SIZE, TOPK_GROUPS)
    masked = jnp.where(grp_taken & real, choice, NEG_INF)
    taken = _take_top(masked, lane, TOP_K)
    w = jnp.where(taken, scores, 0.0)
    w = w / jnp.sum(w, axis=-1, keepdims=True) * ROUTED_SCALE
    g_ref[...] = jnp.where(lane == N_EXPERTS, 1.0, w)


def _router(x2, w_router_pad, bias_pad):
    T = x2.shape[0]
    tm = ROW_TILE
    return pl.pallas_call(
        _router_kernel,
        grid=(T // tm,),
        in_specs=[pl.BlockSpec((tm, D_MODEL), lambda i: (i, 0)),
                  pl.BlockSpec(w_router_pad.shape, lambda i: (0, 0)),
                  pl.BlockSpec((1, LANES), lambda i: (0, 0))],
        out_specs=pl.BlockSpec((tm, LANES), lambda i: (i, 0)),
        out_shape=jax.ShapeDtypeStruct((T, LANES), F32),
        compiler_params=_cparams("parallel"),
    )(x2, w_router_pad, bias_pad)


def _moe_kernel(x_ref, gates_ref, wg_ref, wu_ref, wd_ref, g_ref, b_ref, o_ref, xb_sc, acc_sc):
    step = pl.program_id(1)
    last = pl.num_programs(1) - 1
    per = wg_ref.shape[0]

    @pl.when(step == 0)
    def _():
        xb_sc[...] = x_ref[...].astype(BF16)
        acc_sc[...] = jnp.zeros(acc_sc.shape, F32)

    xb = xb_sc[...]
    gates = gates_ref[...]
    lane = _lane_iota(gates.shape)
    hidden = []
    for j in range(per):
        hg = jnp.dot(xb, wg_ref[j], preferred_element_type=F32)
        hu = jnp.dot(xb, wu_ref[j], preferred_element_type=F32)
        col = jnp.sum(jnp.where(lane == step * per + j, gates, 0.0), axis=-1, keepdims=True)
        hidden.append((jax.nn.silu(hg) * hu * col).astype(BF16))
    h = jnp.concatenate(hidden, axis=-1)
    wd = wd_ref[...].reshape(per * EXPERT_FF, D_MODEL)
    acc_sc[...] += jnp.dot(h, wd, preferred_element_type=F32)

    @pl.when(step == last)
    def _():
        o_ref[...] = _layer_norm(DN_ALPHA * x_ref[...] + acc_sc[...], g_ref[...], b_ref[...])


def _moe_ln(x2, gates, wg, wu, wd, g, b):
    T = x2.shape[0]
    tm = MOE_TILE
    per = MOE_EXPERTS_PER_STEP
    n_steps = wg.shape[0] // per
    assert n_steps * per == wg.shape[0]
    return pl.pallas_call(
        _moe_kernel,
        grid=(T // tm, n_steps),
        in_specs=[pl.BlockSpec((tm, D_MODEL), lambda i, e: (i, 0)),
                  pl.BlockSpec((tm, LANES), lambda i, e: (i, 0)),
                  pl.BlockSpec((per, D_MODEL, EXPERT_FF), lambda i, e: (e, 0, 0)),
                  pl.BlockSpec((per, D_MODEL, EXPERT_FF), lambda i, e: (e, 0, 0)),
                  pl.BlockSpec((per, EXPERT_FF, D_MODEL), lambda i, e: (e, 0, 0)),
                  pl.BlockSpec((1, D_MODEL), lambda i, e: (0, 0)),
                  pl.BlockSpec((1, D_MODEL), lambda i, e: (0, 0))],
        out_specs=pl.BlockSpec((tm, D_MODEL), lambda i, e: (i, 0)),
        out_shape=jax.ShapeDtypeStruct((T, D_MODEL), F32),
        scratch_shapes=[pltpu.VMEM((tm, D_MODEL), BF16), pltpu.VMEM((tm, D_MODEL), F32)],
        compiler_params=_cparams("parallel", "arbitrary"),
    )(x2, gates, wg, wu, wd, g.reshape(1, -1), b.reshape(1, -1))


def _moe_block(x2, router, router_bias, exp_gate, exp_up, exp_down, sh_gate, sh_up, sh_down, g, b):
    w_router_pad = jnp.pad(router, ((0, 0), (0, LANES - N_EXPERTS)))
    bias_pad = jnp.pad(router_bias, (0, LANES - N_EXPERTS)).reshape(1, LANES)
    gates = _router(x2, w_router_pad, bias_pad)
    wg = jnp.concatenate([exp_gate, sh_gate[None]], 0).astype(BF16)
    wu = jnp.concatenate([exp_up, sh_up[None]], 0).astype(BF16)
    wd = jnp.concatenate([exp_down, sh_down[None]], 0).astype(BF16)
    return _moe_ln(x2, gates, wg, wu, wd, g, b)


def _qkv1_kernel(x_ref, w_ref, tab_ref, q_ref, k_ref, v_ref, kmean_ref, *, seq_tiles):
    tm = x_ref.shape[0]
    xb = x_ref[...].astype(BF16)
    width = MOBA_HEADS * HEAD_DIM
    lane = _lane_iota((tm, LANES))
    low = lane < HEAD_DIM
    pair_first = (lane % HEAD_DIM) < (HEAD_DIM // 2)
    cos_p, sin_p = tab_ref[0], tab_ref[1]
    pos = (pl.program_id(0) % seq_tiles) * tm + lax.broadcasted_iota(jnp.int32, (tm, 1), 0)
    onehot = _block_onehot(pos, MOBA_BLOCK, (tm, LANES))
    one_lane = jnp.where(lane == HEAD_DIM, 1.0, 0.0)
    wide = 2 * LANES
    for j in range(width // wide):
        q4 = jnp.dot(xb, w_ref[:, j * wide:(j + 1) * wide], preferred_element_type=F32)
        k4 = jnp.dot(xb, w_ref[:, width + j * wide:width + (j + 1) * wide], preferred_element_type=F32)
        v4 = jnp.dot(xb, w_ref[:, 2 * width + j * wide:2 * width + (j + 1) * wide],
                     preferred_element_type=F32)
        for half in range(2):
            c = j * wide + half * LANES
            part = slice(half * LANES, (half + 1) * LANES)
            q = _rope_lanes(q4[:, part], cos_p, sin_p, pair_first, 32)
            q_ref[:, c:c + LANES] = (q * (HEAD_DIM ** -0.5)).astype(BF16)
            k = _rope_lanes(k4[:, part], cos_p, sin_p, pair_first, 32)
            k_ref[:, 2 * c:2 * c + LANES] = jnp.where(low, k, onehot).astype(BF16)
            k_ref[:, 2 * c + LANES:2 * c + 2 * LANES] = jnp.where(
                low, pltpu.roll(k, HEAD_DIM, 1), onehot).astype(BF16)
            for blk in range(tm // MOBA_BLOCK):
                kmean_ref[blk, :, c:c + LANES] = jnp.mean(
                    k[blk * MOBA_BLOCK:(blk + 1) * MOBA_BLOCK], axis=0, keepdims=True)
            v = v4[:, part]
            v_ref[:, 2 * c:2 * c + LANES] = jnp.where(low, v, one_lane).astype(BF16)
            v_ref[:, 2 * c + LANES:2 * c + 2 * LANES] = jnp.where(
                low, pltpu.roll(v, HEAD_DIM, 1), one_lane).astype(BF16)


def _qkv1(x2, w_qkv, tabs, S):
    T = x2.shape[0]
    tm = ROW_TILE
    assert tm % MOBA_BLOCK == 0
    ns = S // tm
    width = MOBA_HEADS * HEAD_DIM
    row = lambda w: pl.BlockSpec((tm, w), lambda i: (i, 0))
    return pl.pallas_call(
        functools.partial(_qkv1_kernel, seq_tiles=ns),
        grid=(T // tm,),
        in_specs=[pl.BlockSpec((tm, D_MODEL), lambda i: (i, 0)),
                  pl.BlockSpec(w_qkv.shape, lambda i: (0, 0)),
                  pl.BlockSpec((2, tm, LANES), lambda i: (0, i % ns, 0))],
        out_specs=[row(width), row(2 * width), row(2 * width),
                   pl.BlockSpec((tm // MOBA_BLOCK, 1, width), lambda i: (i, 0, 0))],
        out_shape=[jax.ShapeDtypeStruct((T, width), BF16), jax.ShapeDtypeStruct((T, 2 * width), BF16),
                   jax.ShapeDtypeStruct((T, 2 * width), BF16),
                   jax.ShapeDtypeStruct((T // MOBA_BLOCK, 1, width), F32)],
        compiler_params=_cparams("parallel"),
    )(x2, w_qkv, tabs)


def _moba_select_kernel(q_ref, km_ref, qa_ref, *, n_top, n_blocks):
    i = pl.program_id(1)
    tq = q_ref.shape[1]
    gate = _dot_split_lhs_rhs(q_ref[0], km_ref[0])
    lane = _lane_iota(gate.shape)
    blk = lane % 8
    own = (i * tq + lax.broadcasted_iota(jnp.int32, (tq, 1), 0)) // MOBA_BLOCK
    score = jnp.where((blk < own) & (blk < n_blocks), gate, NEG_INF)
    rank = _group_rank(score, 8)
    chosen = ((score > NEG_INF) & (rank < n_top)) | (blk == own)
    bias = jnp.where(chosen, 0.0, -1e30)
    low = lane < HEAD_DIM
    in_bias = (lane >= HEAD_DIM) & (lane < HEAD_DIM + 8)
    for pair in range(MOBA_HEADS // 2):
        qf = q_ref[0, :, pair * LANES:(pair + 1) * LANES].astype(F32)
        for half in range(2):
            h = 2 * pair + half
            x = qf if half == 0 else pltpu.roll(qf, HEAD_DIM, 1)
            b = jnp.where(in_bias, pltpu.roll(bias, (HEAD_DIM - 8 * h) % LANES, 1), 0.0)
            qa_ref[0, :, h * LANES:(h + 1) * LANES] = jnp.where(low, x, b).astype(BF16)


def _dot_split_lhs_rhs(q_bf16, b):
    hi, lo = _split(b)
    return (jnp.dot(q_bf16, hi, preferred_element_type=F32)
            + jnp.dot(q_bf16, lo, preferred_element_type=F32))


def _moba_select(q3, km, n_top, n_blocks):
    B, S, width = q3.shape
    tq = ROW_TILE
    kern = functools.partial(_moba_select_kernel, n_top=n_top, n_blocks=n_blocks)
    return pl.pallas_call(
        kern,
        grid=(B, S // tq),
        in_specs=[pl.BlockSpec((1, tq, width), lambda b, i: (b, i, 0)),
                  pl.BlockSpec((1, width, LANES), lambda b, i: (b, 0, 0))],
        out_specs=pl.BlockSpec((1, tq, MOBA_HEADS * LANES), lambda b, i: (b, i, 0)),
        out_shape=jax.ShapeDtypeStruct((B, S, MOBA_HEADS * LANES), BF16),
        compiler_params=_cparams("parallel", "parallel"),
    )(q3, km)


def _layer0(x2, B, S, w_in, pe_k, pe_v, cmp_k1, cmp_k2, cmp_v1, cmp_v2,
            q_norm, w_uq, kv_norm, w_ukv, w_out, ln_g, ln_b):
    T = B * S
    G = NSA_KV_HEADS
    w_perm = _permute_columns(w_in, _proj0_columns()).astype(BF16)
    tabs = _rope_tables(S)
    qd = MLA_NOPE_DIM + MLA_ROPE_DIM
    wuq = jnp.pad(w_uq.reshape(MLA_Q_RANK, MLA_HEADS, qd), ((0, 0), (0, 0), (0, LANES - qd)))
    wuq = wuq.reshape(MLA_Q_RANK, MLA_HEADS * LANES).astype(BF16)
    wukv = w_ukv.reshape(MLA_KV_RANK, MLA_HEADS, MLA_NOPE_DIM + MLA_V_DIM)
    wk = jnp.pad(wukv[:, :, :MLA_NOPE_DIM], ((0, 0), (0, 0), (0, LANES - MLA_NOPE_DIM)))
    wk = wk.reshape(MLA_KV_RANK, MLA_HEADS * LANES).astype(BF16)
    wv = jnp.pad(wukv[:, :, MLA_NOPE_DIM:], ((0, 0), (0, 0), (0, LANES - MLA_V_DIM)))
    wv = wv.reshape(MLA_KV_RANK, MLA_HEADS * LANES).astype(BF16)

    q, ks, vs, kw, vw, kc, vc, gate, qm, km, vm = _proj0(
        x2, w_perm, tabs, q_norm.reshape(1, -1), kv_norm.reshape(1, -1), wuq, wk, wv, S)

    n_chunks = S // NSA_CMP_STRIDE
    chunk_w = NSA_CMP_STRIDE * HEAD_DIM

    def chunks(t):
        t = t.reshape(B, S, G, HEAD_DIM).transpose(0, 2, 1, 3)
        return t.reshape(B * G, n_chunks, chunk_w)

    pe = jnp.stack([pe_k[:NSA_CMP_STRIDE].reshape(-1), pe_k[NSA_CMP_STRIDE:].reshape(-1),
                    pe_v[:NSA_CMP_STRIDE].reshape(-1), pe_v[NSA_CMP_STRIDE:].reshape(-1)])
    pad2 = lambda w: jnp.pad(w, ((0, 0), (0, LANES - HEAD_DIM))).astype(BF16)
    kcmp, vcmp = _compress(chunks(kc), chunks(vc), pe, cmp_k1.astype(BF16), pad2(cmp_k2),
                           cmp_v1.astype(BF16), pad2(cmp_v2), _cmp_rope_tables(n_chunks))

    n_sel = S // NSA_SEL_BLOCK
    n_cmp = (S - NSA_CMP_LEN) // NSA_CMP_STRIDE + 1
    tok = np.arange(n_chunks)[:, None] * NSA_CMP_STRIDE + np.arange(NSA_CMP_LEN)[None, :]
    c2s = (tok[:, :, None] // NSA_SEL_BLOCK == np.arange(LANES)[None, None, :]).sum(1) / NSA_CMP_LEN
    c2s[n_cmp:] = 0.0
    c2s = jnp.asarray(c2s, dtype=BF16)

    q3 = q.reshape(B, S, -1)
    gate3 = gate.reshape(B, S, -1)
    o_cmp, q_aug = _cmp_attn(q3, kcmp, vcmp, gate3, c2s, n_sel)

    r3 = lambda t: t.reshape(B, S, -1)
    nsa_heads = [(r, 0, 0) for r in range(NSA_GROUP)]
    o_sel = _flash(q_aug, r3(ks), r3(vs), gate3, n_steps=G, q_w=NSA_GROUP * LANES, k_w=LANES,
                   v_w=LANES, heads=nsa_heads, gate_cols=[NSA_GROUP + r for r in range(NSA_GROUP)])
    o_win = _flash(q_aug, r3(kw), r3(vw), gate3, n_steps=G, q_w=NSA_GROUP * LANES, k_w=LANES,
                   v_w=LANES, heads=nsa_heads, window=NSA_WINDOW,
                   gate_cols=[2 * NSA_GROUP + r for r in range(NSA_GROUP)])
    o_mla = _flash(r3(qm), r3(km), r3(vm), None, n_steps=MLA_HEADS // FLASH_HEADS,
                   q_w=FLASH_HEADS * LANES, k_w=FLASH_HEADS * LANES, v_w=FLASH_HEADS * LANES,
                   heads=_OWN_KV_HEADS, out_dtype=BF16)

    n_nsa = NSA_HEADS * HEAD_DIM
    w_out_b = w_out.astype(BF16)
    groups = [([o_cmp.reshape(T, -1), o_sel.reshape(T, -1), o_win.reshape(T, -1)], w_out_b[:n_nsa]),
              ([o_mla.reshape(T, -1)], w_out_b[n_nsa:])]
    return _outproj_ln(groups, x2, ln_g, ln_b)


def _layer1(x2, B, S, w_qkv, w_out, ln_g, ln_b):
    T = B * S
    width = MOBA_HEADS * HEAD_DIM
    tabs = _rope_tables(S)[0:2]
    q, k, v, kmean = _qkv1(x2, w_qkv.astype(BF16), tabs, S)
    n_blocks = S // MOBA_BLOCK
    n_top = min(MOBA_TOPK, max(n_blocks - 1, 1))
    km = kmean.reshape(B, n_blocks, width).transpose(0, 2, 1)
    km = jnp.pad(km, ((0, 0), (0, 0), (0, 8 - n_blocks)))
    km = jnp.tile(km, (1, 1, MOBA_HEADS))
    diag = (np.arange(width)[:, None] // HEAD_DIM) == (np.arange(LANES)[None, :] // 8)
    km = jnp.where(jnp.asarray(diag)[None], km, 0.0)
    q3 = q.reshape(B, S, width)
    q_aug = _moba_select(q3, km, n_top, n_blocks)
    o = _flash(q_aug, k.reshape(B, S, 2 * width), v.reshape(B, S, 2 * width), None,
               n_steps=MOBA_HEADS // FLASH_HEADS, q_w=FLASH_HEADS * LANES, k_w=FLASH_HEADS * LANES,
               v_w=FLASH_HEADS * LANES, heads=_OWN_KV_HEADS, out_dtype=BF16)
    return _outproj_ln([([o.reshape(T, width)], w_out.astype(BF16))], x2, ln_g, ln_b)


def kernel(x, l0_w_in, l0_nsa_pe_k, l0_nsa_pe_v, l0_nsa_cmp_k1, l0_nsa_cmp_k2, l0_nsa_cmp_v1, l0_nsa_cmp_v2, l0_mla_q_norm, l0_mla_w_uq, l0_mla_kv_norm, l0_mla_w_ukv, l0_w_out, l0_ln1_g, l0_ln1_b, l0_router, l0_router_bias, l0_exp_gate, l0_exp_up, l0_exp_down, l0_sh_gate, l0_sh_up, l0_sh_down, l0_ln2_g, l0_ln2_b, l1_w_qkv, l1_w_out, l1_ln1_g, l1_ln1_b, l1_router, l1_router_bias, l1_exp_gate, l1_exp_up, l1_exp_down, l1_sh_gate, l1_sh_up, l1_sh_down, l1_ln2_g, l1_ln2_b):
    B, S, D = x.shape
    x2 = x.reshape(B * S, D)
    x2 = _layer0(x2, B, S, l0_w_in, l0_nsa_pe_k, l0_nsa_pe_v, l0_nsa_cmp_k1, l0_nsa_cmp_k2,
                 l0_nsa_cmp_v1, l0_nsa_cmp_v2, l0_mla_q_norm, l0_mla_w_uq, l0_mla_kv_norm,
                 l0_mla_w_ukv, l0_w_out, l0_ln1_g, l0_ln1_b)
    x2 = _moe_block(x2, l0_router, l0_router_bias, l0_exp_gate, l0_exp_up, l0_exp_down,
                    l0_sh_gate, l0_sh_up, l0_sh_down, l0_ln2_g, l0_ln2_b)
    x2 = _layer1(x2, B, S, l1_w_qkv, l1_w_out, l1_ln1_g, l1_ln1_b)
    x2 = _moe_block(x2, l1_router, l1_router_bias, l1_exp_gate, l1_exp_up, l1_exp_down,
                    l1_sh_gate, l1_sh_up, l1_sh_down, l1_ln2_g, l1_ln2_b)
    return x2.reshape(B, S, D)
```

```python
import functools

import numpy as np
import jax
import jax.numpy as jnp
from jax import lax
from jax.experimental import pallas as pl
from jax.experimental.pallas import tpu as pltpu

F32 = jnp.float32
BF16 = jnp.bfloat16

LANES = 128
VMEM_LIMIT = 48 * 1024 * 1024

D_MODEL = 1024
DEPTH = 2
HEAD_DIM = 64
ROPE_THETA = 10000.0
LN_EPS = 1e-5
RMS_EPS = 1e-6

NSA_HEADS = 8
NSA_KV_HEADS = 2
NSA_GROUP = NSA_HEADS // NSA_KV_HEADS
NSA_CMP_LEN = 32
NSA_CMP_STRIDE = 16
NSA_CMP_HIDDEN = 128
NSA_SEL_BLOCK = 64
NSA_SEL_TOPN = 8
NSA_WINDOW = 512

MLA_HEADS = 8
MLA_Q_RANK = 256
MLA_KV_RANK = 128
MLA_NOPE_DIM = 64
MLA_ROPE_DIM = 32
MLA_V_DIM = 64

MOBA_HEADS = 16
MOBA_BLOCK = 256
MOBA_TOPK = 3

N_EXPERTS = 64
N_GROUPS = 8
GROUP_SIZE = N_EXPERTS // N_GROUPS
TOPK_GROUPS = 4
TOP_K = 8
EXPERT_FF = 256
ROUTED_SCALE = 2.5

DN_ALPHA = (2 * DEPTH) ** 0.25

ROW_TILE = 512
ATTN_TQ = 256
MOE_TILE = 1024
MOE_EXPERTS_PER_STEP = 5
FLASH_HEADS = 4
_OWN_KV_HEADS = tuple((h, h, h) for h in range(FLASH_HEADS))

NEG_INF = float("-inf")


def _cparams(*sem):
    return pltpu.CompilerParams(dimension_semantics=sem, vmem_limit_bytes=VMEM_LIMIT)


def _dot(a, b):
    return jnp.dot(a.astype(BF16), b.astype(BF16), preferred_element_type=F32)


def _dot_nt(a, b):
    return lax.dot_general(a.astype(BF16), b.astype(BF16), (((1,), (1,)), ((), ())),
                           preferred_element_type=F32)


def _split(a):
    hi = a.astype(BF16)
    lo = (a - hi.astype(F32)).astype(BF16)
    return hi, lo


def _dot_split_lhs(a, b_bf16):
    hi, lo = _split(a)
    return (jnp.dot(hi, b_bf16, preferred_element_type=F32)
            + jnp.dot(lo, b_bf16, preferred_element_type=F32))


def _dot_split_both(a, b):
    ah, al = _split(a)
    bh, bl = _split(b)
    return (jnp.dot(ah, bh, preferred_element_type=F32)
            + jnp.dot(al, bh, preferred_element_type=F32)
            + jnp.dot(ah, bl, preferred_element_type=F32))


def _lane_iota(shape):
    return lax.broadcasted_iota(jnp.int32, shape, len(shape) - 1)


def _rope_lanes(x, cos, sin, first_half, half):
    n = x.shape[-1]
    rot = jnp.where(first_half, -pltpu.roll(x, n - half, 1), pltpu.roll(x, half, 1))
    return x * cos + rot * sin


def _layer_norm(z, g, b):
    mu = jnp.mean(z, axis=-1, keepdims=True)
    zc = z - mu
    var = jnp.mean(zc * zc, axis=-1, keepdims=True)
    return zc * lax.rsqrt(var + LN_EPS) * g + b


def _rms_norm(x, g):
    return x * lax.rsqrt(jnp.mean(x * x, axis=-1, keepdims=True) + RMS_EPS) * g


def _group_rank(x, group):
    n = x.shape[-1]
    pos = _lane_iota(x.shape) % group
    rank = jnp.zeros(x.shape, F32)
    for d in range(1, group):
        lower = pltpu.roll(x, d, 1)
        upper = pltpu.roll(x, n - d, 1)
        rank = rank + jnp.where((pos >= d) & (lower >= x), 1.0, 0.0)
        rank = rank + jnp.where((pos + d < group) & (upper > x), 1.0, 0.0)
    return rank


def _group_sum(x, group):
    n = x.shape[-1]
    pos = _lane_iota(x.shape) % group
    tot = x
    for d in range(1, group):
        tot = tot + jnp.where(pos >= d, pltpu.roll(x, d, 1), 0.0)
        tot = tot + jnp.where(pos + d < group, pltpu.roll(x, n - d, 1), 0.0)
    return tot


def _take_top(x, key, k):
    big = jnp.int32(1 << 30)
    taken = jnp.zeros(x.shape, jnp.bool_)
    for _ in range(k):
        m = jnp.max(x, axis=-1, keepdims=True)
        first = jnp.min(jnp.where(x == m, key, big), axis=-1, keepdims=True)
        hit = key == first
        taken = taken | hit
        x = jnp.where(hit, NEG_INF, x)
    return taken


def _rope_tables(S):
    pos = jnp.arange(S, dtype=F32)[:, None]

    def cs(half):
        inv = ROPE_THETA ** (-jnp.arange(half, dtype=F32) / half)
        ang = pos * inv[None, :]
        return jnp.cos(ang), jnp.sin(ang)

    c32, s32 = cs(HEAD_DIM // 2)
    c16, s16 = cs(MLA_ROPE_DIM // 2)
    one = jnp.ones((S, 1), F32)
    zero = jnp.zeros((S, 1), F32)
    cos_pair = jnp.concatenate([c32] * 4, -1)
    sin_pair = jnp.concatenate([s32] * 4, -1)
    cos_kv = jnp.concatenate([c32, c32, jnp.tile(one, (1, 64))], -1)
    sin_kv = jnp.concatenate([s32, s32, jnp.tile(zero, (1, 64))], -1)
    cos_mla = jnp.concatenate([jnp.tile(one, (1, 64)), c16, c16, jnp.tile(one, (1, 32))], -1)
    sin_mla = jnp.concatenate([jnp.tile(zero, (1, 64)), s16, s16, jnp.tile(zero, (1, 32))], -1)
    return jnp.stack([cos_pair, sin_pair, cos_kv, sin_kv, cos_mla, sin_mla])


def _cmp_rope_tables(n_chunks):
    pos = (jnp.arange(n_chunks, dtype=F32) * NSA_CMP_STRIDE + (NSA_CMP_LEN - 1))[:, None]
    half = HEAD_DIM // 2
    inv = ROPE_THETA ** (-jnp.arange(half, dtype=F32) / half)
    ang = pos * inv[None, :]
    c, s = jnp.cos(ang), jnp.sin(ang)
    one = jnp.ones((n_chunks, 64), F32)
    return jnp.stack([jnp.concatenate([c, c, one], -1), jnp.concatenate([s, s, 0 * one], -1)])


P0_Q = 0
P0_KVS = 512
P0_KVW = 768
P0_KC = 1024
P0_VC = 1152
P0_GATE = 1280
P0_CQ = 1536
P0_CKV = 1792
P0_KR = 1920
P0_W = 2048


def _proj0_columns():
    q_w = NSA_HEADS * HEAD_DIM
    kv0 = q_w
    piece = NSA_KV_HEADS * HEAD_DIM
    gate0 = kv0 + 6 * piece
    cq0 = gate0 + 3 * NSA_HEADS
    ckv0 = cq0 + MLA_Q_RANK
    kr0 = ckv0 + MLA_KV_RANK
    src = -np.ones((P0_W,), np.int64)
    src[P0_Q:P0_Q + q_w] = np.arange(q_w)
    d = np.arange(HEAD_DIM)
    for base, kp, vp in ((P0_KVS, 2, 3), (P0_KVW, 4, 5)):
        for g in range(NSA_KV_HEADS):
            src[base + g * 128 + d] = kv0 + kp * piece + g * HEAD_DIM + d
            src[base + g * 128 + 64 + d] = kv0 + vp * piece + g * HEAD_DIM + d
    src[P0_KC:P0_KC + piece] = kv0 + 0 * piece + np.arange(piece)
    src[P0_VC:P0_VC + piece] = kv0 + 1 * piece + np.arange(piece)
    for g in range(NSA_KV_HEADS):
        for br in range(3):
            for r in range(NSA_GROUP):
                src[P0_GATE + g * 128 + br * NSA_GROUP + r] = gate0 + br * NSA_HEADS + g * NSA_GROUP + r
    src[P0_CQ:P0_CQ + MLA_Q_RANK] = cq0 + np.arange(MLA_Q_RANK)
    src[P0_CKV:P0_CKV + MLA_KV_RANK] = ckv0 + np.arange(MLA_KV_RANK)
    src[P0_KR + 64:P0_KR + 64 + MLA_ROPE_DIM] = kr0 + np.arange(MLA_ROPE_DIM)
    return src


def _permute_columns(w, src):
    cols = jnp.take(w, jnp.asarray(np.maximum(src, 0)), axis=1)
    return jnp.where(jnp.asarray(src >= 0)[None, :], cols, 0.0)


def _block_onehot(pos, block, shape):
    return jnp.where(_lane_iota(shape) == HEAD_DIM + pos // block, 1.0, 0.0)


def _proj0_kernel(x_ref, w_ref, tab_ref, qg_ref, kvg_ref, wuq_ref, wk_ref, wv_ref,
                  q_ref, ks_ref, vs_ref, kw_ref, vw_ref, kc_ref, vc_ref, gate_ref,
                  qm_ref, km_ref, vm_ref, *, seq_tiles):
    tm = x_ref.shape[0]
    xb = x_ref[...].astype(BF16)
    lane = _lane_iota((tm, LANES))
    low = lane < HEAD_DIM
    pair_first = (lane % HEAD_DIM) < (HEAD_DIM // 2)
    mla_first = lane < (MLA_NOPE_DIM + MLA_ROPE_DIM // 2)
    cos_p, sin_p = tab_ref[0], tab_ref[1]
    cos_kv, sin_kv = tab_ref[2], tab_ref[3]
    cos_m, sin_m = tab_ref[4], tab_ref[5]
    pos = (pl.program_id(0) % seq_tiles) * tm + lax.broadcasted_iota(jnp.int32, (tm, 1), 0)
    sel_onehot = _block_onehot(pos, NSA_SEL_BLOCK, (tm, LANES))
    one_lane = jnp.where(lane == HEAD_DIM, 1.0, 0.0)

    def seg(c0, width):
        return jnp.dot(xb, w_ref[:, c0:c0 + width], preferred_element_type=F32)

    q = seg(P0_Q, 512)
    for j in range(4):
        blk = _rope_lanes(q[:, j * 128:(j + 1) * 128], cos_p, sin_p, pair_first, 32)
        q_ref[:, j * 128:(j + 1) * 128] = (blk * (HEAD_DIM ** -0.5)).astype(BF16)
    for c0, k_out, v_out, extra in ((P0_KVS, ks_ref, vs_ref, sel_onehot), (P0_KVW, kw_ref, vw_ref, 0.0)):
        kv = seg(c0, 256)
        for j in range(2):
            blk = _rope_lanes(kv[:, j * 128:(j + 1) * 128], cos_kv, sin_kv, pair_first, 32)
            k_out[:, j * 128:(j + 1) * 128] = jnp.where(low, blk, extra).astype(BF16)
            v_out[:, j * 128:(j + 1) * 128] = jnp.where(low, pltpu.roll(blk, HEAD_DIM, 1), one_lane).astype(BF16)
    kc_ref[...] = seg(P0_KC, 128)
    vc_ref[...] = seg(P0_VC, 128)
    gate_ref[...] = seg(P0_GATE, 256)

    cq = _rms_norm(seg(P0_CQ, 256), qg_ref[...])
    qm = _dot(cq, wuq_ref[...])
    mla_scale = (MLA_NOPE_DIM + MLA_ROPE_DIM) ** -0.5
    for h in range(MLA_HEADS):
        blk = _rope_lanes(qm[:, h * 128:(h + 1) * 128], cos_m, sin_m, mla_first, 16)
        qm_ref[:, h * 128:(h + 1) * 128] = (blk * mla_scale).astype(BF16)
    ckv = _rms_norm(seg(P0_CKV, 128), kvg_ref[...]).astype(BF16)
    kn = jnp.dot(ckv, wk_ref[...], preferred_element_type=F32)
    kpe = _rope_lanes(seg(P0_KR, 128), cos_m, sin_m, mla_first, 16)
    for h in range(MLA_HEADS):
        km_ref[:, h * 128:(h + 1) * 128] = (kn[:, h * 128:(h + 1) * 128] + kpe).astype(BF16)
    vm = jnp.dot(ckv, wv_ref[...], preferred_element_type=F32)
    for h in range(MLA_HEADS):
        vm_ref[:, h * 128:(h + 1) * 128] = jnp.where(low, vm[:, h * 128:(h + 1) * 128], one_lane).astype(BF16)


def _proj0(x2, w_perm, tabs, q_norm, kv_norm, wuq, wk, wv, S):
    T = x2.shape[0]
    tm = ROW_TILE
    ns = S // tm
    row = lambda w: pl.BlockSpec((tm, w), lambda i: (i, 0))
    full = lambda a: pl.BlockSpec(a.shape, lambda i: (0,) * a.ndim)
    widths = (512, 256, 256, 256, 256, 128, 128, 256, 1024, 1024, 1024)
    dtypes = (BF16, BF16, BF16, BF16, BF16, F32, F32, F32, BF16, BF16, BF16)
    return pl.pallas_call(
        functools.partial(_proj0_kernel, seq_tiles=ns),
        grid=(T // tm,),
        in_specs=[row(D_MODEL), full(w_perm),
                  pl.BlockSpec((6, tm, LANES), lambda i: (0, i % ns, 0)),
                  full(q_norm), full(kv_norm), full(wuq), full(wk), full(wv)],
        out_specs=[row(w) for w in widths],
        out_shape=[jax.ShapeDtypeStruct((T, w), d) for w, d in zip(widths, dtypes)],
        compiler_params=_cparams("parallel"),
    )(x2, w_perm, tabs, q_norm, kv_norm, wuq, wk, wv)


def _compress_kernel(kc_ref, vc_ref, pe_ref, w1k_ref, w2k_ref, w1v_ref, w2v_ref, tab_ref,
                     ko_ref, vo_ref):
    half = NSA_CMP_STRIDE * HEAD_DIM
    n = kc_ref.shape[1]

    def mlp(x, pe_lo, pe_hi, w1_ref, w2_ref):
        first = _dot(x + pe_lo, w1_ref[0:half, :])
        second = _dot(x + pe_hi, w1_ref[half:2 * half, :])
        hidden = first + pltpu.roll(second, n - 1, 0)
        return _dot(jax.nn.gelu(hidden), w2_ref[...])

    k = mlp(kc_ref[0], pe_ref[0:1, :], pe_ref[1:2, :], w1k_ref, w2k_ref)
    lane = _lane_iota(k.shape)
    ko_ref[0] = _rope_lanes(k, tab_ref[0], tab_ref[1], lane < HEAD_DIM // 2, 32).astype(BF16)
    vo_ref[0] = mlp(vc_ref[0], pe_ref[2:3, :], pe_ref[3:4, :], w1v_ref, w2v_ref).astype(BF16)


def _compress(kc_chunks, vc_chunks, pe, w1k, w2k, w1v, w2v, ctab):
    n_bg, n, width = kc_chunks.shape
    blk = pl.BlockSpec((1, n, width), lambda i: (i, 0, 0))
    full = lambda a: pl.BlockSpec(a.shape, lambda i: (0,) * a.ndim)
    out = pl.BlockSpec((1, n, LANES), lambda i: (i, 0, 0))
    return pl.pallas_call(
        _compress_kernel,
        grid=(n_bg,),
        in_specs=[blk, blk, full(pe), full(w1k), full(w2k), full(w1v), full(w2v), full(ctab)],
        out_specs=[out, out],
        out_shape=[jax.ShapeDtypeStruct((n_bg, n, LANES), BF16)] * 2,
        compiler_params=_cparams("parallel"),
    )(kc_chunks, vc_chunks, pe, w1k, w2k, w1v, w2v, ctab)


def _cmp_attn_kernel(q_ref, k_ref, v_ref, gate_ref, c2s_ref, o_ref, qa_ref, *, n_sel, n_top):
    qi = pl.program_id(2)
    tq = q_ref.shape[1]
    n = k_ref.shape[1]
    q = q_ref[0]
    k = k_ref[0][:, 0:HEAD_DIM]
    v = v_ref[0][:, 0:HEAD_DIM]
    gates = jax.nn.sigmoid(gate_ref[0])
    pos = qi * tq + lax.broadcasted_iota(jnp.int32, (tq, 1), 0)
    cmp_end = lax.broadcasted_iota(jnp.int32, (1, n), 1) * NSA_CMP_STRIDE + (NSA_CMP_LEN - 1)
    visible = cmp_end <= pos
    p_sum = jnp.zeros((tq, n), F32)
    outs = []
    for r in range(NSA_GROUP):
        s = jnp.where(visible, _dot_nt(q[:, r * HEAD_DIM:(r + 1) * HEAD_DIM], k), NEG_INF)
        m = jnp.max(s, axis=-1, keepdims=True)
        e = jnp.exp(s - jnp.where(m > NEG_INF, m, 0.0))
        den = jnp.sum(e, axis=-1, keepdims=True)
        p = e / jnp.where(den > 0, den, 1.0)
        p_sum = p_sum + p
        outs.append(_dot(p, v) * gates[:, r:r + 1])
    o_ref[0] = jnp.concatenate(outs, axis=-1)

    imp = _dot_split_lhs(p_sum, c2s_ref[...])
    blk = _lane_iota(imp.shape)
    cur = pos // NSA_SEL_BLOCK
    forced = (blk == 0) | (blk == cur) | (blk == cur - 1)
    valid = (blk <= cur) & (blk < n_sel)
    free = jnp.where(valid & ~forced, imp, NEG_INF)
    taken = forced | _take_top(free, blk, n_top - 3)
    bias = jnp.where(blk < n_sel, jnp.where(taken & valid, 0.0, -1e30), 0.0)
    bias = pltpu.roll(bias, HEAD_DIM, 1)
    low = blk < HEAD_DIM
    qf = q.astype(F32)
    for r in range(NSA_GROUP):
        x = qf[:, (r // 2) * LANES:(r // 2 + 1) * LANES]
        if r % 2 == 1:
            x = pltpu.roll(x, HEAD_DIM, 1)
        qa_ref[0, :, r * LANES:(r + 1) * LANES] = jnp.where(low, x, bias).astype(BF16)


def _cmp_attn(q3, kcmp, vcmp, gate3, c2s, n_sel):
    B, S, _ = q3.shape
    n = kcmp.shape[1]
    tq = ROW_TILE
    G = NSA_KV_HEADS
    assert n_sel >= 3 and NSA_SEL_TOPN >= 3
    kern = functools.partial(_cmp_attn_kernel, n_sel=n_sel, n_top=min(NSA_SEL_TOPN, n_sel))
    return pl.pallas_call(
        kern,
        grid=(B, G, S // tq),
        in_specs=[pl.BlockSpec((1, tq, 256), lambda b, g, i: (b, i, g)),
                  pl.BlockSpec((1, n, LANES), lambda b, g, i: (b * G + g, 0, 0)),
                  pl.BlockSpec((1, n, LANES), lambda b, g, i: (b * G + g, 0, 0)),
                  pl.BlockSpec((1, tq, LANES), lambda b, g, i: (b, i, g)),
                  pl.BlockSpec(c2s.shape, lambda b, g, i: (0, 0))],
        out_specs=[pl.BlockSpec((1, tq, 256), lambda b, g, i: (b, i, g)),
                   pl.BlockSpec((1, tq, 512), lambda b, g, i: (b, i, g))],
        out_shape=[jax.ShapeDtypeStruct((B, S, 512), F32),
                   jax.ShapeDtypeStruct((B, S, NSA_HEADS * LANES), BF16)],
        compiler_params=_cparams("parallel", "parallel", "parallel"),
    )(q3, kcmp, vcmp, gate3, c2s)


def _flash_kernel(*refs, heads, window, gate_cols):
    it = iter(refs)
    q_ref, k_ref, v_ref = next(it), next(it), next(it)
    gate_ref = next(it) if gate_cols is not None else None
    o_ref = next(it)
    s_sc, mx_sc, mb_sc, acc_sc = next(it), next(it), next(it), next(it)

    qi = pl.program_id(2)
    tq = q_ref.shape[1]
    tk = tq
    nh = len(heads)

    lane = _lane_iota((tq, LANES))
    low = lane < HEAD_DIM
    q_heads = [q_ref[0, :, g * LANES:(g + 1) * LANES] for (g, _, _) in heads]
    mx_sc[...] = jnp.full(mx_sc.shape, NEG_INF, F32)
    acc_sc[...] = jnp.zeros(acc_sc.shape, F32)

    def score_tile(j, pos_mask):
        start = pl.multiple_of(j * tk, tk)
        k = k_ref[0, pl.ds(start, tk), :]
        for h, (_, kg, _) in enumerate(heads):
            s = _dot_nt(q_heads[h], k[:, kg * LANES:(kg + 1) * LANES])
            if pos_mask is not None:
                s = jnp.where(pos_mask, s, NEG_INF)
            s_sc[h, j] = s
            best = s[:, 0:LANES]
            for c in range(1, tk // LANES):
                best = jnp.maximum(best, s[:, c * LANES:(c + 1) * LANES])
            mx_sc[h] = jnp.maximum(mx_sc[h], best)

    def value_tile(j):
        start = pl.multiple_of(j * tk, tk)
        v = v_ref[0, pl.ds(start, tk), :]
        for h, (_, _, vg) in enumerate(heads):
            mb = mb_sc[h]
            s = s_sc[h, j]
            p = jnp.concatenate([jnp.exp(s[:, c * LANES:(c + 1) * LANES] - mb)
                                 for c in range(tk // LANES)], axis=-1).astype(BF16)
            acc_sc[h] += jnp.dot(p, v[:, vg * LANES:(vg + 1) * LANES], preferred_element_type=F32)

    def for_each_tile(fn_full, fn_masked):
        row = lax.broadcasted_iota(jnp.int32, (tq, tk), 0)
        col = lax.broadcasted_iota(jnp.int32, (tq, tk), 1)
        first_full = 0
        if window is not None:
            back = window // tk
            first_full = jnp.maximum(qi - back + 1, 0)

            @pl.when(qi >= back)
            def _():
                fn_masked(qi - back, col > row)

        def body(j, carry):
            fn_full(j)
            return carry

        lax.fori_loop(first_full, qi, body, 0)
        fn_masked(qi, col <= row)

    for_each_tile(lambda j: score_tile(j, None), score_tile)
    for h in range(nh):
        m = jnp.max(mx_sc[h], axis=-1, keepdims=True)
        mb_sc[h] = jnp.broadcast_to(m, (tq, LANES))
    for_each_tile(value_tile, lambda j, mask: value_tile(j))

    if gate_ref is not None:
        gates = jax.nn.sigmoid(gate_ref[0])
    results = []
    for h in range(nh):
        acc = acc_sc[h]
        o = acc / acc[:, HEAD_DIM:HEAD_DIM + 1]
        if gate_ref is not None:
            c = gate_cols[h]
            o = o * gates[:, c:c + 1]
        results.append(o)
    for pair in range(nh // 2):
        high = pltpu.roll(results[2 * pair + 1], HEAD_DIM, 1)
        o_ref[0, :, pair * LANES:(pair + 1) * LANES] = jnp.where(low, results[2 * pair], high).astype(o_ref.dtype)


def _flash(q, k, v, gate, *, n_steps, q_w, k_w, v_w, heads, window=None, gate_cols=None,
           out_dtype=F32):
    B, S, _ = q.shape
    tq = ATTN_TQ
    nh = len(heads)
    assert nh % 2 == 0 and S % tq == 0 and (window is None or window % tq == 0)
    n_stash = S // tq
    in_specs = [pl.BlockSpec((1, tq, q_w), lambda b, h, i: (b, i, h)),
                pl.BlockSpec((1, S, k_w), lambda b, h, i: (b, 0, h)),
                pl.BlockSpec((1, S, v_w), lambda b, h, i: (b, 0, h))]
    args = [q, k, v]
    if gate is not None:
        in_specs.append(pl.BlockSpec((1, tq, LANES), lambda b, h, i: (b, i, h)))
        args.append(gate)
    kern = functools.partial(_flash_kernel, heads=tuple(heads), window=window, gate_cols=gate_cols)
    return pl.pallas_call(
        kern,
        grid=(B, n_steps, S // tq),
        in_specs=in_specs,
        out_specs=pl.BlockSpec((1, tq, nh * HEAD_DIM), lambda b, h, i: (b, i, h)),
        out_shape=jax.ShapeDtypeStruct((B, S, n_steps * nh * HEAD_DIM), out_dtype),
        scratch_shapes=[pltpu.VMEM((nh, n_stash, tq, tq), F32), pltpu.VMEM((nh, tq, LANES), F32),
                        pltpu.VMEM((nh, tq, LANES), F32), pltpu.VMEM((nh, tq, LANES), F32)],
        compiler_params=_cparams("parallel", "parallel", "arbitrary"),
    )(*args)


def _outproj_kernel(*refs, group_sizes):
    it = iter(refs)
    y = None
    for n_in in group_sizes:
        acts = [next(it)[...].astype(F32) for _ in range(n_in)]
        w_ref = next(it)
        a = acts[0]
        for extra in acts[1:]:
            a = a + extra
        part = _dot(a, w_ref[...])
        y = part if y is None else y + part
    x_ref, g_ref, b_ref, o_ref = next(it), next(it), next(it), next(it)
    o_ref[...] = _layer_norm(DN_ALPHA * x_ref[...] + y, g_ref[...], b_ref[...])


def _outproj_ln(groups, x2, g, b):
    T = x2.shape[0]
    tm = ROW_TILE
    in_specs, args, sizes = [], [], []
    for acts, w in groups:
        for a in acts:
            in_specs.append(pl.BlockSpec((tm, a.shape[1]), lambda i: (i, 0)))
            args.append(a)
        in_specs.append(pl.BlockSpec(w.shape, lambda i: (0, 0)))
        args.append(w)
        sizes.append(len(acts))
    in_specs += [pl.BlockSpec((tm, D_MODEL), lambda i: (i, 0)),
                 pl.BlockSpec((1, D_MODEL), lambda i: (0, 0)),
                 pl.BlockSpec((1, D_MODEL), lambda i: (0, 0))]
    args += [x2, g.reshape(1, -1), b.reshape(1, -1)]
    return pl.pallas_call(
        functools.partial(_outproj_kernel, group_sizes=tuple(sizes)),
        grid=(T // tm,),
        in_specs=in_specs,
        out_specs=pl.BlockSpec((tm, D_MODEL), lambda i: (i, 0)),
        out_shape=jax.ShapeDtypeStruct((T, D_MODEL), F32),
        compiler_params=_cparams("parallel"),
    )(*args)


def _router_kernel(x_ref, w_ref, bias_ref, g_ref):
    logits = _dot_split_both(x_ref[...], w_ref[...])
    lane = _lane_iota(logits.shape)
    real = lane < N_EXPERTS
    scores = jax.nn.sigmoid(logits)
    choice = jnp.where(real, scores + bias_ref[...], NEG_INF)
    top2 = jnp.where(_group_rank(choice, GROUP_SIZE) < 2, choice, 0.0)
    grp_score = jnp.where(real, _group_sum(top2, GROUP_SIZE), NEG_INF)
    grp_taken = _take_top(grp_score, lane // GROUP_SIZE, TOPK_GROUPS)
    masked = jnp.where(grp_taken & real, choice, NEG_INF)
    taken = _take_top(masked, lane, TOP_K)
    w = jnp.where(taken, scores, 0.0)
    w = w / jnp.sum(w, axis=-1, keepdims=True) * ROUTED_SCALE
    g_ref[...] = jnp.where(lane == N_EXPERTS, 1.0, w)


def _router(x2, w_router_pad, bias_pad):
    T = x2.shape[0]
    tm = ROW_TILE
    return pl.pallas_call(
        _router_kernel,
        grid=(T // tm,),
        in_specs=[pl.BlockSpec((tm, D_MODEL), lambda i: (i, 0)),
                  pl.BlockSpec(w_router_pad.shape, lambda i: (0, 0)),
                  pl.BlockSpec((1, LANES), lambda i: (0, 0))],
        out_specs=pl.BlockSpec((tm, LANES), lambda i: (i, 0)),
        out_shape=jax.ShapeDtypeStruct((T, LANES), F32),
        compiler_params=_cparams("parallel"),
    )(x2, w_router_pad, bias_pad)


def _moe_kernel(x_ref, gates_ref, wg_ref, wu_ref, wd_ref, g_ref, b_ref, o_ref, xb_sc, acc_sc):
    step = pl.program_id(1)
    last = pl.num_programs(1) - 1
    per = wg_ref.shape[0]

    @pl.when(step == 0)
    def _():
        xb_sc[...] = x_ref[...].astype(BF16)
        acc_sc[...] = jnp.zeros(acc_sc.shape, F32)

    xb = xb_sc[...]
    gates = gates_ref[...]
    lane = _lane_iota(gates.shape)
    hidden = []
    for j in range(per):
        hg = jnp.dot(xb, wg_ref[j], preferred_element_type=F32)
        hu = jnp.dot(xb, wu_ref[j], preferred_element_type=F32)
        col = jnp.sum(jnp.where(lane == step * per + j, gates, 0.0), axis=-1, keepdims=True)
        hidden.append((jax.nn.silu(hg) * hu * col).astype(BF16))
    h = jnp.concatenate(hidden, axis=-1)
    wd = wd_ref[...].reshape(per * EXPERT_FF, D_MODEL)
    acc_sc[...] += jnp.dot(h, wd, preferred_element_type=F32)

    @pl.when(step == last)
    def _():
        o_ref[...] = _layer_norm(DN_ALPHA * x_ref[...] + acc_sc[...], g_ref[...], b_ref[...])


def _moe_ln(x2, gates, wg, wu, wd, g, b):
    T = x2.shape[0]
    tm = MOE_TILE
    per = MOE_EXPERTS_PER_STEP
    n_steps = wg.shape[0] // per
    assert n_steps * per == wg.shape[0]
    return pl.pallas_call(
        _moe_kernel,
        grid=(T // tm, n_steps),
        in_specs=[pl.BlockSpec((tm, D_MODEL), lambda i, e: (i, 0)),
                  pl.BlockSpec((tm, LANES), lambda i, e: (i, 0)),
                  pl.BlockSpec((per, D_MODEL, EXPERT_FF), lambda i, e: (e, 0, 0)),
                  pl.BlockSpec((per, D_MODEL, EXPERT_FF), lambda i, e: (e, 0, 0)),
                  pl.BlockSpec((per, EXPERT_FF, D_MODEL), lambda i, e: (e, 0, 0)),
                  pl.BlockSpec((1, D_MODEL), lambda i, e: (0, 0)),
                  pl.BlockSpec((1, D_MODEL), lambda i, e: (0, 0))],
        out_specs=pl.BlockSpec((tm, D_MODEL), lambda i, e: (i, 0)),
        out_shape=jax.ShapeDtypeStruct((T, D_MODEL), F32),
        scratch_shapes=[pltpu.VMEM((tm, D_MODEL), BF16), pltpu.VMEM((tm, D_MODEL), F32)],
        compiler_params=_cparams("parallel", "arbitrary"),
    )(x2, gates, wg, wu, wd, g.reshape(1, -1), b.reshape(1, -1))


def _moe_block(x2, router, router_bias, exp_gate, exp_up, exp_down, sh_gate, sh_up, sh_down, g, b):
    w_router_pad = jnp.pad(router, ((0, 0), (0, LANES - N_EXPERTS)))
    bias_pad = jnp.pad(router_bias, (0, LANES - N_EXPERTS)).reshape(1, LANES)
    gates = _router(x2, w_router_pad, bias_pad)
    wg = jnp.concatenate([exp_gate, sh_gate[None]], 0).astype(BF16)
    wu = jnp.concatenate([exp_up, sh_up[None]], 0).astype(BF16)
    wd = jnp.concatenate([exp_down, sh_down[None]], 0).astype(BF16)
    return _moe_ln(x2, gates, wg, wu, wd, g, b)


def _qkv1_kernel(x_ref, w_ref, tab_ref, q_ref, k_ref, v_ref, kmean_ref, *, seq_tiles):
    tm = x_ref.shape[0]
    xb = x_ref[...].astype(BF16)
    width = MOBA_HEADS * HEAD_DIM
    lane = _lane_iota((tm, LANES))
    low = lane < HEAD_DIM
    pair_first = (lane % HEAD_DIM) < (HEAD_DIM // 2)
    cos_p, sin_p = tab_ref[0], tab_ref[1]
    pos = (pl.program_id(0) % seq_tiles) * tm + lax.broadcasted_iota(jnp.int32, (tm, 1), 0)
    onehot = _block_onehot(pos, MOBA_BLOCK, (tm, LANES))
    one_lane = jnp.where(lane == HEAD_DIM, 1.0, 0.0)
    wide = 2 * LANES
    for j in range(width // wide):
        q4 = jnp.dot(xb, w_ref[:, j * wide:(j + 1) * wide], preferred_element_type=F32)
        k4 = jnp.dot(xb, w_ref[:, width + j * wide:width + (j + 1) * wide], preferred_element_type=F32)
        v4 = jnp.dot(xb, w_ref[:, 2 * width + j * wide:2 * width + (j + 1) * wide],
                     preferred_element_type=F32)
        for half in range(2):
            c = j * wide + half * LANES
            part = slice(half * LANES, (half + 1) * LANES)
            q = _rope_lanes(q4[:, part], cos_p, sin_p, pair_first, 32)
            q_ref[:, c:c + LANES] = (q * (HEAD_DIM ** -0.5)).astype(BF16)
            k = _rope_lanes(k4[:, part], cos_p, sin_p, pair_first, 32)
            k_ref[:, 2 * c:2 * c + LANES] = jnp.where(low, k, onehot).astype(BF16)
            k_ref[:, 2 * c + LANES:2 * c + 2 * LANES] = jnp.where(
                low, pltpu.roll(k, HEAD_DIM, 1), onehot).astype(BF16)
            for blk in range(tm // MOBA_BLOCK):
                kmean_ref[blk, :, c:c + LANES] = jnp.mean(
                    k[blk * MOBA_BLOCK:(blk + 1) * MOBA_BLOCK], axis=0, keepdims=True)
            v = v4[:, part]
            v_ref[:, 2 * c:2 * c + LANES] = jnp.where(low, v, one_lane).astype(BF16)
            v_ref[:, 2 * c + LANES:2 * c + 2 * LANES] = jnp.where(
                low, pltpu.roll(v, HEAD_DIM, 1), one_lane).astype(BF16)


def _qkv1(x2, w_qkv, tabs, S):
    T = x2.shape[0]
    tm = ROW_TILE
    assert tm % MOBA_BLOCK == 0
    ns = S // tm
    width = MOBA_HEADS * HEAD_DIM
    row = lambda w: pl.BlockSpec((tm, w), lambda i: (i, 0))
    return pl.pallas_call(
        functools.partial(_qkv1_kernel, seq_tiles=ns),
        grid=(T // tm,),
        in_specs=[pl.BlockSpec((tm, D_MODEL), lambda i: (i, 0)),
                  pl.BlockSpec(w_qkv.shape, lambda i: (0, 0)),
                  pl.BlockSpec((2, tm, LANES), lambda i: (0, i % ns, 0))],
        out_specs=[row(width), row(2 * width), row(2 * width),
                   pl.BlockSpec((tm // MOBA_BLOCK, 1, width), lambda i: (i, 0, 0))],
        out_shape=[jax.ShapeDtypeStruct((T, width), BF16), jax.ShapeDtypeStruct((T, 2 * width), BF16),
                   jax.ShapeDtypeStruct((T, 2 * width), BF16),
                   jax.ShapeDtypeStruct((T // MOBA_BLOCK, 1, width), F32)],
        compiler_params=_cparams("parallel"),
    )(x2, w_qkv, tabs)


def _moba_select_kernel(q_ref, km_ref, qa_ref, *, n_top, n_blocks):
    i = pl.program_id(1)
    tq = q_ref.shape[1]
    gate = _dot_split_lhs_rhs(q_ref[0], km_ref[0])
    lane = _lane_iota(gate.shape)
    blk = lane % 8
    own = (i * tq + lax.broadcasted_iota(jnp.int32, (tq, 1), 0)) // MOBA_BLOCK
    score = jnp.where((blk < own) & (blk < n_blocks), gate, NEG_INF)
    rank = _group_rank(score, 8)
    chosen = ((score > NEG_INF) & (rank < n_top)) | (blk == own)
    bias = jnp.where(chosen, 0.0, -1e30)
    low = lane < HEAD_DIM
    in_bias = (lane >= HEAD_DIM) & (lane < HEAD_DIM + 8)
    for pair in range(MOBA_HEADS // 2):
        qf = q_ref[0, :, pair * LANES:(pair + 1) * LANES].astype(F32)
        for half in range(2):
            h = 2 * pair + half
            x = qf if half == 0 else pltpu.roll(qf, HEAD_DIM, 1)
            b = jnp.where(in_bias, pltpu.roll(bias, (HEAD_DIM - 8 * h) % LANES, 1), 0.0)
            qa_ref[0, :, h * LANES:(h + 1) * LANES] = jnp.where(low, x, b).astype(BF16)


def _dot_split_lhs_rhs(q_bf16, b):
    hi, lo = _split(b)
    return (jnp.dot(q_bf16, hi, preferred_element_type=F32)
            + jnp.dot(q_bf16, lo, preferred_element_type=F32))


def _moba_select(q3, km, n_top, n_blocks):
    B, S, width = q3.shape
    tq = ROW_TILE
    kern = functools.partial(_moba_select_kernel, n_top=n_top, n_blocks=n_blocks)
    return pl.pallas_call(
        kern,
        grid=(B, S // tq),
        in_specs=[pl.BlockSpec((1, tq, width), lambda b, i: (b, i, 0)),
                  pl.BlockSpec((1, width, LANES), lambda b, i: (b, 0, 0))],
        out_specs=pl.BlockSpec((1, tq, MOBA_HEADS * LANES), lambda b, i: (b, i, 0)),
        out_shape=jax.ShapeDtypeStruct((B, S, MOBA_HEADS * LANES), BF16),
        compiler_params=_cparams("parallel", "parallel"),
    )(q3, km)


def _layer0(x2, B, S, w_in, pe_k, pe_v, cmp_k1, cmp_k2, cmp_v1, cmp_v2,
            q_norm, w_uq, kv_norm, w_ukv, w_out, ln_g, ln_b):
    T = B * S
    G = NSA_KV_HEADS
    w_perm = _permute_columns(w_in, _proj0_columns()).astype(BF16)
    tabs = _rope_tables(S)
    qd = MLA_NOPE_DIM + MLA_ROPE_DIM
    wuq = jnp.pad(w_uq.reshape(MLA_Q_RANK, MLA_HEADS, qd), ((0, 0), (0, 0), (0, LANES - qd)))
    wuq = wuq.reshape(MLA_Q_RANK, MLA_HEADS * LANES).astype(BF16)
    wukv = w_ukv.reshape(MLA_KV_RANK, MLA_HEADS, MLA_NOPE_DIM + MLA_V_DIM)
    wk = jnp.pad(wukv[:, :, :MLA_NOPE_DIM], ((0, 0), (0, 0), (0, LANES - MLA_NOPE_DIM)))
    wk = wk.reshape(MLA_KV_RANK, MLA_HEADS * LANES).astype(BF16)
    wv = jnp.pad(wukv[:, :, MLA_NOPE_DIM:], ((0, 0), (0, 0), (0, LANES - MLA_V_DIM)))
    wv = wv.reshape(MLA_KV_RANK, MLA_HEADS * LANES).astype(BF16)

    q, ks, vs, kw, vw, kc, vc, gate, qm, km, vm = _proj0(
        x2, w_perm, tabs, q_norm.reshape(1, -1), kv_norm.reshape(1, -1), wuq, wk, wv, S)

    n_chunks = S // NSA_CMP_STRIDE
    chunk_w = NSA_CMP_STRIDE * HEAD_DIM

    def chunks(t):
        t = t.reshape(B, S, G, HEAD_DIM).transpose(0, 2, 1, 3)
        return t.reshape(B * G, n_chunks, chunk_w)

    pe = jnp.stack([pe_k[:NSA_CMP_STRIDE].reshape(-1), pe_k[NSA_CMP_STRIDE:].reshape(-1),
                    pe_v[:NSA_CMP_STRIDE].reshape(-1), pe_v[NSA_CMP_STRIDE:].reshape(-1)])
    pad2 = lambda w: jnp.pad(w, ((0, 0), (0, LANES - HEAD_DIM))).astype(BF16)
    kcmp, vcmp = _compress(chunks(kc), chunks(vc), pe, cmp_k1.astype(BF16), pad2(cmp_k2),
                           cmp_v1.astype(BF16), pad2(cmp_v2), _cmp_rope_tables(n_chunks))

    n_sel = S // NSA_SEL_BLOCK
    n_cmp = (S - NSA_CMP_LEN) // NSA_CMP_STRIDE + 1
    tok = np.arange(n_chunks)[:, None] * NSA_CMP_STRIDE + np.arange(NSA_CMP_LEN)[None, :]
    c2s = (tok[:, :, None] // NSA_SEL_BLOCK == np.arange(LANES)[None, None, :]).sum(1) / NSA_CMP_LEN
    c2s[n_cmp:] = 0.0
    c2s = jnp.asarray(c2s, dtype=BF16)

    q3 = q.reshape(B, S, -1)
    gate3 = gate.reshape(B, S, -1)
    o_cmp, q_aug = _cmp_attn(q3, kcmp, vcmp, gate3, c2s, n_sel)

    r3 = lambda t: t.reshape(B, S, -1)
    nsa_heads = [(r, 0, 0) for r in range(NSA_GROUP)]
    o_sel = _flash(q_aug, r3(ks), r3(vs), gate3, n_steps=G, q_w=NSA_GROUP * LANES, k_w=LANES,
                   v_w=LANES, heads=nsa_heads, gate_cols=[NSA_GROUP + r for r in range(NSA_GROUP)])
    o_win = _flash(q_aug, r3(kw), r3(vw), gate3, n_steps=G, q_w=NSA_GROUP * LANES, k_w=LANES,
                   v_w=LANES, heads=nsa_heads, window=NSA_WINDOW,
                   gate_cols=[2 * NSA_GROUP + r for r in range(NSA_GROUP)])
    o_mla = _flash(r3(qm), r3(km), r3(vm), None, n_steps=MLA_HEADS // FLASH_HEADS,
                   q_w=FLASH_HEADS * LANES, k_w=FLASH_HEADS * LANES, v_w=FLASH_HEADS * LANES,
                   heads=_OWN_KV_HEADS, out_dtype=BF16)

    n_nsa = NSA_HEADS * HEAD_DIM
    w_out_b = w_out.astype(BF16)
    groups = [([o_cmp.reshape(T, -1), o_sel.reshape(T, -1), o_win.reshape(T, -1)], w_out_b[:n_nsa]),
              ([o_mla.reshape(T, -1)], w_out_b[n_nsa:])]
    return _outproj_ln(groups, x2, ln_g, ln_b)


def _layer1(x2, B, S, w_qkv, w_out, ln_g, ln_b):
    T = B * S
    width = MOBA_HEADS * HEAD_DIM
    tabs = _rope_tables(S)[0:2]
    q, k, v, kmean = _qkv1(x2, w_qkv.astype(BF16), tabs, S)
    n_blocks = S // MOBA_BLOCK
    n_top = min(MOBA_TOPK, max(n_blocks - 1, 1))
    km = kmean.reshape(B, n_blocks, width).transpose(0, 2, 1)
    km = jnp.pad(km, ((0, 0), (0, 0), (0, 8 - n_blocks)))
    km = jnp.tile(km, (1, 1, MOBA_HEADS))
    diag = (np.arange(width)[:, None] // HEAD_DIM) == (np.arange(LANES)[None, :] // 8)
    km = jnp.where(jnp.asarray(diag)[None], km, 0.0)
    q3 = q.reshape(B, S, width)
    q_aug = _moba_select(q3, km, n_top, n_blocks)
    o = _flash(q_aug, k.reshape(B, S, 2 * width), v.reshape(B, S, 2 * width), None,
               n_steps=MOBA_HEADS // FLASH_HEADS, q_w=FLASH_HEADS * LANES, k_w=FLASH_HEADS * LANES,
               v_w=FLASH_HEADS * LANES, heads=_OWN_KV_HEADS, out_dtype=BF16)
    return _outproj_ln([([o.reshape(T, width)], w_out.astype(BF16))], x2, ln_g, ln_b)


def kernel(x, l0_w_in, l0_nsa_pe_k, l0_nsa_pe_v, l0_nsa_cmp_k1, l0_nsa_cmp_k2, l0_nsa_cmp_v1, l0_nsa_cmp_v2, l0_mla_q_norm, l0_mla_w_uq, l0_mla_kv_norm, l0_mla_w_ukv, l0_w_out, l0_ln1_g, l0_ln1_b, l0_router, l0_router_bias, l0_exp_gate, l0_exp_up, l0_exp_down, l0_sh_gate, l0_sh_up, l0_sh_down, l0_ln2_g, l0_ln2_b, l1_w_qkv, l1_w_out, l1_ln1_g, l1_ln1_b, l1_router, l1_router_bias, l1_exp_gate, l1_exp_up, l1_exp_down, l1_sh_gate, l1_sh_up, l1_sh_down, l1_ln2_g, l1_ln2_b):
    B, S, D = x.shape
    x2 = x.reshape(B * S, D)
    x2 = _layer0(x2, B, S, l0_w_in, l0_nsa_pe_k, l0_nsa_pe_v, l0_nsa_cmp_k1, l0_nsa_cmp_k2,
                 l0_nsa_cmp_v1, l0_nsa_cmp_v2, l0_mla_q_norm, l0_mla_w_uq, l0_mla_kv_norm,
                 l0_mla_w_ukv, l0_w_out, l0_ln1_g, l0_ln1_b)
    x2 = _moe_block(x2, l0_router, l0_router_bias, l0_exp_gate, l0_exp_up, l0_exp_down,
                    l0_sh_gate, l0_sh_up, l0_sh_down, l0_ln2_g, l0_ln2_b)
    x2 = _layer1(x2, B, S, l1_w_qkv, l1_w_out, l1_ln1_g, l1_ln1_b)
    x2 = _moe_block(x2, l1_router, l1_router_bias, l1_exp_gate, l1_exp_up, l1_exp_down,
                    l1_sh_gate, l1_sh_up, l1_sh_down, l1_ln2_g, l1_ln2_b)
    return x2.reshape(B, S, D)
```

```python
import functools

import numpy as np
import jax
import jax.numpy as jnp
from jax import lax
from jax.experimental import pallas as pl
from jax.experimental.pallas import tpu as pltpu

F32 = jnp.float32
BF16 = jnp.bfloat16

LANES = 128
VMEM_LIMIT = 48 * 1024 * 1024

D_MODEL = 1024
DEPTH = 2
HEAD_DIM = 64
ROPE_THETA = 10000.0
LN_EPS = 1e-5
RMS_EPS = 1e-6

NSA_HEADS = 8
NSA_KV_HEADS = 2
NSA_GROUP = NSA_HEADS // NSA_KV_HEADS
NSA_CMP_LEN = 32
NSA_CMP_STRIDE = 16
NSA_CMP_HIDDEN = 128
NSA_SEL_BLOCK = 64
NSA_SEL_TOPN = 8
NSA_WINDOW = 512

MLA_HEADS = 8
MLA_Q_RANK = 256
MLA_KV_RANK = 128
MLA_NOPE_DIM = 64
MLA_ROPE_DIM = 32
MLA_V_DIM = 64

MOBA_HEADS = 16
MOBA_BLOCK = 256
MOBA_TOPK = 3

N_EXPERTS = 64
N_GROUPS = 8
GROUP_SIZE = N_EXPERTS // N_GROUPS
TOPK_GROUPS = 4
TOP_K = 8
EXPERT_FF = 256
ROUTED_SCALE = 2.5

DN_ALPHA = (2 * DEPTH) ** 0.25

ROW_TILE = 512
SELECT_TILE = 1024
ATTN_TQ = 256
MOE_TILE = 1024
MOE_EXPERTS_PER_STEP = 5
FLASH_HEADS = 4
_OWN_KV_HEADS = tuple((h, h, h) for h in range(FLASH_HEADS))

NEG_INF = float("-inf")


def _cparams(*sem):
    return pltpu.CompilerParams(dimension_semantics=sem, vmem_limit_bytes=VMEM_LIMIT)


def _dot(a, b):
    return jnp.dot(a.astype(BF16), b.astype(BF16), preferred_element_type=F32)


def _dot_nt(a, b):
    return lax.dot_general(a.astype(BF16), b.astype(BF16), (((1,), (1,)), ((), ())),
                           preferred_element_type=F32)


def _split(a):
    hi = a.astype(BF16)
    lo = (a - hi.astype(F32)).astype(BF16)
    return hi, lo


def _dot_split_lhs(a, b_bf16):
    hi, lo = _split(a)
    return (jnp.dot(hi, b_bf16, preferred_element_type=F32)
            + jnp.dot(lo, b_bf16, preferred_element_type=F32))


def _dot_split_both(a, b):
    ah, al = _split(a)
    bh, bl = _split(b)
    return (jnp.dot(ah, bh, preferred_element_type=F32)
            + jnp.dot(al, bh, preferred_element_type=F32)
            + jnp.dot(ah, bl, preferred_element_type=F32))


def _lane_iota(shape):
    return lax.broadcasted_iota(jnp.int32, shape, len(shape) - 1)


def _rope_lanes(x, cos, sin, first_half, half):
    n = x.shape[-1]
    rot = jnp.where(first_half, -pltpu.roll(x, n - half, 1), pltpu.roll(x, half, 1))
    return x * cos + rot * sin


def _layer_norm(z, g, b):
    mu = jnp.mean(z, axis=-1, keepdims=True)
    zc = z - mu
    var = jnp.mean(zc * zc, axis=-1, keepdims=True)
    return zc * lax.rsqrt(var + LN_EPS) * g + b


def _rms_norm(x, g):
    return x * lax.rsqrt(jnp.mean(x * x, axis=-1, keepdims=True) + RMS_EPS) * g


def _group_rank(x, group):
    n = x.shape[-1]
    pos = _lane_iota(x.shape) % group
    rank = jnp.zeros(x.shape, F32)
    for d in range(1, group):
        lower = pltpu.roll(x, d, 1)
        upper = pltpu.roll(x, n - d, 1)
        rank = rank + jnp.where((pos >= d) & (lower >= x), 1.0, 0.0)
        rank = rank + jnp.where((pos + d < group) & (upper > x), 1.0, 0.0)
    return rank


def _group_sum(x, group):
    n = x.shape[-1]
    pos = _lane_iota(x.shape) % group
    tot = x
    for d in range(1, group):
        tot = tot + jnp.where(pos >= d, pltpu.roll(x, d, 1), 0.0)
        tot = tot + jnp.where(pos + d < group, pltpu.roll(x, n - d, 1), 0.0)
    return tot


def _take_top(x, key, k):
    big = jnp.int32(1 << 30)
    taken = jnp.zeros(x.shape, jnp.bool_)
    for _ in range(k):
        m = jnp.max(x, axis=-1, keepdims=True)
        first = jnp.min(jnp.where(x == m, key, big), axis=-1, keepdims=True)
        hit = key == first
        taken = taken | hit
        x = jnp.where(hit, NEG_INF, x)
    return taken


def _rope_tables(S):
    pos = jnp.arange(S, dtype=F32)[:, None]

    def cs(half):
        inv = ROPE_THETA ** (-jnp.arange(half, dtype=F32) / half)
        ang = pos * inv[None, :]
        return jnp.cos(ang), jnp.sin(ang)

    c32, s32 = cs(HEAD_DIM // 2)
    c16, s16 = cs(MLA_ROPE_DIM // 2)
    one = jnp.ones((S, 1), F32)
    zero = jnp.zeros((S, 1), F32)
    cos_pair = jnp.concatenate([c32] * 4, -1)
    sin_pair = jnp.concatenate([s32] * 4, -1)
    cos_kv = jnp.concatenate([c32, c32, jnp.tile(one, (1, 64))], -1)
    sin_kv = jnp.concatenate([s32, s32, jnp.tile(zero, (1, 64))], -1)
    cos_mla = jnp.concatenate([jnp.tile(one, (1, 64)), c16, c16, jnp.tile(one, (1, 32))], -1)
    sin_mla = jnp.concatenate([jnp.tile(zero, (1, 64)), s16, s16, jnp.tile(zero, (1, 32))], -1)
    return jnp.stack([cos_pair, sin_pair, cos_kv, sin_kv, cos_mla, sin_mla])


def _cmp_rope_tables(n_chunks):
    pos = (jnp.arange(n_chunks, dtype=F32) * NSA_CMP_STRIDE + (NSA_CMP_LEN - 1))[:, None]
    half = HEAD_DIM // 2
    inv = ROPE_THETA ** (-jnp.arange(half, dtype=F32) / half)
    ang = pos * inv[None, :]
    c, s = jnp.cos(ang), jnp.sin(ang)
    one = jnp.ones((n_chunks, 64), F32)
    return jnp.stack([jnp.concatenate([c, c, one], -1), jnp.concatenate([s, s, 0 * one], -1)])


P0_Q = 0
P0_KVS = 512
P0_KVW = 768
P0_KC = 1024
P0_VC = 1152
P0_GATE = 1280
P0_CQ = 1536
P0_CKV = 1792
P0_KR = 1920
P0_W = 2048


def _proj0_columns():
    q_w = NSA_HEADS * HEAD_DIM
    kv0 = q_w
    piece = NSA_KV_HEADS * HEAD_DIM
    gate0 = kv0 + 6 * piece
    cq0 = gate0 + 3 * NSA_HEADS
    ckv0 = cq0 + MLA_Q_RANK
    kr0 = ckv0 + MLA_KV_RANK
    src = -np.ones((P0_W,), np.int64)
    src[P0_Q:P0_Q + q_w] = np.arange(q_w)
    d = np.arange(HEAD_DIM)
    for base, kp, vp in ((P0_KVS, 2, 3), (P0_KVW, 4, 5)):
        for g in range(NSA_KV_HEADS):
            src[base + g * 128 + d] = kv0 + kp * piece + g * HEAD_DIM + d
            src[base + g * 128 + 64 + d] = kv0 + vp * piece + g * HEAD_DIM + d
    src[P0_KC:P0_KC + piece] = kv0 + 0 * piece + np.arange(piece)
    src[P0_VC:P0_VC + piece] = kv0 + 1 * piece + np.arange(piece)
    for g in range(NSA_KV_HEADS):
        for br in range(3):
            for r in range(NSA_GROUP):
                src[P0_GATE + g * 128 + br * NSA_GROUP + r] = gate0 + br * NSA_HEADS + g * NSA_GROUP + r
    src[P0_CQ:P0_CQ + MLA_Q_RANK] = cq0 + np.arange(MLA_Q_RANK)
    src[P0_CKV:P0_CKV + MLA_KV_RANK] = ckv0 + np.arange(MLA_KV_RANK)
    src[P0_KR + 64:P0_KR + 64 + MLA_ROPE_DIM] = kr0 + np.arange(MLA_ROPE_DIM)
    return src


def _permute_columns(w, src):
    cols = jnp.take(w, jnp.asarray(np.maximum(src, 0)), axis=1)
    return jnp.where(jnp.asarray(src >= 0)[None, :], cols, 0.0)


def _block_onehot(pos, block, shape):
    return jnp.where(_lane_iota(shape) == HEAD_DIM + pos // block, 1.0, 0.0)


def _proj0_kernel(x_ref, w_ref, tab_ref, qg_ref, kvg_ref, wuq_ref, wk_ref, wv_ref,
                  q_ref, ks_ref, vs_ref, kw_ref, vw_ref, kc_ref, vc_ref, gate_ref,
                  qm_ref, km_ref, vm_ref, *, seq_tiles):
    tm = x_ref.shape[0]
    xb = x_ref[...].astype(BF16)
    lane = _lane_iota((tm, LANES))
    low = lane < HEAD_DIM
    pair_first = (lane % HEAD_DIM) < (HEAD_DIM // 2)
    mla_first = lane < (MLA_NOPE_DIM + MLA_ROPE_DIM // 2)
    cos_p, sin_p = tab_ref[0], tab_ref[1]
    cos_kv, sin_kv = tab_ref[2], tab_ref[3]
    cos_m, sin_m = tab_ref[4], tab_ref[5]
    pos = (pl.program_id(0) % seq_tiles) * tm + lax.broadcasted_iota(jnp.int32, (tm, 1), 0)
    sel_onehot = _block_onehot(pos, NSA_SEL_BLOCK, (tm, LANES))
    one_lane = jnp.where(lane == HEAD_DIM, 1.0, 0.0)

    def seg(c0, width):
        return jnp.dot(xb, w_ref[:, c0:c0 + width], preferred_element_type=F32)

    q = seg(P0_Q, 512)
    for j in range(4):
        blk = _rope_lanes(q[:, j * 128:(j + 1) * 128], cos_p, sin_p, pair_first, 32)
        q_ref[:, j * 128:(j + 1) * 128] = (blk * (HEAD_DIM ** -0.5)).astype(BF16)
    for c0, k_out, v_out, extra in ((P0_KVS, ks_ref, vs_ref, sel_onehot), (P0_KVW, kw_ref, vw_ref, 0.0)):
        kv = seg(c0, 256)
        for j in range(2):
            blk = _rope_lanes(kv[:, j * 128:(j + 1) * 128], cos_kv, sin_kv, pair_first, 32)
            k_out[:, j * 128:(j + 1) * 128] = jnp.where(low, blk, extra).astype(BF16)
            v_out[:, j * 128:(j + 1) * 128] = jnp.where(low, pltpu.roll(blk, HEAD_DIM, 1), one_lane).astype(BF16)
    kc_ref[...] = seg(P0_KC, 128)
    vc_ref[...] = seg(P0_VC, 128)
    gate_ref[...] = seg(P0_GATE, 256)

    cq = _rms_norm(seg(P0_CQ, 256), qg_ref[...])
    qm = _dot(cq, wuq_ref[...])
    mla_scale = (MLA_NOPE_DIM + MLA_ROPE_DIM) ** -0.5
    for h in range(MLA_HEADS):
        blk = _rope_lanes(qm[:, h * 128:(h + 1) * 128], cos_m, sin_m, mla_first, 16)
        qm_ref[:, h * 128:(h + 1) * 128] = (blk * mla_scale).astype(BF16)
    ckv = _rms_norm(seg(P0_CKV, 128), kvg_ref[...]).astype(BF16)
    kn = jnp.dot(ckv, wk_ref[...], preferred_element_type=F32)
    kpe = _rope_lanes(seg(P0_KR, 128), cos_m, sin_m, mla_first, 16)
    for h in range(MLA_HEADS):
        km_ref[:, h * 128:(h + 1) * 128] = (kn[:, h * 128:(h + 1) * 128] + kpe).astype(BF16)
    vm = jnp.dot(ckv, wv_ref[...], preferred_element_type=F32)
    for h in range(MLA_HEADS):
        vm_ref[:, h * 128:(h + 1) * 128] = jnp.where(low, vm[:, h * 128:(h + 1) * 128], one_lane).astype(BF16)


def _proj0(x2, w_perm, tabs, q_norm, kv_norm, wuq, wk, wv, S):
    T = x2.shape[0]
    tm = ROW_TILE
    ns = S // tm
    row = lambda w: pl.BlockSpec((tm, w), lambda i: (i, 0))
    full = lambda a: pl.BlockSpec(a.shape, lambda i: (0,) * a.ndim)
    widths = (512, 256, 256, 256, 256, 128, 128, 256, 1024, 1024, 1024)
    dtypes = (BF16, BF16, BF16, BF16, BF16, F32, F32, F32, BF16, BF16, BF16)
    return pl.pallas_call(
        functools.partial(_proj0_kernel, seq_tiles=ns),
        grid=(T // tm,),
        in_specs=[row(D_MODEL), full(w_perm),
                  pl.BlockSpec((6, tm, LANES), lambda i: (0, i % ns, 0)),
                  full(q_norm), full(kv_norm), full(wuq), full(wk), full(wv)],
        out_specs=[row(w) for w in widths],
        out_shape=[jax.ShapeDtypeStruct((T, w), d) for w, d in zip(widths, dtypes)],
        compiler_params=_cparams("parallel"),
    )(x2, w_perm, tabs, q_norm, kv_norm, wuq, wk, wv)


def _compress_kernel(kc_ref, vc_ref, pe_ref, w1k_ref, w2k_ref, w1v_ref, w2v_ref, tab_ref,
                     ko_ref, vo_ref):
    half = NSA_CMP_STRIDE * HEAD_DIM
    n = kc_ref.shape[1]

    def mlp(x, pe_lo, pe_hi, w1_ref, w2_ref):
        first = _dot(x + pe_lo, w1_ref[0:half, :])
        second = _dot(x + pe_hi, w1_ref[half:2 * half, :])
        hidden = first + pltpu.roll(second, n - 1, 0)
        return _dot(jax.nn.gelu(hidden), w2_ref[...])

    k = mlp(kc_ref[0], pe_ref[0:1, :], pe_ref[1:2, :], w1k_ref, w2k_ref)
    lane = _lane_iota(k.shape)
    ko_ref[0] = _rope_lanes(k, tab_ref[0], tab_ref[1], lane < HEAD_DIM // 2, 32).astype(BF16)
    vo_ref[0] = mlp(vc_ref[0], pe_ref[2:3, :], pe_ref[3:4, :], w1v_ref, w2v_ref).astype(BF16)


def _compress(kc_chunks, vc_chunks, pe, w1k, w2k, w1v, w2v, ctab):
    n_bg, n, width = kc_chunks.shape
    blk = pl.BlockSpec((1, n, width), lambda i: (i, 0, 0))
    full = lambda a: pl.BlockSpec(a.shape, lambda i: (0,) * a.ndim)
    out = pl.BlockSpec((1, n, LANES), lambda i: (i, 0, 0))
    return pl.pallas_call(
        _compress_kernel,
        grid=(n_bg,),
        in_specs=[blk, blk, full(pe), full(w1k), full(w2k), full(w1v), full(w2v), full(ctab)],
        out_specs=[out, out],
        out_shape=[jax.ShapeDtypeStruct((n_bg, n, LANES), BF16)] * 2,
        compiler_params=_cparams("parallel"),
    )(kc_chunks, vc_chunks, pe, w1k, w2k, w1v, w2v, ctab)


def _cmp_attn_kernel(q_ref, k_ref, v_ref, gate_ref, c2s_ref, o_ref, qa_ref, *, n_sel, n_top):
    qi = pl.program_id(2)
    tq = q_ref.shape[1]
    n = k_ref.shape[1]
    q = q_ref[0]
    k = k_ref[0][:, 0:HEAD_DIM]
    v = v_ref[0][:, 0:HEAD_DIM]
    gates = jax.nn.sigmoid(gate_ref[0])
    pos = qi * tq + lax.broadcasted_iota(jnp.int32, (tq, 1), 0)
    cmp_end = lax.broadcasted_iota(jnp.int32, (1, n), 1) * NSA_CMP_STRIDE + (NSA_CMP_LEN - 1)
    visible = cmp_end <= pos
    p_sum = jnp.zeros((tq, n), F32)
    outs = []
    for r in range(NSA_GROUP):
        s = jnp.where(visible, _dot_nt(q[:, r * HEAD_DIM:(r + 1) * HEAD_DIM], k), NEG_INF)
        m = jnp.max(s, axis=-1, keepdims=True)
        e = jnp.exp(s - jnp.where(m > NEG_INF, m, 0.0))
        den = jnp.sum(e, axis=-1, keepdims=True)
        p = e / jnp.where(den > 0, den, 1.0)
        p_sum = p_sum + p
        outs.append(_dot(p, v) * gates[:, r:r + 1])
    o_ref[0] = jnp.concatenate(outs, axis=-1)

    imp = _dot_split_lhs(p_sum, c2s_ref[...])
    blk = _lane_iota(imp.shape)
    cur = pos // NSA_SEL_BLOCK
    forced = (blk == 0) | (blk == cur) | (blk == cur - 1)
    valid = (blk <= cur) & (blk < n_sel)
    free = jnp.where(valid & ~forced, imp, NEG_INF)
    taken = forced | _take_top(free, blk, n_top - 3)
    bias = jnp.where(blk < n_sel, jnp.where(taken & valid, 0.0, -1e30), 0.0)
    bias = pltpu.roll(bias, HEAD_DIM, 1)
    low = blk < HEAD_DIM
    qf = q.astype(F32)
    for r in range(NSA_GROUP):
        x = qf[:, (r // 2) * LANES:(r // 2 + 1) * LANES]
        if r % 2 == 1:
            x = pltpu.roll(x, HEAD_DIM, 1)
        qa_ref[0, :, r * LANES:(r + 1) * LANES] = jnp.where(low, x, bias).astype(BF16)


def _cmp_attn(q3, kcmp, vcmp, gate3, c2s, n_sel):
    B, S, _ = q3.shape
    n = kcmp.shape[1]
    tq = SELECT_TILE
    G = NSA_KV_HEADS
    assert n_sel >= 3 and NSA_SEL_TOPN >= 3
    kern = functools.partial(_cmp_attn_kernel, n_sel=n_sel, n_top=min(NSA_SEL_TOPN, n_sel))
    return pl.pallas_call(
        kern,
        grid=(B, G, S // tq),
        in_specs=[pl.BlockSpec((1, tq, 256), lambda b, g, i: (b, i, g)),
                  pl.BlockSpec((1, n, LANES), lambda b, g, i: (b * G + g, 0, 0)),
                  pl.BlockSpec((1, n, LANES), lambda b, g, i: (b * G + g, 0, 0)),
                  pl.BlockSpec((1, tq, LANES), lambda b, g, i: (b, i, g)),
                  pl.BlockSpec(c2s.shape, lambda b, g, i: (0, 0))],
        out_specs=[pl.BlockSpec((1, tq, 256), lambda b, g, i: (b, i, g)),
                   pl.BlockSpec((1, tq, 512), lambda b, g, i: (b, i, g))],
        out_shape=[jax.ShapeDtypeStruct((B, S, 512), F32),
                   jax.ShapeDtypeStruct((B, S, NSA_HEADS * LANES), BF16)],
        compiler_params=_cparams("parallel", "parallel", "parallel"),
    )(q3, kcmp, vcmp, gate3, c2s)


def _flash_kernel(*refs, heads, window, gate_cols):
    it = iter(refs)
    q_ref, k_ref, v_ref = next(it), next(it), next(it)
    gate_ref = next(it) if gate_cols is not None else None
    o_ref = next(it)
    s_sc, mx_sc, mb_sc, acc_sc = next(it), next(it), next(it), next(it)

    qi = pl.program_id(2)
    tq = q_ref.shape[1]
    tk = tq
    nh = len(heads)

    lane = _lane_iota((tq, LANES))
    low = lane < HEAD_DIM
    q_heads = [q_ref[0, :, g * LANES:(g + 1) * LANES] for (g, _, _) in heads]
    mx_sc[...] = jnp.full(mx_sc.shape, NEG_INF, F32)
    acc_sc[...] = jnp.zeros(acc_sc.shape, F32)

    def score_tile(j, pos_mask):
        start = pl.multiple_of(j * tk, tk)
        k = k_ref[0, pl.ds(start, tk), :]
        for h, (_, kg, _) in enumerate(heads):
            s = _dot_nt(q_heads[h], k[:, kg * LANES:(kg + 1) * LANES])
            if pos_mask is not None:
                s = jnp.where(pos_mask, s, NEG_INF)
            s_sc[h, j] = s
            best = s[:, 0:LANES]
            for c in range(1, tk // LANES):
                best = jnp.maximum(best, s[:, c * LANES:(c + 1) * LANES])
            mx_sc[h] = jnp.maximum(mx_sc[h], best)

    def value_tile(j):
        start = pl.multiple_of(j * tk, tk)
        v = v_ref[0, pl.ds(start, tk), :]
        for h, (_, _, vg) in enumerate(heads):
            mb = mb_sc[h]
            s = s_sc[h, j]
            p = jnp.concatenate([jnp.exp(s[:, c * LANES:(c + 1) * LANES] - mb)
                                 for c in range(tk // LANES)], axis=-1).astype(BF16)
            acc_sc[h] += jnp.dot(p, v[:, vg * LANES:(vg + 1) * LANES], preferred_element_type=F32)

    def for_each_tile(fn_full, fn_masked):
        row = lax.broadcasted_iota(jnp.int32, (tq, tk), 0)
        col = lax.broadcasted_iota(jnp.int32, (tq, tk), 1)
        first_full = 0
        if window is not None:
            back = window // tk
            first_full = jnp.maximum(qi - back + 1, 0)

            @pl.when(qi >= back)
            def _():
                fn_masked(qi - back, col > row)

        def body(j, carry):
            fn_full(j)
            return carry

        lax.fori_loop(first_full, qi, body, 0)
        fn_masked(qi, col <= row)

    for_each_tile(lambda j: score_tile(j, None), score_tile)
    for h in range(nh):
        m = jnp.max(mx_sc[h], axis=-1, keepdims=True)
        mb_sc[h] = jnp.broadcast_to(m, (tq, LANES))
    for_each_tile(value_tile, lambda j, mask: value_tile(j))

    if gate_ref is not None:
        gates = jax.nn.sigmoid(gate_ref[0])
    results = []
    for h in range(nh):
        acc = acc_sc[h]
        o = acc / acc[:, HEAD_DIM:HEAD_DIM + 1]
        if gate_ref is not None:
            c = gate_cols[h]
            o = o * gates[:, c:c + 1]
        results.append(o)
    for pair in range(nh // 2):
        high = pltpu.roll(results[2 * pair + 1], HEAD_DIM, 1)
        o_ref[0, :, pair * LANES:(pair + 1) * LANES] = jnp.where(low, results[2 * pair], high).astype(o_ref.dtype)


def _flash(q, k, v, gate, *, n_steps, q_w, k_w, v_w, heads, window=None, gate_cols=None,
           out_dtype=F32):
    B, S, _ = q.shape
    tq = ATTN_TQ
    nh = len(heads)
    assert nh % 2 == 0 and S % tq == 0 and (window is None or window % tq == 0)
    n_stash = S // tq
    in_specs = [pl.BlockSpec((1, tq, q_w), lambda b, h, i: (b, i, h)),
                pl.BlockSpec((1, S, k_w), lambda b, h, i: (b, 0, h)),
                pl.BlockSpec((1, S, v_w), lambda b, h, i: (b, 0, h))]
    args = [q, k, v]
    if gate is not None:
        in_specs.append(pl.BlockSpec((1, tq, LANES), lambda b, h, i: (b, i, h)))
        args.append(gate)
    kern = functools.partial(_flash_kernel, heads=tuple(heads), window=window, gate_cols=gate_cols)
    return pl.pallas_call(
        kern,
        grid=(B, n_steps, S // tq),
        in_specs=in_specs,
        out_specs=pl.BlockSpec((1, tq, nh * HEAD_DIM), lambda b, h, i: (b, i, h)),
        out_shape=jax.ShapeDtypeStruct((B, S, n_steps * nh * HEAD_DIM), out_dtype),
        scratch_shapes=[pltpu.VMEM((nh, n_stash, tq, tq), F32), pltpu.VMEM((nh, tq, LANES), F32),
                        pltpu.VMEM((nh, tq, LANES), F32), pltpu.VMEM((nh, tq, LANES), F32)],
        compiler_params=_cparams("parallel", "parallel", "arbitrary"),
    )(*args)


def _outproj_kernel(*refs, group_sizes):
    it = iter(refs)
    y = None
    for n_in in group_sizes:
        acts = [next(it)[...].astype(F32) for _ in range(n_in)]
        w_ref = next(it)
        a = acts[0]
        for extra in acts[1:]:
            a = a + extra
        part = _dot(a, w_ref[...])
        y = part if y is None else y + part
    x_ref, g_ref, b_ref, o_ref = next(it), next(it), next(it), next(it)
    o_ref[...] = _layer_norm(DN_ALPHA * x_ref[...] + y, g_ref[...], b_ref[...])


def _outproj_ln(groups, x2, g, b):
    T = x2.shape[0]
    tm = ROW_TILE
    in_specs, args, sizes = [], [], []
    for acts, w in groups:
        for a in acts:
            in_specs.append(pl.BlockSpec((tm, a.shape[1]), lambda i: (i, 0)))
            args.append(a)
        in_specs.append(pl.BlockSpec(w.shape, lambda i: (0, 0)))
        args.append(w)
        sizes.append(len(acts))
    in_specs += [pl.BlockSpec((tm, D_MODEL), lambda i: (i, 0)),
                 pl.BlockSpec((1, D_MODEL), lambda i: (0, 0)),
                 pl.BlockSpec((1, D_MODEL), lambda i: (0, 0))]
    args += [x2, g.reshape(1, -1), b.reshape(1, -1)]
    return pl.pallas_call(
        functools.partial(_outproj_kernel, group_sizes=tuple(sizes)),
        grid=(T // tm,),
        in_specs=in_specs,
        out_specs=pl.BlockSpec((tm, D_MODEL), lambda i: (i, 0)),
        out_shape=jax.ShapeDtypeStruct((T, D_MODEL), F32),
        compiler_params=_cparams("parallel"),
    )(*args)


def _router_kernel(x_ref, w_ref, bias_ref, g_ref):
    logits = _dot_split_both(x_ref[...], w_ref[...])
    lane = _lane_iota(logits.shape)
    real = lane < N_EXPERTS
    scores = jax.nn.sigmoid(logits)
    choice = jnp.where(real, scores + bias_ref[...], NEG_INF)
    top2 = jnp.where(_group_rank(choice, GROUP_SIZE) < 2, choice, 0.0)
    grp_score = jnp.where(real, _group_sum(top2, GROUP_SIZE), NEG_INF)
    grp_taken = _take_top(grp_score, lane // GROUP_SIZE, TOPK_GROUPS)
    masked = jnp.where(grp_taken & real, choice, NEG_INF)
    taken = _take_top(masked, lane, TOP_K)
    w = jnp.where(taken, scores, 0.0)
    w = w / jnp.sum(w, axis=-1, keepdims=True) * ROUTED_SCALE
    g_ref[...] = jnp.where(lane == N_EXPERTS, 1.0, w)


def _router(x2, w_router_pad, bias_pad):
    T = x2.shape[0]
    tm = SELECT_TILE
    return pl.pallas_call(
        _router_kernel,
        grid=(T // tm,),
        in_specs=[pl.BlockSpec((tm, D_MODEL), lambda i: (i, 0)),
                  pl.BlockSpec(w_router_pad.shape, lambda i: (0, 0)),
                  pl.BlockSpec((1, LANES), lambda i: (0, 0))],
        out_specs=pl.BlockSpec((tm, LANES), lambda i: (i, 0)),
        out_shape=jax.ShapeDtypeStruct((T, LANES), F32),
        compiler_params=_cparams("parallel"),
    )(x2, w_router_pad, bias_pad)


def _moe_kernel(x_ref, gates_ref, wg_ref, wu_ref, wd_ref, g_ref, b_ref, o_ref, xb_sc, acc_sc):
    step = pl.program_id(1)
    last = pl.num_programs(1) - 1
    per = wg_ref.shape[0]

    @pl.when(step == 0)
    def _():
        xb_sc[...] = x_ref[...].astype(BF16)
        acc_sc[...] = jnp.zeros(acc_sc.shape, F32)

    xb = xb_sc[...]
    gates = gates_ref[...]
    lane = _lane_iota(gates.shape)
    hidden = []
    for j in range(per):
        hg = jnp.dot(xb, wg_ref[j], preferred_element_type=F32)
        hu = jnp.dot(xb, wu_ref[j], preferred_element_type=F32)
        col = jnp.sum(jnp.where(lane == step * per + j, gates, 0.0), axis=-1, keepdims=True)
        hidden.append((jax.nn.silu(hg) * hu * col).astype(BF16))
    h = jnp.concatenate(hidden, axis=-1)
    wd = wd_ref[...].reshape(per * EXPERT_FF, D_MODEL)
    acc_sc[...] += jnp.dot(h, wd, preferred_element_type=F32)

    @pl.when(step == last)
    def _():
        o_ref[...] = _layer_norm(DN_ALPHA * x_ref[...] + acc_sc[...], g_ref[...], b_ref[...])


def _moe_ln(x2, gates, wg, wu, wd, g, b):
    T = x2.shape[0]
    tm = MOE_TILE
    per = MOE_EXPERTS_PER_STEP
    n_steps = wg.shape[0] // per
    assert n_steps * per == wg.shape[0]
    return pl.pallas_call(
        _moe_kernel,
        grid=(T // tm, n_steps),
        in_specs=[pl.BlockSpec((tm, D_MODEL), lambda i, e: (i, 0)),
                  pl.BlockSpec((tm, LANES), lambda i, e: (i, 0)),
                  pl.BlockSpec((per, D_MODEL, EXPERT_FF), lambda i, e: (e, 0, 0)),
                  pl.BlockSpec((per, D_MODEL, EXPERT_FF), lambda i, e: (e, 0, 0)),
                  pl.BlockSpec((per, EXPERT_FF, D_MODEL), lambda i, e: (e, 0, 0)),
                  pl.BlockSpec((1, D_MODEL), lambda i, e: (0, 0)),
                  pl.BlockSpec((1, D_MODEL), lambda i, e: (0, 0))],
        out_specs=pl.BlockSpec((tm, D_MODEL), lambda i, e: (i, 0)),
        out_shape=jax.ShapeDtypeStruct((T, D_MODEL), F32),
        scratch_shapes=[pltpu.VMEM((tm, D_MODEL), BF16), pltpu.VMEM((tm, D_MODEL), F32)],
        compiler_params=_cparams("parallel", "arbitrary"),
    )(x2, gates, wg, wu, wd, g.reshape(1, -1), b.reshape(1, -1))


def _moe_block(x2, router, router_bias, exp_gate, exp_up, exp_down, sh_gate, sh_up, sh_down, g, b):
    w_router_pad = jnp.pad(router, ((0, 0), (0, LANES - N_EXPERTS)))
    bias_pad = jnp.pad(router_bias, (0, LANES - N_EXPERTS)).reshape(1, LANES)
    gates = _router(x2, w_router_pad, bias_pad)
    wg = jnp.concatenate([exp_gate, sh_gate[None]], 0).astype(BF16)
    wu = jnp.concatenate([exp_up, sh_up[None]], 0).astype(BF16)
    wd = jnp.concatenate([exp_down, sh_down[None]], 0).astype(BF16)
    return _moe_ln(x2, gates, wg, wu, wd, g, b)


def _qkv1_kernel(x_ref, w_ref, tab_ref, q_ref, k_ref, v_ref, kmean_ref, *, seq_tiles):
    tm = x_ref.shape[0]
    xb = x_ref[...].astype(BF16)
    width = MOBA_HEADS * HEAD_DIM
    lane = _lane_iota((tm, LANES))
    low = lane < HEAD_DIM
    pair_first = (lane % HEAD_DIM) < (HEAD_DIM // 2)
    cos_p, sin_p = tab_ref[0], tab_ref[1]
    pos = (pl.program_id(0) % seq_tiles) * tm + lax.broadcasted_iota(jnp.int32, (tm, 1), 0)
    onehot = _block_onehot(pos, MOBA_BLOCK, (tm, LANES))
    one_lane = jnp.where(lane == HEAD_DIM, 1.0, 0.0)
    wide = 2 * LANES
    for j in range(width // wide):
        q4 = jnp.dot(xb, w_ref[:, j * wide:(j + 1) * wide], preferred_element_type=F32)
        k4 = jnp.dot(xb, w_ref[:, width + j * wide:width + (j + 1) * wide], preferred_element_type=F32)
        v4 = jnp.dot(xb, w_ref[:, 2 * width + j * wide:2 * width + (j + 1) * wide],
                     preferred_element_type=F32)
        for half in range(2):
            c = j * wide + half * LANES
            part = slice(half * LANES, (half + 1) * LANES)
            q = _rope_lanes(q4[:, part], cos_p, sin_p, pair_first, 32)
            q_ref[:, c:c + LANES] = (q * (HEAD_DIM ** -0.5)).astype(BF16)
            k = _rope_lanes(k4[:, part], cos_p, sin_p, pair_first, 32)
            k_ref[:, 2 * c:2 * c + LANES] = jnp.where(low, k, onehot).astype(BF16)
            k_ref[:, 2 * c + LANES:2 * c + 2 * LANES] = jnp.where(
                low, pltpu.roll(k, HEAD_DIM, 1), onehot).astype(BF16)
            for blk in range(tm // MOBA_BLOCK):
                kmean_ref[blk, :, c:c + LANES] = jnp.mean(
                    k[blk * MOBA_BLOCK:(blk + 1) * MOBA_BLOCK], axis=0, keepdims=True)
            v = v4[:, part]
            v_ref[:, 2 * c:2 * c + LANES] = jnp.where(low, v, one_lane).astype(BF16)
            v_ref[:, 2 * c + LANES:2 * c + 2 * LANES] = jnp.where(
                low, pltpu.roll(v, HEAD_DIM, 1), one_lane).astype(BF16)


def _qkv1(x2, w_qkv, tabs, S):
    T = x2.shape[0]
    tm = ROW_TILE
    assert tm % MOBA_BLOCK == 0
    ns = S // tm
    width = MOBA_HEADS * HEAD_DIM
    row = lambda w: pl.BlockSpec((tm, w), lambda i: (i, 0))
    return pl.pallas_call(
        functools.partial(_qkv1_kernel, seq_tiles=ns),
        grid=(T // tm,),
        in_specs=[pl.BlockSpec((tm, D_MODEL), lambda i: (i, 0)),
                  pl.BlockSpec(w_qkv.shape, lambda i: (0, 0)),
                  pl.BlockSpec((2, tm, LANES), lambda i: (0, i % ns, 0))],
        out_specs=[row(width), row(2 * width), row(2 * width),
                   pl.BlockSpec((tm // MOBA_BLOCK, 1, width), lambda i: (i, 0, 0))],
        out_shape=[jax.ShapeDtypeStruct((T, width), BF16), jax.ShapeDtypeStruct((T, 2 * width), BF16),
                   jax.ShapeDtypeStruct((T, 2 * width), BF16),
                   jax.ShapeDtypeStruct((T // MOBA_BLOCK, 1, width), F32)],
        compiler_params=_cparams("parallel"),
    )(x2, w_qkv, tabs)


def _moba_select_kernel(q_ref, km_ref, qa_ref, *, n_top, n_blocks):
    i = pl.program_id(1)
    tq = q_ref.shape[1]
    gate = _dot_split_lhs_rhs(q_ref[0], km_ref[0])
    lane = _lane_iota(gate.shape)
    blk = lane % 8
    own = (i * tq + lax.broadcasted_iota(jnp.int32, (tq, 1), 0)) // MOBA_BLOCK
    score = jnp.where((blk < own) & (blk < n_blocks), gate, NEG_INF)
    rank = _group_rank(score, 8)
    chosen = ((score > NEG_INF) & (rank < n_top)) | (blk == own)
    bias = jnp.where(chosen, 0.0, -1e30)
    low = lane < HEAD_DIM
    in_bias = (lane >= HEAD_DIM) & (lane < HEAD_DIM + 8)
    for pair in range(MOBA_HEADS // 2):
        qf = q_ref[0, :, pair * LANES:(pair + 1) * LANES].astype(F32)
        for half in range(2):
            h = 2 * pair + half
            x = qf if half == 0 else pltpu.roll(qf, HEAD_DIM, 1)
            b = jnp.where(in_bias, pltpu.roll(bias, (HEAD_DIM - 8 * h) % LANES, 1), 0.0)
            qa_ref[0, :, h * LANES:(h + 1) * LANES] = jnp.where(low, x, b).astype(BF16)


def _dot_split_lhs_rhs(q_bf16, b):
    hi, lo = _split(b)
    return (jnp.dot(q_bf16, hi, preferred_element_type=F32)
            + jnp.dot(q_bf16, lo, preferred_element_type=F32))


def _moba_select(q3, km, n_top, n_blocks):
    B, S, width = q3.shape
    tq = SELECT_TILE
    kern = functools.partial(_moba_select_kernel, n_top=n_top, n_blocks=n_blocks)
    return pl.pallas_call(
        kern,
        grid=(B, S // tq),
        in_specs=[pl.BlockSpec((1, tq, width), lambda b, i: (b, i, 0)),
                  pl.BlockSpec((1, width, LANES), lambda b, i: (b, 0, 0))],
        out_specs=pl.BlockSpec((1, tq, MOBA_HEADS * LANES), lambda b, i: (b, i, 0)),
        out_shape=jax.ShapeDtypeStruct((B, S, MOBA_HEADS * LANES), BF16),
        compiler_params=_cparams("parallel", "parallel"),
    )(q3, km)


def _layer0(x2, B, S, w_in, pe_k, pe_v, cmp_k1, cmp_k2, cmp_v1, cmp_v2,
            q_norm, w_uq, kv_norm, w_ukv, w_out, ln_g, ln_b):
    T = B * S
    G = NSA_KV_HEADS
    w_perm = _permute_columns(w_in, _proj0_columns()).astype(BF16)
    tabs = _rope_tables(S)
    qd = MLA_NOPE_DIM + MLA_ROPE_DIM
    wuq = jnp.pad(w_uq.reshape(MLA_Q_RANK, MLA_HEADS, qd), ((0, 0), (0, 0), (0, LANES - qd)))
    wuq = wuq.reshape(MLA_Q_RANK, MLA_HEADS * LANES).astype(BF16)
    wukv = w_ukv.reshape(MLA_KV_RANK, MLA_HEADS, MLA_NOPE_DIM + MLA_V_DIM)
    wk = jnp.pad(wukv[:, :, :MLA_NOPE_DIM], ((0, 0), (0, 0), (0, LANES - MLA_NOPE_DIM)))
    wk = wk.reshape(MLA_KV_RANK, MLA_HEADS * LANES).astype(BF16)
    wv = jnp.pad(wukv[:, :, MLA_NOPE_DIM:], ((0, 0), (0, 0), (0, LANES - MLA_V_DIM)))
    wv = wv.reshape(MLA_KV_RANK, MLA_HEADS * LANES).astype(BF16)

    q, ks, vs, kw, vw, kc, vc, gate, qm, km, vm = _proj0(
        x2, w_perm, tabs, q_norm.reshape(1, -1), kv_norm.reshape(1, -1), wuq, wk, wv, S)

    n_chunks = S // NSA_CMP_STRIDE
    chunk_w = NSA_CMP_STRIDE * HEAD_DIM

    def chunks(t):
        t = t.reshape(B, S, G, HEAD_DIM).transpose(0, 2, 1, 3)
        return t.reshape(B * G, n_chunks, chunk_w)

    pe = jnp.stack([pe_k[:NSA_CMP_STRIDE].reshape(-1), pe_k[NSA_CMP_STRIDE:].reshape(-1),
                    pe_v[:NSA_CMP_STRIDE].reshape(-1), pe_v[NSA_CMP_STRIDE:].reshape(-1)])
    pad2 = lambda w: jnp.pad(w, ((0, 0), (0, LANES - HEAD_DIM))).astype(BF16)
    kcmp, vcmp = _compress(chunks(kc), chunks(vc), pe, cmp_k1.astype(BF16), pad2(cmp_k2),
                           cmp_v1.astype(BF16), pad2(cmp_v2), _cmp_rope_tables(n_chunks))

    n_sel = S // NSA_SEL_BLOCK
    n_cmp = (S - NSA_CMP_LEN) // NSA_CMP_STRIDE + 1
    tok = np.arange(n_chunks)[:, None] * NSA_CMP_STRIDE + np.arange(NSA_CMP_LEN)[None, :]
    c2s = (tok[:, :, None] // NSA_SEL_BLOCK == np.arange(LANES)[None, None, :]).sum(1) / NSA_CMP_LEN
    c2s[n_cmp:] = 0.0
    c2s = jnp.asarray(c2s, dtype=BF16)

    q3 = q.reshape(B, S, -1)
    gate3 = gate.reshape(B, S, -1)
    o_cmp, q_aug = _cmp_attn(q3, kcmp, vcmp, gate3, c2s, n_sel)

    r3 = lambda t: t.reshape(B, S, -1)
    nsa_heads = [(r, 0, 0) for r in range(NSA_GROUP)]
    o_sel = _flash(q_aug, r3(ks), r3(vs), gate3, n_steps=G, q_w=NSA_GROUP * LANES, k_w=LANES,
                   v_w=LANES, heads=nsa_heads, gate_cols=[NSA_GROUP + r for r in range(NSA_GROUP)])
    o_win = _flash(q_aug, r3(kw), r3(vw), gate3, n_steps=G, q_w=NSA_GROUP * LANES, k_w=LANES,
                   v_w=LANES, heads=nsa_heads, window=NSA_WINDOW,
                   gate_cols=[2 * NSA_GROUP + r for r in range(NSA_GROUP)])
    o_mla = _flash(r3(qm), r3(km), r3(vm), None, n_steps=MLA_HEADS // FLASH_HEADS,
                   q_w=FLASH_HEADS * LANES, k_w=FLASH_HEADS * LANES, v_w=FLASH_HEADS * LANES,
                   heads=_OWN_KV_HEADS, out_dtype=BF16)

    n_nsa = NSA_HEADS * HEAD_DIM
    w_out_b = w_out.astype(BF16)
    groups = [([o_cmp.reshape(T, -1), o_sel.reshape(T, -1), o_win.reshape(T, -1)], w_out_b[:n_nsa]),
              ([o_mla.reshape(T, -1)], w_out_b[n_nsa:])]
    return _outproj_ln(groups, x2, ln_g, ln_b)


def _layer1(x2, B, S, w_qkv, w_out, ln_g, ln_b):
    T = B * S
    width = MOBA_HEADS * HEAD_DIM
    tabs = _rope_tables(S)[0:2]
    q, k, v, kmean = _qkv1(x2, w_qkv.astype(BF16), tabs, S)
    n_blocks = S // MOBA_BLOCK
    n_top = min(MOBA_TOPK, max(n_blocks - 1, 1))
    km = kmean.reshape(B, n_blocks, width).transpose(0, 2, 1)
    km = jnp.pad(km, ((0, 0), (0, 0), (0, 8 - n_blocks)))
    km = jnp.tile(km, (1, 1, MOBA_HEADS))
    diag = (np.arange(width)[:, None] // HEAD_DIM) == (np.arange(LANES)[None, :] // 8)
    km = jnp.where(jnp.asarray(diag)[None], km, 0.0)
    q3 = q.reshape(B, S, width)
    q_aug = _moba_select(q3, km, n_top, n_blocks)
    o = _flash(q_aug, k.reshape(B, S, 2 * width), v.reshape(B, S, 2 * width), None,
               n_steps=MOBA_HEADS // FLASH_HEADS, q_w=FLASH_HEADS * LANES, k_w=FLASH_HEADS * LANES,
               v_w=FLASH_HEADS * LANES, heads=_OWN_KV_HEADS, out_dtype=BF16)
    return _outproj_ln([([o.reshape(T, width)], w_out.astype(BF16))], x2, ln_g, ln_b)


def kernel(x, l0_w_in, l0_nsa_pe_k, l0_nsa_pe_v, l0_nsa_cmp_k1, l0_nsa_cmp_k2, l0_nsa_cmp_v1, l0_nsa_cmp_v2, l0_mla_q_norm, l0_mla_w_uq, l0_mla_kv_norm, l0_mla_w_ukv, l0_w_out, l0_ln1_g, l0_ln1_b, l0_router, l0_router_bias, l0_exp_gate, l0_exp_up, l0_exp_down, l0_sh_gate, l0_sh_up, l0_sh_down, l0_ln2_g, l0_ln2_b, l1_w_qkv, l1_w_out, l1_ln1_g, l1_ln1_b, l1_router, l1_router_bias, l1_exp_gate, l1_exp_up, l1_exp_down, l1_sh_gate, l1_sh_up, l1_sh_down, l1_ln2_g, l1_ln2_b):
    B, S, D = x.shape
    x2 = x.reshape(B * S, D)
    x2 = _layer0(x2, B, S, l0_w_in, l0_nsa_pe_k, l0_nsa_pe_v, l0_nsa_cmp_k1, l0_nsa_cmp_k2,
                 l0_nsa_cmp_v1, l0_nsa_cmp_v2, l0_mla_q_norm, l0_mla_w_uq, l0_mla_kv_norm,
                 l0_mla_w_ukv, l0_w_out, l0_ln1_g, l0_ln1_b)
    x2 = _moe_block(x2, l0_router, l0_router_bias, l0_exp_gate, l0_exp_up, l0_exp_down,
                    l0_sh_gate, l0_sh_up, l0_sh_down, l0_ln2_g, l0_ln2_b)
    x2 = _layer1(x2, B, S, l1_w_qkv, l1_w_out, l1_ln1_g, l1_ln1_b)
    x2 = _moe_block(x2, l1_router, l1_router_bias, l1_exp_gate, l1_exp_up, l1_exp_down,
                    l1_sh_gate, l1_sh_up, l1_sh_down, l1_ln2_g, l1_ln2_b)
    return x2.reshape(B, S, D)
```

```python
import functools

import numpy as np
import jax
import jax.numpy as jnp
from jax import lax
from jax.experimental import pallas as pl
from jax.experimental.pallas import tpu as pltpu

F32 = jnp.float32
BF16 = jnp.bfloat16

LANES = 128
VMEM_LIMIT = 48 * 1024 * 1024

D_MODEL = 1024
DEPTH = 2
HEAD_DIM = 64
ROPE_THETA = 10000.0
LN_EPS = 1e-5
RMS_EPS = 1e-6

NSA_HEADS = 8
NSA_KV_HEADS = 2
NSA_GROUP = NSA_HEADS // NSA_KV_HEADS
NSA_CMP_LEN = 32
NSA_CMP_STRIDE = 16
NSA_CMP_HIDDEN = 128
NSA_SEL_BLOCK = 64
NSA_SEL_TOPN = 8
NSA_WINDOW = 512

MLA_HEADS = 8
MLA_Q_RANK = 256
MLA_KV_RANK = 128
MLA_NOPE_DIM = 64
MLA_ROPE_DIM = 32
MLA_V_DIM = 64

MOBA_HEADS = 16
MOBA_BLOCK = 256
MOBA_TOPK = 3

N_EXPERTS = 64
N_GROUPS = 8
GROUP_SIZE = N_EXPERTS // N_GROUPS
TOPK_GROUPS = 4
TOP_K = 8
EXPERT_FF = 256
ROUTED_SCALE = 2.5

DN_ALPHA = (2 * DEPTH) ** 0.25

ROW_TILE = 512
SELECT_TILE = 1024
ATTN_TQ = 256
MOE_TILE = 1024
MOE_EXPERTS_PER_STEP = 5
FLASH_HEADS = 8
_OWN_KV_HEADS = tuple((h, h, h) for h in range(FLASH_HEADS))

NEG_INF = float("-inf")


def _cparams(*sem):
    return pltpu.CompilerParams(dimension_semantics=sem, vmem_limit_bytes=VMEM_LIMIT)


def _dot(a, b):
    return jnp.dot(a.astype(BF16), b.astype(BF16), preferred_element_type=F32)


def _dot_nt(a, b):
    return lax.dot_general(a.astype(BF16), b.astype(BF16), (((1,), (1,)), ((), ())),
                           preferred_element_type=F32)


def _split(a):
    hi = a.astype(BF16)
    lo = (a - hi.astype(F32)).astype(BF16)
    return hi, lo


def _dot_split_lhs(a, b_bf16):
    hi, lo = _split(a)
    return (jnp.dot(hi, b_bf16, preferred_element_type=F32)
            + jnp.dot(lo, b_bf16, preferred_element_type=F32))


def _dot_split_both(a, b):
    ah, al = _split(a)
    bh, bl = _split(b)
    return (jnp.dot(ah, bh, preferred_element_type=F32)
            + jnp.dot(al, bh, preferred_element_type=F32)
            + jnp.dot(ah, bl, preferred_element_type=F32))


def _lane_iota(shape):
    return lax.broadcasted_iota(jnp.int32, shape, len(shape) - 1)


def _rope_lanes(x, cos, sin, first_half, half):
    n = x.shape[-1]
    rot = jnp.where(first_half, -pltpu.roll(x, n - half, 1), pltpu.roll(x, half, 1))
    return x * cos + rot * sin


def _layer_norm(z, g, b):
    mu = jnp.mean(z, axis=-1, keepdims=True)
    zc = z - mu
    var = jnp.mean(zc * zc, axis=-1, keepdims=True)
    return zc * lax.rsqrt(var + LN_EPS) * g + b


def _rms_norm(x, g):
    return x * lax.rsqrt(jnp.mean(x * x, axis=-1, keepdims=True) + RMS_EPS) * g


def _group_rank(x, group):
    n = x.shape[-1]
    pos = _lane_iota(x.shape) % group
    rank = jnp.zeros(x.shape, F32)
    for d in range(1, group):
        lower = pltpu.roll(x, d, 1)
        upper = pltpu.roll(x, n - d, 1)
        rank = rank + jnp.where((pos >= d) & (lower >= x), 1.0, 0.0)
        rank = rank + jnp.where((pos + d < group) & (upper > x), 1.0, 0.0)
    return rank


def _group_sum(x, group):
    n = x.shape[-1]
    pos = _lane_iota(x.shape) % group
    tot = x
    for d in range(1, group):
        tot = tot + jnp.where(pos >= d, pltpu.roll(x, d, 1), 0.0)
        tot = tot + jnp.where(pos + d < group, pltpu.roll(x, n - d, 1), 0.0)
    return tot


def _take_top(x, key, k):
    big = jnp.int32(1 << 30)
    taken = jnp.zeros(x.shape, jnp.bool_)
    for _ in range(k):
        m = jnp.max(x, axis=-1, keepdims=True)
        first = jnp.min(jnp.where(x == m, key, big), axis=-1, keepdims=True)
        hit = key == first
        taken = taken | hit
        x = jnp.where(hit, NEG_INF, x)
    return taken


def _rope_tables(S):
    pos = jnp.arange(S, dtype=F32)[:, None]

    def cs(half):
        inv = ROPE_THETA ** (-jnp.arange(half, dtype=F32) / half)
        ang = pos * inv[None, :]
        return jnp.cos(ang), jnp.sin(ang)

    c32, s32 = cs(HEAD_DIM // 2)
    c16, s16 = cs(MLA_ROPE_DIM // 2)
    one = jnp.ones((S, 1), F32)
    zero = jnp.zeros((S, 1), F32)
    cos_pair = jnp.concatenate([c32] * 4, -1)
    sin_pair = jnp.concatenate([s32] * 4, -1)
    cos_kv = jnp.concatenate([c32, c32, jnp.tile(one, (1, 64))], -1)
    sin_kv = jnp.concatenate([s32, s32, jnp.tile(zero, (1, 64))], -1)
    cos_mla = jnp.concatenate([jnp.tile(one, (1, 64)), c16, c16, jnp.tile(one, (1, 32))], -1)
    sin_mla = jnp.concatenate([jnp.tile(zero, (1, 64)), s16, s16, jnp.tile(zero, (1, 32))], -1)
    return jnp.stack([cos_pair, sin_pair, cos_kv, sin_kv, cos_mla, sin_mla])


def _cmp_rope_tables(n_chunks):
    pos = (jnp.arange(n_chunks, dtype=F32) * NSA_CMP_STRIDE + (NSA_CMP_LEN - 1))[:, None]
    half = HEAD_DIM // 2
    inv = ROPE_THETA ** (-jnp.arange(half, dtype=F32) / half)
    ang = pos * inv[None, :]
    c, s = jnp.cos(ang), jnp.sin(ang)
    one = jnp.ones((n_chunks, 64), F32)
    return jnp.stack([jnp.concatenate([c, c, one], -1), jnp.concatenate([s, s, 0 * one], -1)])


P0_Q = 0
P0_KVS = 512
P0_KVW = 768
P0_KC = 1024
P0_VC = 1152
P0_GATE = 1280
P0_CQ = 1536
P0_CKV = 1792
P0_KR = 1920
P0_W = 2048


def _proj0_columns():
    q_w = NSA_HEADS * HEAD_DIM
    kv0 = q_w
    piece = NSA_KV_HEADS * HEAD_DIM
    gate0 = kv0 + 6 * piece
    cq0 = gate0 + 3 * NSA_HEADS
    ckv0 = cq0 + MLA_Q_RANK
    kr0 = ckv0 + MLA_KV_RANK
    src = -np.ones((P0_W,), np.int64)
    src[P0_Q:P0_Q + q_w] = np.arange(q_w)
    d = np.arange(HEAD_DIM)
    for base, kp, vp in ((P0_KVS, 2, 3), (P0_KVW, 4, 5)):
        for g in range(NSA_KV_HEADS):
            src[base + g * 128 + d] = kv0 + kp * piece + g * HEAD_DIM + d
            src[base + g * 128 + 64 + d] = kv0 + vp * piece + g * HEAD_DIM + d
    src[P0_KC:P0_KC + piece] = kv0 + 0 * piece + np.arange(piece)
    src[P0_VC:P0_VC + piece] = kv0 + 1 * piece + np.arange(piece)
    for g in range(NSA_KV_HEADS):
        for br in range(3):
            for r in range(NSA_GROUP):
                src[P0_GATE + g * 128 + br * NSA_GROUP + r] = gate0 + br * NSA_HEADS + g * NSA_GROUP + r
    src[P0_CQ:P0_CQ + MLA_Q_RANK] = cq0 + np.arange(MLA_Q_RANK)
    src[P0_CKV:P0_CKV + MLA_KV_RANK] = ckv0 + np.arange(MLA_KV_RANK)
    src[P0_KR + 64:P0_KR + 64 + MLA_ROPE_DIM] = kr0 + np.arange(MLA_ROPE_DIM)
    return src


def _permute_columns(w, src):
    cols = jnp.take(w, jnp.asarray(np.maximum(src, 0)), axis=1)
    return jnp.where(jnp.asarray(src >= 0)[None, :], cols, 0.0)


def _block_onehot(pos, block, shape):
    return jnp.where(_lane_iota(shape) == HEAD_DIM + pos // block, 1.0, 0.0)


def _proj0_kernel(x_ref, w_ref, tab_ref, qg_ref, kvg_ref, wuq_ref, wk_ref, wv_ref,
                  q_ref, ks_ref, vs_ref, kw_ref, vw_ref, kc_ref, vc_ref, gate_ref,
                  qm_ref, km_ref, vm_ref, *, seq_tiles):
    tm = x_ref.shape[0]
    xb = x_ref[...].astype(BF16)
    lane = _lane_iota((tm, LANES))
    low = lane < HEAD_DIM
    pair_first = (lane % HEAD_DIM) < (HEAD_DIM // 2)
    mla_first = lane < (MLA_NOPE_DIM + MLA_ROPE_DIM // 2)
    cos_p, sin_p = tab_ref[0], tab_ref[1]
    cos_kv, sin_kv = tab_ref[2], tab_ref[3]
    cos_m, sin_m = tab_ref[4], tab_ref[5]
    pos = (pl.program_id(0) % seq_tiles) * tm + lax.broadcasted_iota(jnp.int32, (tm, 1), 0)
    sel_onehot = _block_onehot(pos, NSA_SEL_BLOCK, (tm, LANES))
    one_lane = jnp.where(lane == HEAD_DIM, 1.0, 0.0)

    def seg(c0, width):
        return jnp.dot(xb, w_ref[:, c0:c0 + width], preferred_element_type=F32)

    q = seg(P0_Q, 512)
    for j in range(4):
        blk = _rope_lanes(q[:, j * 128:(j + 1) * 128], cos_p, sin_p, pair_first, 32)
        q_ref[:, j * 128:(j + 1) * 128] = (blk * (HEAD_DIM ** -0.5)).astype(BF16)
    for c0, k_out, v_out, extra in ((P0_KVS, ks_ref, vs_ref, sel_onehot), (P0_KVW, kw_ref, vw_ref, 0.0)):
        kv = seg(c0, 256)
        for j in range(2):
            blk = _rope_lanes(kv[:, j * 128:(j + 1) * 128], cos_kv, sin_kv, pair_first, 32)
            k_out[:, j * 128:(j + 1) * 128] = jnp.where(low, blk, extra).astype(BF16)
            v_out[:, j * 128:(j + 1) * 128] = jnp.where(low, pltpu.roll(blk, HEAD_DIM, 1), one_lane).astype(BF16)
    kc_ref[...] = seg(P0_KC, 128)
    vc_ref[...] = seg(P0_VC, 128)
    gate_ref[...] = seg(P0_GATE, 256)

    cq = _rms_norm(seg(P0_CQ, 256), qg_ref[...])
    qm = _dot(cq, wuq_ref[...])
    mla_scale = (MLA_NOPE_DIM + MLA_ROPE_DIM) ** -0.5
    for h in range(MLA_HEADS):
        blk = _rope_lanes(qm[:, h * 128:(h + 1) * 128], cos_m, sin_m, mla_first, 16)
        qm_ref[:, h * 128:(h + 1) * 128] = (blk * mla_scale).astype(BF16)
    ckv = _rms_norm(seg(P0_CKV, 128), kvg_ref[...]).astype(BF16)
    kn = jnp.dot(ckv, wk_ref[...], preferred_element_type=F32)
    kpe = _rope_lanes(seg(P0_KR, 128), cos_m, sin_m, mla_first, 16)
    for h in range(MLA_HEADS):
        km_ref[:, h * 128:(h + 1) * 128] = (kn[:, h * 128:(h + 1) * 128] + kpe).astype(BF16)
    vm = jnp.dot(ckv, wv_ref[...], preferred_element_type=F32)
    for h in range(MLA_HEADS):
        vm_ref[:, h * 128:(h + 1) * 128] = jnp.where(low, vm[:, h * 128:(h + 1) * 128], one_lane).astype(BF16)


def _proj0(x2, w_perm, tabs, q_norm, kv_norm, wuq, wk, wv, S):
    T = x2.shape[0]
    tm = ROW_TILE
    ns = S // tm
    row = lambda w: pl.BlockSpec((tm, w), lambda i: (i, 0))
    full = lambda a: pl.BlockSpec(a.shape, lambda i: (0,) * a.ndim)
    widths = (512, 256, 256, 256, 256, 128, 128, 256, 1024, 1024, 1024)
    dtypes = (BF16, BF16, BF16, BF16, BF16, F32, F32, F32, BF16, BF16, BF16)
    return pl.pallas_call(
        functools.partial(_proj0_kernel, seq_tiles=ns),
        grid=(T // tm,),
        in_specs=[row(D_MODEL), full(w_perm),
                  pl.BlockSpec((6, tm, LANES), lambda i: (0, i % ns, 0)),
                  full(q_norm), full(kv_norm), full(wuq), full(wk), full(wv)],
        out_specs=[row(w) for w in widths],
        out_shape=[jax.ShapeDtypeStruct((T, w), d) for w, d in zip(widths, dtypes)],
        compiler_params=_cparams("parallel"),
    )(x2, w_perm, tabs, q_norm, kv_norm, wuq, wk, wv)


def _compress_kernel(kc_ref, vc_ref, pe_ref, w1k_ref, w2k_ref, w1v_ref, w2v_ref, tab_ref,
                     ko_ref, vo_ref):
    half = NSA_CMP_STRIDE * HEAD_DIM
    n = kc_ref.shape[1]

    def mlp(x, pe_lo, pe_hi, w1_ref, w2_ref):
        first = _dot(x + pe_lo, w1_ref[0:half, :])
        second = _dot(x + pe_hi, w1_ref[half:2 * half, :])
        hidden = first + pltpu.roll(second, n - 1, 0)
        return _dot(jax.nn.gelu(hidden), w2_ref[...])

    k = mlp(kc_ref[0], pe_ref[0:1, :], pe_ref[1:2, :], w1k_ref, w2k_ref)
    lane = _lane_iota(k.shape)
    ko_ref[0] = _rope_lanes(k, tab_ref[0], tab_ref[1], lane < HEAD_DIM // 2, 32).astype(BF16)
    vo_ref[0] = mlp(vc_ref[0], pe_ref[2:3, :], pe_ref[3:4, :], w1v_ref, w2v_ref).astype(BF16)


def _compress(kc_chunks, vc_chunks, pe, w1k, w2k, w1v, w2v, ctab):
    n_bg, n, width = kc_chunks.shape
    blk = pl.BlockSpec((1, n, width), lambda i: (i, 0, 0))
    full = lambda a: pl.BlockSpec(a.shape, lambda i: (0,) * a.ndim)
    out = pl.BlockSpec((1, n, LANES), lambda i: (i, 0, 0))
    return pl.pallas_call(
        _compress_kernel,
        grid=(n_bg,),
        in_specs=[blk, blk, full(pe), full(w1k), full(w2k), full(w1v), full(w2v), full(ctab)],
        out_specs=[out, out],
        out_shape=[jax.ShapeDtypeStruct((n_bg, n, LANES), BF16)] * 2,
        compiler_params=_cparams("parallel"),
    )(kc_chunks, vc_chunks, pe, w1k, w2k, w1v, w2v, ctab)


def _cmp_attn_kernel(q_ref, k_ref, v_ref, gate_ref, c2s_ref, o_ref, qa_ref, *, n_sel, n_top):
    qi = pl.program_id(2)
    tq = q_ref.shape[1]
    n = k_ref.shape[1]
    q = q_ref[0]
    k = k_ref[0][:, 0:HEAD_DIM]
    v = v_ref[0][:, 0:HEAD_DIM]
    gates = jax.nn.sigmoid(gate_ref[0])
    pos = qi * tq + lax.broadcasted_iota(jnp.int32, (tq, 1), 0)
    cmp_end = lax.broadcasted_iota(jnp.int32, (1, n), 1) * NSA_CMP_STRIDE + (NSA_CMP_LEN - 1)
    visible = cmp_end <= pos
    p_sum = jnp.zeros((tq, n), F32)
    outs = []
    for r in range(NSA_GROUP):
        s = jnp.where(visible, _dot_nt(q[:, r * HEAD_DIM:(r + 1) * HEAD_DIM], k), NEG_INF)
        m = jnp.max(s, axis=-1, keepdims=True)
        e = jnp.exp(s - jnp.where(m > NEG_INF, m, 0.0))
        den = jnp.sum(e, axis=-1, keepdims=True)
        p = e / jnp.where(den > 0, den, 1.0)
        p_sum = p_sum + p
        outs.append(_dot(p, v) * gates[:, r:r + 1])
    o_ref[0] = jnp.concatenate(outs, axis=-1)

    imp = _dot_split_lhs(p_sum, c2s_ref[...])
    blk = _lane_iota(imp.shape)
    cur = pos // NSA_SEL_BLOCK
    forced = (blk == 0) | (blk == cur) | (blk == cur - 1)
    valid = (blk <= cur) & (blk < n_sel)
    free = jnp.where(valid & ~forced, imp, NEG_INF)
    taken = forced | _take_top(free, blk, n_top - 3)
    bias = jnp.where(blk < n_sel, jnp.where(taken & valid, 0.0, -1e30), 0.0)
    bias = pltpu.roll(bias, HEAD_DIM, 1)
    low = blk < HEAD_DIM
    qf = q.astype(F32)
    for r in range(NSA_GROUP):
        x = qf[:, (r // 2) * LANES:(r // 2 + 1) * LANES]
        if r % 2 == 1:
            x = pltpu.roll(x, HEAD_DIM, 1)
        qa_ref[0, :, r * LANES:(r + 1) * LANES] = jnp.where(low, x, bias).astype(BF16)


def _cmp_attn(q3, kcmp, vcmp, gate3, c2s, n_sel):
    B, S, _ = q3.shape
    n = kcmp.shape[1]
    tq = SELECT_TILE
    G = NSA_KV_HEADS
    assert n_sel >= 3 and NSA_SEL_TOPN >= 3
    kern = functools.partial(_cmp_attn_kernel, n_sel=n_sel, n_top=min(NSA_SEL_TOPN, n_sel))
    return pl.pallas_call(
        kern,
        grid=(B, G, S // tq),
        in_specs=[pl.BlockSpec((1, tq, 256), lambda b, g, i: (b, i, g)),
                  pl.BlockSpec((1, n, LANES), lambda b, g, i: (b * G + g, 0, 0)),
                  pl.BlockSpec((1, n, LANES), lambda b, g, i: (b * G + g, 0, 0)),
                  pl.BlockSpec((1, tq, LANES), lambda b, g, i: (b, i, g)),
                  pl.BlockSpec(c2s.shape, lambda b, g, i: (0, 0))],
        out_specs=[pl.BlockSpec((1, tq, 256), lambda b, g, i: (b, i, g)),
                   pl.BlockSpec((1, tq, 512), lambda b, g, i: (b, i, g))],
        out_shape=[jax.ShapeDtypeStruct((B, S, 512), F32),
                   jax.ShapeDtypeStruct((B, S, NSA_HEADS * LANES), BF16)],
        compiler_params=_cparams("parallel", "parallel", "parallel"),
    )(q3, kcmp, vcmp, gate3, c2s)


def _flash_kernel(*refs, heads, window, gate_cols):
    it = iter(refs)
    q_ref, k_ref, v_ref = next(it), next(it), next(it)
    gate_ref = next(it) if gate_cols is not None else None
    o_ref = next(it)
    s_sc, mx_sc, mb_sc, acc_sc = next(it), next(it), next(it), next(it)

    qi = pl.program_id(2)
    tq = q_ref.shape[1]
    tk = tq
    nh = len(heads)

    lane = _lane_iota((tq, LANES))
    low = lane < HEAD_DIM
    q_heads = [q_ref[0, :, g * LANES:(g + 1) * LANES] for (g, _, _) in heads]
    mx_sc[...] = jnp.full(mx_sc.shape, NEG_INF, F32)
    acc_sc[...] = jnp.zeros(acc_sc.shape, F32)

    def score_tile(j, pos_mask):
        start = pl.multiple_of(j * tk, tk)
        k = k_ref[0, pl.ds(start, tk), :]
        for h, (_, kg, _) in enumerate(heads):
            s = _dot_nt(q_heads[h], k[:, kg * LANES:(kg + 1) * LANES])
            if pos_mask is not None:
                s = jnp.where(pos_mask, s, NEG_INF)
            s_sc[h, j] = s
            best = s[:, 0:LANES]
            for c in range(1, tk // LANES):
                best = jnp.maximum(best, s[:, c * LANES:(c + 1) * LANES])
            mx_sc[h] = jnp.maximum(mx_sc[h], best)

    def value_tile(j):
        start = pl.multiple_of(j * tk, tk)
        v = v_ref[0, pl.ds(start, tk), :]
        for h, (_, _, vg) in enumerate(heads):
            mb = mb_sc[h]
            s = s_sc[h, j]
            p = jnp.concatenate([jnp.exp(s[:, c * LANES:(c + 1) * LANES] - mb)
                                 for c in range(tk // LANES)], axis=-1).astype(BF16)
            acc_sc[h] += jnp.dot(p, v[:, vg * LANES:(vg + 1) * LANES], preferred_element_type=F32)

    def for_each_tile(fn_full, fn_masked):
        row = lax.broadcasted_iota(jnp.int32, (tq, tk), 0)
        col = lax.broadcasted_iota(jnp.int32, (tq, tk), 1)
        first_full = 0
        if window is not None:
            back = window // tk
            first_full = jnp.maximum(qi - back + 1, 0)

            @pl.when(qi >= back)
            def _():
                fn_masked(qi - back, col > row)

        def body(j, carry):
            fn_full(j)
            return carry

        lax.fori_loop(first_full, qi, body, 0)
        fn_masked(qi, col <= row)

    for_each_tile(lambda j: score_tile(j, None), score_tile)
    for h in range(nh):
        m = jnp.max(mx_sc[h], axis=-1, keepdims=True)
        mb_sc[h] = jnp.broadcast_to(m, (tq, LANES))
    for_each_tile(value_tile, lambda j, mask: value_tile(j))

    if gate_ref is not None:
        gates = jax.nn.sigmoid(gate_ref[0])
    results = []
    for h in range(nh):
        acc = acc_sc[h]
        o = acc / acc[:, HEAD_DIM:HEAD_DIM + 1]
        if gate_ref is not None:
            c = gate_cols[h]
            o = o * gates[:, c:c + 1]
        results.append(o)
    for pair in range(nh // 2):
        high = pltpu.roll(results[2 * pair + 1], HEAD_DIM, 1)
        o_ref[0, :, pair * LANES:(pair + 1) * LANES] = jnp.where(low, results[2 * pair], high).astype(o_ref.dtype)


def _flash(q, k, v, gate, *, n_steps, q_w, k_w, v_w, heads, window=None, gate_cols=None,
           out_dtype=F32):
    B, S, _ = q.shape
    tq = ATTN_TQ
    nh = len(heads)
    assert nh % 2 == 0 and S % tq == 0 and (window is None or window % tq == 0)
    n_stash = S // tq
    in_specs = [pl.BlockSpec((1, tq, q_w), lambda b, h, i: (b, i, h)),
                pl.BlockSpec((1, S, k_w), lambda b, h, i: (b, 0, h)),
                pl.BlockSpec((1, S, v_w), lambda b, h, i: (b, 0, h))]
    args = [q, k, v]
    if gate is not None:
        in_specs.append(pl.BlockSpec((1, tq, gate.shape[-1] // n_steps), lambda b, h, i: (b, i, h)))
        args.append(gate)
    kern = functools.partial(_flash_kernel, heads=tuple(heads), window=window, gate_cols=gate_cols)
    return pl.pallas_call(
        kern,
        grid=(B, n_steps, S // tq),
        in_specs=in_specs,
        out_specs=pl.BlockSpec((1, tq, nh * HEAD_DIM), lambda b, h, i: (b, i, h)),
        out_shape=jax.ShapeDtypeStruct((B, S, n_steps * nh * HEAD_DIM), out_dtype),
        scratch_shapes=[pltpu.VMEM((nh, n_stash, tq, tq), F32), pltpu.VMEM((nh, tq, LANES), F32),
                        pltpu.VMEM((nh, tq, LANES), F32), pltpu.VMEM((nh, tq, LANES), F32)],
        compiler_params=_cparams("parallel", "parallel", "arbitrary"),
    )(*args)


def _outproj_kernel(*refs, group_sizes):
    it = iter(refs)
    y = None
    for n_in in group_sizes:
        acts = [next(it)[...].astype(F32) for _ in range(n_in)]
        w_ref = next(it)
        a = acts[0]
        for extra in acts[1:]:
            a = a + extra
        part = _dot(a, w_ref[...])
        y = part if y is None else y + part
    x_ref, g_ref, b_ref, o_ref = next(it), next(it), next(it), next(it)
    o_ref[...] = _layer_norm(DN_ALPHA * x_ref[...] + y, g_ref[...], b_ref[...])


def _outproj_ln(groups, x2, g, b):
    T = x2.shape[0]
    tm = ROW_TILE
    in_specs, args, sizes = [], [], []
    for acts, w in groups:
        for a in acts:
            in_specs.append(pl.BlockSpec((tm, a.shape[1]), lambda i: (i, 0)))
            args.append(a)
        in_specs.append(pl.BlockSpec(w.shape, lambda i: (0, 0)))
        args.append(w)
        sizes.append(len(acts))
    in_specs += [pl.BlockSpec((tm, D_MODEL), lambda i: (i, 0)),
                 pl.BlockSpec((1, D_MODEL), lambda i: (0, 0)),
                 pl.BlockSpec((1, D_MODEL), lambda i: (0, 0))]
    args += [x2, g.reshape(1, -1), b.reshape(1, -1)]
    return pl.pallas_call(
        functools.partial(_outproj_kernel, group_sizes=tuple(sizes)),
        grid=(T // tm,),
        in_specs=in_specs,
        out_specs=pl.BlockSpec((tm, D_MODEL), lambda i: (i, 0)),
        out_shape=jax.ShapeDtypeStruct((T, D_MODEL), F32),
        compiler_params=_cparams("parallel"),
    )(*args)


def _router_kernel(x_ref, w_ref, bias_ref, g_ref):
    logits = _dot_split_both(x_ref[...], w_ref[...])
    lane = _lane_iota(logits.shape)
    real = lane < N_EXPERTS
    scores = jax.nn.sigmoid(logits)
    choice = jnp.where(real, scores + bias_ref[...], NEG_INF)
    top2 = jnp.where(_group_rank(choice, GROUP_SIZE) < 2, choice, 0.0)
    grp_score = jnp.where(real, _group_sum(top2, GROUP_SIZE), NEG_INF)
    grp_taken = _take_top(grp_score, lane // GROUP_SIZE, TOPK_GROUPS)
    masked = jnp.where(grp_taken & real, choice, NEG_INF)
    taken = _take_top(masked, lane, TOP_K)
    w = jnp.where(taken, scores, 0.0)
    w = w / jnp.sum(w, axis=-1, keepdims=True) * ROUTED_SCALE
    g_ref[...] = jnp.where(lane == N_EXPERTS, 1.0, w)


def _router(x2, w_router_pad, bias_pad):
    T = x2.shape[0]
    tm = SELECT_TILE
    return pl.pallas_call(
        _router_kernel,
        grid=(T // tm,),
        in_specs=[pl.BlockSpec((tm, D_MODEL), lambda i: (i, 0)),
                  pl.BlockSpec(w_router_pad.shape, lambda i: (0, 0)),
                  pl.BlockSpec((1, LANES), lambda i: (0, 0))],
        out_specs=pl.BlockSpec((tm, LANES), lambda i: (i, 0)),
        out_shape=jax.ShapeDtypeStruct((T, LANES), F32),
        compiler_params=_cparams("parallel"),
    )(x2, w_router_pad, bias_pad)


def _moe_kernel(x_ref, gates_ref, wg_ref, wu_ref, wd_ref, g_ref, b_ref, o_ref, xb_sc, acc_sc):
    step = pl.program_id(1)
    last = pl.num_programs(1) - 1
    per = wg_ref.shape[0]

    @pl.when(step == 0)
    def _():
        xb_sc[...] = x_ref[...].astype(BF16)
        acc_sc[...] = jnp.zeros(acc_sc.shape, F32)

    xb = xb_sc[...]
    gates = gates_ref[...]
    lane = _lane_iota(gates.shape)
    hidden = []
    for j in range(per):
        hg = jnp.dot(xb, wg_ref[j], preferred_element_type=F32)
        hu = jnp.dot(xb, wu_ref[j], preferred_element_type=F32)
        col = jnp.sum(jnp.where(lane == step * per + j, gates, 0.0), axis=-1, keepdims=True)
        hidden.append((jax.nn.silu(hg) * hu * col).astype(BF16))
    h = jnp.concatenate(hidden, axis=-1)
    wd = wd_ref[...].reshape(per * EXPERT_FF, D_MODEL)
    acc_sc[...] += jnp.dot(h, wd, preferred_element_type=F32)

    @pl.when(step == last)
    def _():
        o_ref[...] = _layer_norm(DN_ALPHA * x_ref[...] + acc_sc[...], g_ref[...], b_ref[...])


def _moe_ln(x2, gates, wg, wu, wd, g, b):
    T = x2.shape[0]
    tm = MOE_TILE
    per = MOE_EXPERTS_PER_STEP
    n_steps = wg.shape[0] // per
    assert n_steps * per == wg.shape[0]
    return pl.pallas_call(
        _moe_kernel,
        grid=(T // tm, n_steps),
        in_specs=[pl.BlockSpec((tm, D_MODEL), lambda i, e: (i, 0)),
                  pl.BlockSpec((tm, LANES), lambda i, e: (i, 0)),
                  pl.BlockSpec((per, D_MODEL, EXPERT_FF), lambda i, e: (e, 0, 0)),
                  pl.BlockSpec((per, D_MODEL, EXPERT_FF), lambda i, e: (e, 0, 0)),
                  pl.BlockSpec((per, EXPERT_FF, D_MODEL), lambda i, e: (e, 0, 0)),
                  pl.BlockSpec((1, D_MODEL), lambda i, e: (0, 0)),
                  pl.BlockSpec((1, D_MODEL), lambda i, e: (0, 0))],
        out_specs=pl.BlockSpec((tm, D_MODEL), lambda i, e: (i, 0)),
        out_shape=jax.ShapeDtypeStruct((T, D_MODEL), F32),
        scratch_shapes=[pltpu.VMEM((tm, D_MODEL), BF16), pltpu.VMEM((tm, D_MODEL), F32)],
        compiler_params=_cparams("parallel", "arbitrary"),
    )(x2, gates, wg, wu, wd, g.reshape(1, -1), b.reshape(1, -1))


def _moe_block(x2, router, router_bias, exp_gate, exp_up, exp_down, sh_gate, sh_up, sh_down, g, b):
    w_router_pad = jnp.pad(router, ((0, 0), (0, LANES - N_EXPERTS)))
    bias_pad = jnp.pad(router_bias, (0, LANES - N_EXPERTS)).reshape(1, LANES)
    gates = _router(x2, w_router_pad, bias_pad)
    wg = jnp.concatenate([exp_gate, sh_gate[None]], 0).astype(BF16)
    wu = jnp.concatenate([exp_up, sh_up[None]], 0).astype(BF16)
    wd = jnp.concatenate([exp_down, sh_down[None]], 0).astype(BF16)
    return _moe_ln(x2, gates, wg, wu, wd, g, b)


def _qkv1_kernel(x_ref, w_ref, tab_ref, q_ref, k_ref, v_ref, kmean_ref, *, seq_tiles):
    tm = x_ref.shape[0]
    xb = x_ref[...].astype(BF16)
    width = MOBA_HEADS * HEAD_DIM
    lane = _lane_iota((tm, LANES))
    low = lane < HEAD_DIM
    pair_first = (lane % HEAD_DIM) < (HEAD_DIM // 2)
    cos_p, sin_p = tab_ref[0], tab_ref[1]
    pos = (pl.program_id(0) % seq_tiles) * tm + lax.broadcasted_iota(jnp.int32, (tm, 1), 0)
    onehot = _block_onehot(pos, MOBA_BLOCK, (tm, LANES))
    one_lane = jnp.where(lane == HEAD_DIM, 1.0, 0.0)
    wide = 2 * LANES
    for j in range(width // wide):
        q4 = jnp.dot(xb, w_ref[:, j * wide:(j + 1) * wide], preferred_element_type=F32)
        k4 = jnp.dot(xb, w_ref[:, width + j * wide:width + (j + 1) * wide], preferred_element_type=F32)
        v4 = jnp.dot(xb, w_ref[:, 2 * width + j * wide:2 * width + (j + 1) * wide],
                     preferred_element_type=F32)
        for half in range(2):
            c = j * wide + half * LANES
            part = slice(half * LANES, (half + 1) * LANES)
            q = _rope_lanes(q4[:, part], cos_p, sin_p, pair_first, 32)
            q_ref[:, c:c + LANES] = (q * (HEAD_DIM ** -0.5)).astype(BF16)
            k = _rope_lanes(k4[:, part], cos_p, sin_p, pair_first, 32)
            k_ref[:, 2 * c:2 * c + LANES] = jnp.where(low, k, onehot).astype(BF16)
            k_ref[:, 2 * c + LANES:2 * c + 2 * LANES] = jnp.where(
                low, pltpu.roll(k, HEAD_DIM, 1), onehot).astype(BF16)
            for blk in range(tm // MOBA_BLOCK):
                kmean_ref[blk, :, c:c + LANES] = jnp.mean(
                    k[blk * MOBA_BLOCK:(blk + 1) * MOBA_BLOCK], axis=0, keepdims=True)
            v = v4[:, part]
            v_ref[:, 2 * c:2 * c + LANES] = jnp.where(low, v, one_lane).astype(BF16)
            v_ref[:, 2 * c + LANES:2 * c + 2 * LANES] = jnp.where(
                low, pltpu.roll(v, HEAD_DIM, 1), one_lane).astype(BF16)


def _qkv1(x2, w_qkv, tabs, S):
    T = x2.shape[0]
    tm = ROW_TILE
    assert tm % MOBA_BLOCK == 0
    ns = S // tm
    width = MOBA_HEADS * HEAD_DIM
    row = lambda w: pl.BlockSpec((tm, w), lambda i: (i, 0))
    return pl.pallas_call(
        functools.partial(_qkv1_kernel, seq_tiles=ns),
        grid=(T // tm,),
        in_specs=[pl.BlockSpec((tm, D_MODEL), lambda i: (i, 0)),
                  pl.BlockSpec(w_qkv.shape, lambda i: (0, 0)),
                  pl.BlockSpec((2, tm, LANES), lambda i: (0, i % ns, 0))],
        out_specs=[row(width), row(2 * width), row(2 * width),
                   pl.BlockSpec((tm // MOBA_BLOCK, 1, width), lambda i: (i, 0, 0))],
        out_shape=[jax.ShapeDtypeStruct((T, width), BF16), jax.ShapeDtypeStruct((T, 2 * width), BF16),
                   jax.ShapeDtypeStruct((T, 2 * width), BF16),
                   jax.ShapeDtypeStruct((T // MOBA_BLOCK, 1, width), F32)],
        compiler_params=_cparams("parallel"),
    )(x2, w_qkv, tabs)


def _moba_select_kernel(q_ref, km_ref, qa_ref, *, n_top, n_blocks):
    i = pl.program_id(1)
    tq = q_ref.shape[1]
    gate = _dot_split_lhs_rhs(q_ref[0], km_ref[0])
    lane = _lane_iota(gate.shape)
    blk = lane % 8
    own = (i * tq + lax.broadcasted_iota(jnp.int32, (tq, 1), 0)) // MOBA_BLOCK
    score = jnp.where((blk < own) & (blk < n_blocks), gate, NEG_INF)
    rank = _group_rank(score, 8)
    chosen = ((score > NEG_INF) & (rank < n_top)) | (blk == own)
    bias = jnp.where(chosen, 0.0, -1e30)
    low = lane < HEAD_DIM
    in_bias = (lane >= HEAD_DIM) & (lane < HEAD_DIM + 8)
    for pair in range(MOBA_HEADS // 2):
        qf = q_ref[0, :, pair * LANES:(pair + 1) * LANES].astype(F32)
        for half in range(2):
            h = 2 * pair + half
            x = qf if half == 0 else pltpu.roll(qf, HEAD_DIM, 1)
            b = jnp.where(in_bias, pltpu.roll(bias, (HEAD_DIM - 8 * h) % LANES, 1), 0.0)
            qa_ref[0, :, h * LANES:(h + 1) * LANES] = jnp.where(low, x, b).astype(BF16)


def _dot_split_lhs_rhs(q_bf16, b):
    hi, lo = _split(b)
    return (jnp.dot(q_bf16, hi, preferred_element_type=F32)
            + jnp.dot(q_bf16, lo, preferred_element_type=F32))


def _moba_select(q3, km, n_top, n_blocks):
    B, S, width = q3.shape
    tq = SELECT_TILE
    kern = functools.partial(_moba_select_kernel, n_top=n_top, n_blocks=n_blocks)
    return pl.pallas_call(
        kern,
        grid=(B, S // tq),
        in_specs=[pl.BlockSpec((1, tq, width), lambda b, i: (b, i, 0)),
                  pl.BlockSpec((1, width, LANES), lambda b, i: (b, 0, 0))],
        out_specs=pl.BlockSpec((1, tq, MOBA_HEADS * LANES), lambda b, i: (b, i, 0)),
        out_shape=jax.ShapeDtypeStruct((B, S, MOBA_HEADS * LANES), BF16),
        compiler_params=_cparams("parallel", "parallel"),
    )(q3, km)


def _layer0(x2, B, S, w_in, pe_k, pe_v, cmp_k1, cmp_k2, cmp_v1, cmp_v2,
            q_norm, w_uq, kv_norm, w_ukv, w_out, ln_g, ln_b):
    T = B * S
    G = NSA_KV_HEADS
    w_perm = _permute_columns(w_in, _proj0_columns()).astype(BF16)
    tabs = _rope_tables(S)
    qd = MLA_NOPE_DIM + MLA_ROPE_DIM
    wuq = jnp.pad(w_uq.reshape(MLA_Q_RANK, MLA_HEADS, qd), ((0, 0), (0, 0), (0, LANES - qd)))
    wuq = wuq.reshape(MLA_Q_RANK, MLA_HEADS * LANES).astype(BF16)
    wukv = w_ukv.reshape(MLA_KV_RANK, MLA_HEADS, MLA_NOPE_DIM + MLA_V_DIM)
    wk = jnp.pad(wukv[:, :, :MLA_NOPE_DIM], ((0, 0), (0, 0), (0, LANES - MLA_NOPE_DIM)))
    wk = wk.reshape(MLA_KV_RANK, MLA_HEADS * LANES).astype(BF16)
    wv = jnp.pad(wukv[:, :, MLA_NOPE_DIM:], ((0, 0), (0, 0), (0, LANES - MLA_V_DIM)))
    wv = wv.reshape(MLA_KV_RANK, MLA_HEADS * LANES).astype(BF16)

    q, ks, vs, kw, vw, kc, vc, gate, qm, km, vm = _proj0(
        x2, w_perm, tabs, q_norm.reshape(1, -1), kv_norm.reshape(1, -1), wuq, wk, wv, S)

    n_chunks = S // NSA_CMP_STRIDE
    chunk_w = NSA_CMP_STRIDE * HEAD_DIM

    def chunks(t):
        t = t.reshape(B, S, G, HEAD_DIM).transpose(0, 2, 1, 3)
        return t.reshape(B * G, n_chunks, chunk_w)

    pe = jnp.stack([pe_k[:NSA_CMP_STRIDE].reshape(-1), pe_k[NSA_CMP_STRIDE:].reshape(-1),
                    pe_v[:NSA_CMP_STRIDE].reshape(-1), pe_v[NSA_CMP_STRIDE:].reshape(-1)])
    pad2 = lambda w: jnp.pad(w, ((0, 0), (0, LANES - HEAD_DIM))).astype(BF16)
    kcmp, vcmp = _compress(chunks(kc), chunks(vc), pe, cmp_k1.astype(BF16), pad2(cmp_k2),
                           cmp_v1.astype(BF16), pad2(cmp_v2), _cmp_rope_tables(n_chunks))

    n_sel = S // NSA_SEL_BLOCK
    n_cmp = (S - NSA_CMP_LEN) // NSA_CMP_STRIDE + 1
    tok = np.arange(n_chunks)[:, None] * NSA_CMP_STRIDE + np.arange(NSA_CMP_LEN)[None, :]
    c2s = (tok[:, :, None] // NSA_SEL_BLOCK == np.arange(LANES)[None, None, :]).sum(1) / NSA_CMP_LEN
    c2s[n_cmp:] = 0.0
    c2s = jnp.asarray(c2s, dtype=BF16)

    q3 = q.reshape(B, S, -1)
    gate3 = gate.reshape(B, S, -1)
    o_cmp, q_aug = _cmp_attn(q3, kcmp, vcmp, gate3, c2s, n_sel)

    r3 = lambda t: t.reshape(B, S, -1)
    nsa_heads = [(r, r // NSA_GROUP, r // NSA_GROUP) for r in range(NSA_HEADS)]
    gate_col = lambda branch, r: (r // NSA_GROUP) * LANES + branch * NSA_GROUP + r % NSA_GROUP
    o_sel = _flash(q_aug, r3(ks), r3(vs), gate3, n_steps=1, q_w=NSA_HEADS * LANES, k_w=G * LANES,
                   v_w=G * LANES, heads=nsa_heads, gate_cols=[gate_col(1, r) for r in range(NSA_HEADS)])
    o_win = _flash(q_aug, r3(kw), r3(vw), gate3, n_steps=1, q_w=NSA_HEADS * LANES, k_w=G * LANES,
                   v_w=G * LANES, heads=nsa_heads, window=NSA_WINDOW,
                   gate_cols=[gate_col(2, r) for r in range(NSA_HEADS)])
    o_mla = _flash(r3(qm), r3(km), r3(vm), None, n_steps=MLA_HEADS // FLASH_HEADS,
                   q_w=FLASH_HEADS * LANES, k_w=FLASH_HEADS * LANES, v_w=FLASH_HEADS * LANES,
                   heads=_OWN_KV_HEADS, out_dtype=BF16)

    n_nsa = NSA_HEADS * HEAD_DIM
    w_out_b = w_out.astype(BF16)
    groups = [([o_cmp.reshape(T, -1), o_sel.reshape(T, -1), o_win.reshape(T, -1)], w_out_b[:n_nsa]),
              ([o_mla.reshape(T, -1)], w_out_b[n_nsa:])]
    return _outproj_ln(groups, x2, ln_g, ln_b)


def _layer1(x2, B, S, w_qkv, w_out, ln_g, ln_b):
    T = B * S
    width = MOBA_HEADS * HEAD_DIM
    tabs = _rope_tables(S)[0:2]
    q, k, v, kmean = _qkv1(x2, w_qkv.astype(BF16), tabs, S)
    n_blocks = S // MOBA_BLOCK
    n_top = min(MOBA_TOPK, max(n_blocks - 1, 1))
    km = kmean.reshape(B, n_blocks, width).transpose(0, 2, 1)
    km = jnp.pad(km, ((0, 0), (0, 0), (0, 8 - n_blocks)))
    km = jnp.tile(km, (1, 1, MOBA_HEADS))
    diag = (np.arange(width)[:, None] // HEAD_DIM) == (np.arange(LANES)[None, :] // 8)
    km = jnp.where(jnp.asarray(diag)[None], km, 0.0)
    q3 = q.reshape(B, S, width)
    q_aug = _moba_select(q3, km, n_top, n_blocks)
    o = _flash(q_aug, k.reshape(B, S, 2 * width), v.reshape(B, S, 2 * width), None,
               n_steps=MOBA_HEADS // FLASH_HEADS, q_w=FLASH_HEADS * LANES, k_w=FLASH_HEADS * LANES,
               v_w=FLASH_HEADS * LANES, heads=_OWN_KV_HEADS, out_dtype=BF16)
    return _outproj_ln([([o.reshape(T, width)], w_out.astype(BF16))], x2, ln_g, ln_b)


def kernel(x, l0_w_in, l0_nsa_pe_k, l0_nsa_pe_v, l0_nsa_cmp_k1, l0_nsa_cmp_k2, l0_nsa_cmp_v1, l0_nsa_cmp_v2, l0_mla_q_norm, l0_mla_w_uq, l0_mla_kv_norm, l0_mla_w_ukv, l0_w_out, l0_ln1_g, l0_ln1_b, l0_router, l0_router_bias, l0_exp_gate, l0_exp_up, l0_exp_down, l0_sh_gate, l0_sh_up, l0_sh_down, l0_ln2_g, l0_ln2_b, l1_w_qkv, l1_w_out, l1_ln1_g, l1_ln1_b, l1_router, l1_router_bias, l1_exp_gate, l1_exp_up, l1_exp_down, l1_sh_gate, l1_sh_up, l1_sh_down, l1_ln2_g, l1_ln2_b):
    B, S, D = x.shape
    x2 = x.reshape(B * S, D)
    x2 = _layer0(x2, B, S, l0_w_in, l0_nsa_pe_k, l0_nsa_pe_v, l0_nsa_cmp_k1, l0_nsa_cmp_k2,
                 l0_nsa_cmp_v1, l0_nsa_cmp_v2, l0_mla_q_norm, l0_mla_w_uq, l0_mla_kv_norm,
                 l0_mla_w_ukv, l0_w_out, l0_ln1_g, l0_ln1_b)
    x2 = _moe_block(x2, l0_router, l0_router_bias, l0_exp_gate, l0_exp_up, l0_exp_down,
                    l0_sh_gate, l0_sh_up, l0_sh_down, l0_ln2_g, l0_ln2_b)
    x2 = _layer1(x2, B, S, l1_w_qkv, l1_w_out, l1_ln1_g, l1_ln1_b)
    x2 = _moe_block(x2, l1_router, l1_router_bias, l1_exp_gate, l1_exp_up, l1_exp_down,
                    l1_sh_gate, l1_sh_up, l1_sh_down, l1_ln2_g, l1_ln2_b)
    return x2.reshape(B, S, D)
```

```python
import functools

import numpy as np
import jax
import jax.numpy as jnp
from jax import lax
from jax.experimental import pallas as pl
from jax.experimental.pallas import tpu as pltpu

F32 = jnp.float32
BF16 = jnp.bfloat16

LANES = 128
VMEM_LIMIT = 48 * 1024 * 1024

D_MODEL = 1024
DEPTH = 2
HEAD_DIM = 64
ROPE_THETA = 10000.0
LN_EPS = 1e-5
RMS_EPS = 1e-6

NSA_HEADS = 8
NSA_KV_HEADS = 2
NSA_GROUP = NSA_HEADS // NSA_KV_HEADS
NSA_CMP_LEN = 32
NSA_CMP_STRIDE = 16
NSA_CMP_HIDDEN = 128
NSA_SEL_BLOCK = 64
NSA_SEL_TOPN = 8
NSA_WINDOW = 512

MLA_HEADS = 8
MLA_Q_RANK = 256
MLA_KV_RANK = 128
MLA_NOPE_DIM = 64
MLA_ROPE_DIM = 32
MLA_V_DIM = 64

MOBA_HEADS = 16
MOBA_BLOCK = 256
MOBA_TOPK = 3

N_EXPERTS = 64
N_GROUPS = 8
GROUP_SIZE = N_EXPERTS // N_GROUPS
TOPK_GROUPS = 4
TOP_K = 8
EXPERT_FF = 256
ROUTED_SCALE = 2.5

DN_ALPHA = (2 * DEPTH) ** 0.25

ROW_TILE = 512
SELECT_TILE = 1024
ATTN_TQ = 256
MOE_TILE = 1024
MOE_EXPERTS_PER_STEP = 5
FLASH_HEADS = 8
_OWN_KV_HEADS = tuple((h, h, h) for h in range(FLASH_HEADS))

NEG_INF = float("-inf")


def _cparams(*sem):
    return pltpu.CompilerParams(dimension_semantics=sem, vmem_limit_bytes=VMEM_LIMIT)


def _dot(a, b):
    return jnp.dot(a.astype(BF16), b.astype(BF16), preferred_element_type=F32)


def _dot_nt(a, b):
    return lax.dot_general(a.astype(BF16), b.astype(BF16), (((1,), (1,)), ((), ())),
                           preferred_element_type=F32)


def _split(a):
    hi = a.astype(BF16)
    lo = (a - hi.astype(F32)).astype(BF16)
    return hi, lo


def _dot_split_lhs(a, b_bf16):
    hi, lo = _split(a)
    return (jnp.dot(hi, b_bf16, preferred_element_type=F32)
            + jnp.dot(lo, b_bf16, preferred_element_type=F32))


def _dot_split_both(a, b):
    ah, al = _split(a)
    bh, bl = _split(b)
    return (jnp.dot(ah, bh, preferred_element_type=F32)
            + jnp.dot(al, bh, preferred_element_type=F32)
            + jnp.dot(ah, bl, preferred_element_type=F32))


def _lane_iota(shape):
    return lax.broadcasted_iota(jnp.int32, shape, len(shape) - 1)


def _rope_lanes(x, cos, sin, first_half, half):
    n = x.shape[-1]
    rot = jnp.where(first_half, -pltpu.roll(x, n - half, 1), pltpu.roll(x, half, 1))
    return x * cos + rot * sin


def _layer_norm(z, g, b):
    mu = jnp.mean(z, axis=-1, keepdims=True)
    zc = z - mu
    var = jnp.mean(zc * zc, axis=-1, keepdims=True)
    return zc * lax.rsqrt(var + LN_EPS) * g + b


def _rms_norm(x, g):
    return x * lax.rsqrt(jnp.mean(x * x, axis=-1, keepdims=True) + RMS_EPS) * g


def _group_rank(x, group):
    n = x.shape[-1]
    pos = _lane_iota(x.shape) % group
    rank = jnp.zeros(x.shape, F32)
    for d in range(1, group):
        lower = pltpu.roll(x, d, 1)
        upper = pltpu.roll(x, n - d, 1)
        rank = rank + jnp.where((pos >= d) & (lower >= x), 1.0, 0.0)
        rank = rank + jnp.where((pos + d < group) & (upper > x), 1.0, 0.0)
    return rank


def _group_sum(x, group):
    n = x.shape[-1]
    pos = _lane_iota(x.shape) % group
    tot = x
    for d in range(1, group):
        tot = tot + jnp.where(pos >= d, pltpu.roll(x, d, 1), 0.0)
        tot = tot + jnp.where(pos + d < group, pltpu.roll(x, n - d, 1), 0.0)
    return tot


def _take_top(x, key, k):
    big = jnp.int32(1 << 30)
    taken = jnp.zeros(x.shape, jnp.bool_)
    for _ in range(k):
        m = jnp.max(x, axis=-1, keepdims=True)
        first = jnp.min(jnp.where(x == m, key, big), axis=-1, keepdims=True)
        hit = key == first
        taken = taken | hit
        x = jnp.where(hit, NEG_INF, x)
    return taken


def _rope_tables(S):
    pos = jnp.arange(S, dtype=F32)[:, None]

    def cs(half):
        inv = ROPE_THETA ** (-jnp.arange(half, dtype=F32) / half)
        ang = pos * inv[None, :]
        return jnp.cos(ang), jnp.sin(ang)

    c32, s32 = cs(HEAD_DIM // 2)
    c16, s16 = cs(MLA_ROPE_DIM // 2)
    one = jnp.ones((S, 1), F32)
    zero = jnp.zeros((S, 1), F32)
    cos_pair = jnp.concatenate([c32] * 4, -1)
    sin_pair = jnp.concatenate([s32] * 4, -1)
    cos_kv = jnp.concatenate([c32, c32, jnp.tile(one, (1, 64))], -1)
    sin_kv = jnp.concatenate([s32, s32, jnp.tile(zero, (1, 64))], -1)
    cos_mla = jnp.concatenate([jnp.tile(one, (1, 64)), c16, c16, jnp.tile(one, (1, 32))], -1)
    sin_mla = jnp.concatenate([jnp.tile(zero, (1, 64)), s16, s16, jnp.tile(zero, (1, 32))], -1)
    return jnp.stack([cos_pair, sin_pair, cos_kv, sin_kv, cos_mla, sin_mla])


def _cmp_rope_tables(n_chunks):
    pos = (jnp.arange(n_chunks, dtype=F32) * NSA_CMP_STRIDE + (NSA_CMP_LEN - 1))[:, None]
    half = HEAD_DIM // 2
    inv = ROPE_THETA ** (-jnp.arange(half, dtype=F32) / half)
    ang = pos * inv[None, :]
    c, s = jnp.cos(ang), jnp.sin(ang)
    one = jnp.ones((n_chunks, 64), F32)
    return jnp.stack([jnp.concatenate([c, c, one], -1), jnp.concatenate([s, s, 0 * one], -1)])


P0_Q = 0
P0_KVS = 512
P0_KVW = 768
P0_KC = 1024
P0_VC = 1152
P0_GATE = 1280
P0_CQ = 1536
P0_CKV = 1792
P0_KR = 1920
P0_W = 2048


def _proj0_columns():
    q_w = NSA_HEADS * HEAD_DIM
    kv0 = q_w
    piece = NSA_KV_HEADS * HEAD_DIM
    gate0 = kv0 + 6 * piece
    cq0 = gate0 + 3 * NSA_HEADS
    ckv0 = cq0 + MLA_Q_RANK
    kr0 = ckv0 + MLA_KV_RANK
    src = -np.ones((P0_W,), np.int64)
    src[P0_Q:P0_Q + q_w] = np.arange(q_w)
    d = np.arange(HEAD_DIM)
    for base, kp, vp in ((P0_KVS, 2, 3), (P0_KVW, 4, 5)):
        for g in range(NSA_KV_HEADS):
            src[base + g * 128 + d] = kv0 + kp * piece + g * HEAD_DIM + d
            src[base + g * 128 + 64 + d] = kv0 + vp * piece + g * HEAD_DIM + d
    src[P0_KC:P0_KC + piece] = kv0 + 0 * piece + np.arange(piece)
    src[P0_VC:P0_VC + piece] = kv0 + 1 * piece + np.arange(piece)
    for g in range(NSA_KV_HEADS):
        for br in range(3):
            for r in range(NSA_GROUP):
                src[P0_GATE + g * 128 + br * NSA_GROUP + r] = gate0 + br * NSA_HEADS + g * NSA_GROUP + r
    src[P0_CQ:P0_CQ + MLA_Q_RANK] = cq0 + np.arange(MLA_Q_RANK)
    src[P0_CKV:P0_CKV + MLA_KV_RANK] = ckv0 + np.arange(MLA_KV_RANK)
    src[P0_KR + 64:P0_KR + 64 + MLA_ROPE_DIM] = kr0 + np.arange(MLA_ROPE_DIM)
    return src


def _permute_columns(w, src):
    cols = jnp.take(w, jnp.asarray(np.maximum(src, 0)), axis=1)
    return jnp.where(jnp.asarray(src >= 0)[None, :], cols, 0.0)


def _block_onehot(pos, block, shape):
    return jnp.where(_lane_iota(shape) == HEAD_DIM + pos // block, 1.0, 0.0)


def _proj0_kernel(x_ref, w_ref, tab_ref, qg_ref, kvg_ref, wuq_ref, wk_ref, wv_ref,
                  q_ref, ks_ref, vs_ref, kw_ref, vw_ref, kc_ref, vc_ref, gate_ref,
                  qm_ref, km_ref, vm_ref, *, seq_tiles):
    tm = x_ref.shape[0]
    xb = x_ref[...].astype(BF16)
    lane = _lane_iota((tm, LANES))
    low = lane < HEAD_DIM
    pair_first = (lane % HEAD_DIM) < (HEAD_DIM // 2)
    mla_first = lane < (MLA_NOPE_DIM + MLA_ROPE_DIM // 2)
    cos_p, sin_p = tab_ref[0], tab_ref[1]
    cos_kv, sin_kv = tab_ref[2], tab_ref[3]
    cos_m, sin_m = tab_ref[4], tab_ref[5]
    pos = (pl.program_id(0) % seq_tiles) * tm + lax.broadcasted_iota(jnp.int32, (tm, 1), 0)
    sel_onehot = _block_onehot(pos, NSA_SEL_BLOCK, (tm, LANES))
    one_lane = jnp.where(lane == HEAD_DIM, 1.0, 0.0)

    def seg(c0, width):
        return jnp.dot(xb, w_ref[:, c0:c0 + width], preferred_element_type=F32)

    q = seg(P0_Q, 512)
    for j in range(4):
        blk = _rope_lanes(q[:, j * 128:(j + 1) * 128], cos_p, sin_p, pair_first, 32)
        q_ref[:, j * 128:(j + 1) * 128] = (blk * (HEAD_DIM ** -0.5)).astype(BF16)
    for c0, k_out, v_out, extra in ((P0_KVS, ks_ref, vs_ref, sel_onehot), (P0_KVW, kw_ref, vw_ref, 0.0)):
        kv = seg(c0, 256)
        for j in range(2):
            blk = _rope_lanes(kv[:, j * 128:(j + 1) * 128], cos_kv, sin_kv, pair_first, 32)
            k_out[:, j * 128:(j + 1) * 128] = jnp.where(low, blk, extra).astype(BF16)
            v_out[:, j * 128:(j + 1) * 128] = jnp.where(low, pltpu.roll(blk, HEAD_DIM, 1), one_lane).astype(BF16)
    kc_ref[...] = seg(P0_KC, 128)
    vc_ref[...] = seg(P0_VC, 128)
    gate_ref[...] = seg(P0_GATE, 256)

    cq = _rms_norm(seg(P0_CQ, 256), qg_ref[...])
    qm = _dot(cq, wuq_ref[...])
    mla_scale = (MLA_NOPE_DIM + MLA_ROPE_DIM) ** -0.5
    for h in range(MLA_HEADS):
        blk = _rope_lanes(qm[:, h * 128:(h + 1) * 128], cos_m, sin_m, mla_first, 16)
        qm_ref[:, h * 128:(h + 1) * 128] = (blk * mla_scale).astype(BF16)
    ckv = _rms_norm(seg(P0_CKV, 128), kvg_ref[...]).astype(BF16)
    kn = jnp.dot(ckv, wk_ref[...], preferred_element_type=F32)
    kpe = _rope_lanes(seg(P0_KR, 128), cos_m, sin_m, mla_first, 16)
    for h in range(MLA_HEADS):
        km_ref[:, h * 128:(h + 1) * 128] = (kn[:, h * 128:(h + 1) * 128] + kpe).astype(BF16)
    vm = jnp.dot(ckv, wv_ref[...], preferred_element_type=F32)
    for h in range(MLA_HEADS):
        vm_ref[:, h * 128:(h + 1) * 128] = jnp.where(low, vm[:, h * 128:(h + 1) * 128], one_lane).astype(BF16)


def _proj0(x2, w_perm, tabs, q_norm, kv_norm, wuq, wk, wv, S):
    T = x2.shape[0]
    tm = ROW_TILE
    ns = S // tm
    row = lambda w: pl.BlockSpec((tm, w), lambda i: (i, 0))
    full = lambda a: pl.BlockSpec(a.shape, lambda i: (0,) * a.ndim)
    widths = (512, 256, 256, 256, 256, 128, 128, 256, 1024, 1024, 1024)
    dtypes = (BF16, BF16, BF16, BF16, BF16, F32, F32, F32, BF16, BF16, BF16)
    return pl.pallas_call(
        functools.partial(_proj0_kernel, seq_tiles=ns),
        grid=(T // tm,),
        in_specs=[row(D_MODEL), full(w_perm),
                  pl.BlockSpec((6, tm, LANES), lambda i: (0, i % ns, 0)),
                  full(q_norm), full(kv_norm), full(wuq), full(wk), full(wv)],
        out_specs=[row(w) for w in widths],
        out_shape=[jax.ShapeDtypeStruct((T, w), d) for w, d in zip(widths, dtypes)],
        compiler_params=_cparams("parallel"),
    )(x2, w_perm, tabs, q_norm, kv_norm, wuq, wk, wv)


def _compress_kernel(kc_ref, vc_ref, pe_ref, w1k_ref, w2k_ref, w1v_ref, w2v_ref, tab_ref,
                     ko_ref, vo_ref):
    half = NSA_CMP_STRIDE * HEAD_DIM
    n = kc_ref.shape[1]

    def mlp(x, pe_lo, pe_hi, w1_ref, w2_ref):
        first = _dot(x + pe_lo, w1_ref[0:half, :])
        second = _dot(x + pe_hi, w1_ref[half:2 * half, :])
        hidden = first + pltpu.roll(second, n - 1, 0)
        return _dot(jax.nn.gelu(hidden), w2_ref[...])

    k = mlp(kc_ref[0], pe_ref[0:1, :], pe_ref[1:2, :], w1k_ref, w2k_ref)
    lane = _lane_iota(k.shape)
    ko_ref[0] = _rope_lanes(k, tab_ref[0], tab_ref[1], lane < HEAD_DIM // 2, 32).astype(BF16)
    vo_ref[0] = mlp(vc_ref[0], pe_ref[2:3, :], pe_ref[3:4, :], w1v_ref, w2v_ref).astype(BF16)


def _compress(kc_chunks, vc_chunks, pe, w1k, w2k, w1v, w2v, ctab):
    n_bg, n, width = kc_chunks.shape
    blk = pl.BlockSpec((1, n, width), lambda i: (i, 0, 0))
    full = lambda a: pl.BlockSpec(a.shape, lambda i: (0,) * a.ndim)
    out = pl.BlockSpec((1, n, LANES), lambda i: (i, 0, 0))
    return pl.pallas_call(
        _compress_kernel,
        grid=(n_bg,),
        in_specs=[blk, blk, full(pe), full(w1k), full(w2k), full(w1v), full(w2v), full(ctab)],
        out_specs=[out, out],
        out_shape=[jax.ShapeDtypeStruct((n_bg, n, LANES), BF16)] * 2,
        compiler_params=_cparams("parallel"),
    )(kc_chunks, vc_chunks, pe, w1k, w2k, w1v, w2v, ctab)


def _cmp_attn_kernel(q_ref, k_ref, v_ref, gate_ref, c2s_ref, o_ref, qa_ref, *, n_sel, n_top):
    qi = pl.program_id(2)
    tq = q_ref.shape[1]
    n = k_ref.shape[1]
    q = q_ref[0]
    k = k_ref[0][:, 0:HEAD_DIM]
    v = v_ref[0][:, 0:HEAD_DIM]
    gates = jax.nn.sigmoid(gate_ref[0])
    pos = qi * tq + lax.broadcasted_iota(jnp.int32, (tq, 1), 0)
    cmp_end = lax.broadcasted_iota(jnp.int32, (1, n), 1) * NSA_CMP_STRIDE + (NSA_CMP_LEN - 1)
    visible = cmp_end <= pos
    p_sum = jnp.zeros((tq, n), F32)
    outs = []
    for r in range(NSA_GROUP):
        s = jnp.where(visible, _dot_nt(q[:, r * HEAD_DIM:(r + 1) * HEAD_DIM], k), NEG_INF)
        m = jnp.max(s, axis=-1, keepdims=True)
        e = jnp.exp(s - jnp.where(m > NEG_INF, m, 0.0))
        den = jnp.sum(e, axis=-1, keepdims=True)
        p = e / jnp.where(den > 0, den, 1.0)
        p_sum = p_sum + p
        outs.append(_dot(p, v) * gates[:, r:r + 1])
    o_ref[0] = jnp.concatenate(outs, axis=-1)

    imp = _dot_split_lhs(p_sum, c2s_ref[...])
    blk = _lane_iota(imp.shape)
    cur = pos // NSA_SEL_BLOCK
    forced = (blk == 0) | (blk == cur) | (blk == cur - 1)
    valid = (blk <= cur) & (blk < n_sel)
    free = jnp.where(valid & ~forced, imp, NEG_INF)
    taken = forced | _take_top(free, blk, n_top - 3)
    bias = jnp.where(blk < n_sel, jnp.where(taken & valid, 0.0, -1e30), 0.0)
    bias = pltpu.roll(bias, HEAD_DIM, 1)
    low = blk < HEAD_DIM
    qf = q.astype(F32)
    for r in range(NSA_GROUP):
        x = qf[:, (r // 2) * LANES:(r // 2 + 1) * LANES]
        if r % 2 == 1:
            x = pltpu.roll(x, HEAD_DIM, 1)
        qa_ref[0, :, r * LANES:(r + 1) * LANES] = jnp.where(low, x, bias).astype(BF16)


def _cmp_attn(q3, kcmp, vcmp, gate3, c2s, n_sel):
    B, S, _ = q3.shape
    n = kcmp.shape[1]
    tq = SELECT_TILE
    G = NSA_KV_HEADS
    assert n_sel >= 3 and NSA_SEL_TOPN >= 3
    kern = functools.partial(_cmp_attn_kernel, n_sel=n_sel, n_top=min(NSA_SEL_TOPN, n_sel))
    return pl.pallas_call(
        kern,
        grid=(B, G, S // tq),
        in_specs=[pl.BlockSpec((1, tq, 256), lambda b, g, i: (b, i, g)),
                  pl.BlockSpec((1, n, LANES), lambda b, g, i: (b * G + g, 0, 0)),
                  pl.BlockSpec((1, n, LANES), lambda b, g, i: (b * G + g, 0, 0)),
                  pl.BlockSpec((1, tq, LANES), lambda b, g, i: (b, i, g)),
                  pl.BlockSpec(c2s.shape, lambda b, g, i: (0, 0))],
        out_specs=[pl.BlockSpec((1, tq, 256), lambda b, g, i: (b, i, g)),
                   pl.BlockSpec((1, tq, 512), lambda b, g, i: (b, i, g))],
        out_shape=[jax.ShapeDtypeStruct((B, S, 512), F32),
                   jax.ShapeDtypeStruct((B, S, NSA_HEADS * LANES), BF16)],
        compiler_params=_cparams("parallel", "parallel", "parallel"),
    )(q3, kcmp, vcmp, gate3, c2s)


def _flash_kernel(*refs, heads, window, gate_cols):
    it = iter(refs)
    q_ref, k_ref, v_ref = next(it), next(it), next(it)
    gate_ref = next(it) if gate_cols is not None else None
    o_ref = next(it)
    s_sc, mx_sc, mb_sc, acc_sc = next(it), next(it), next(it), next(it)

    qi = pl.program_id(2)
    tq = q_ref.shape[1]
    tk = tq
    nh = len(heads)

    lane = _lane_iota((tq, LANES))
    low = lane < HEAD_DIM
    q_heads = [q_ref[0, :, g * LANES:(g + 1) * LANES] for (g, _, _) in heads]
    mx_sc[...] = jnp.full(mx_sc.shape, NEG_INF, F32)
    acc_sc[...] = jnp.zeros(acc_sc.shape, F32)

    def score_tile(j, pos_mask):
        start = pl.multiple_of(j * tk, tk)
        k = k_ref[0, pl.ds(start, tk), :]
        for h, (_, kg, _) in enumerate(heads):
            s = _dot_nt(q_heads[h], k[:, kg * LANES:(kg + 1) * LANES])
            if pos_mask is not None:
                s = jnp.where(pos_mask, s, NEG_INF)
            s_sc[h, j] = s
            best = s[:, 0:LANES]
            for c in range(1, tk // LANES):
                best = jnp.maximum(best, s[:, c * LANES:(c + 1) * LANES])
            mx_sc[h] = jnp.maximum(mx_sc[h], best)

    def value_tile(j):
        start = pl.multiple_of(j * tk, tk)
        v = v_ref[0, pl.ds(start, tk), :]
        for h, (_, _, vg) in enumerate(heads):
            mb = mb_sc[h]
            s = s_sc[h, j]
            p = jnp.concatenate([jnp.exp(s[:, c * LANES:(c + 1) * LANES] - mb)
                                 for c in range(tk // LANES)], axis=-1).astype(BF16)
            acc_sc[h] += jnp.dot(p, v[:, vg * LANES:(vg + 1) * LANES], preferred_element_type=F32)

    def for_each_tile(fn_full, fn_masked):
        row = lax.broadcasted_iota(jnp.int32, (tq, tk), 0)
        col = lax.broadcasted_iota(jnp.int32, (tq, tk), 1)
        first_full = 0
        if window is not None:
            back = window // tk
            first_full = jnp.maximum(qi - back + 1, 0)

            @pl.when(qi >= back)
            def _():
                fn_masked(qi - back, col > row)

        def body(j, carry):
            fn_full(j)
            return carry

        lax.fori_loop(first_full, qi, body, 0)
        fn_masked(qi, col <= row)

    for_each_tile(lambda j: score_tile(j, None), score_tile)
    for h in range(nh):
        m = jnp.max(mx_sc[h], axis=-1, keepdims=True)
        mb_sc[h] = jnp.broadcast_to(m, (tq, LANES))
    for_each_tile(value_tile, lambda j, mask: value_tile(j))

    if gate_ref is not None:
        gates = jax.nn.sigmoid(gate_ref[0])
    results = []
    for h in range(nh):
        acc = acc_sc[h]
        o = acc / acc[:, HEAD_DIM:HEAD_DIM + 1]
        if gate_ref is not None:
            c = gate_cols[h]
            o = o * gates[:, c:c + 1]
        results.append(o)
    for pair in range(nh // 2):
        high = pltpu.roll(results[2 * pair + 1], HEAD_DIM, 1)
        o_ref[0, :, pair * LANES:(pair + 1) * LANES] = jnp.where(low, results[2 * pair], high).astype(o_ref.dtype)


def _flash(q, k, v, gate, *, n_steps, q_w, k_w, v_w, heads, window=None, gate_cols=None,
           out_dtype=F32):
    B, S, _ = q.shape
    tq = ATTN_TQ
    nh = len(heads)
    assert nh % 2 == 0 and S % tq == 0 and (window is None or window % tq == 0)
    n_stash = S // tq
    in_specs = [pl.BlockSpec((1, tq, q_w), lambda b, h, i: (b, i, h)),
                pl.BlockSpec((1, S, k_w), lambda b, h, i: (b, 0, h)),
                pl.BlockSpec((1, S, v_w), lambda b, h, i: (b, 0, h))]
    args = [q, k, v]
    if gate is not None:
        in_specs.append(pl.BlockSpec((1, tq, gate.shape[-1] // n_steps), lambda b, h, i: (b, i, h)))
        args.append(gate)
    kern = functools.partial(_flash_kernel, heads=tuple(heads), window=window, gate_cols=gate_cols)
    return pl.pallas_call(
        kern,
        grid=(B, n_steps, S // tq),
        in_specs=in_specs,
        out_specs=pl.BlockSpec((1, tq, nh * HEAD_DIM), lambda b, h, i: (b, i, h)),
        out_shape=jax.ShapeDtypeStruct((B, S, n_steps * nh * HEAD_DIM), out_dtype),
        scratch_shapes=[pltpu.VMEM((nh, n_stash, tq, tq), F32), pltpu.VMEM((nh, tq, LANES), F32),
                        pltpu.VMEM((nh, tq, LANES), F32), pltpu.VMEM((nh, tq, LANES), F32)],
        compiler_params=_cparams("parallel", "parallel", "arbitrary"),
    )(*args)


def _outproj_kernel(*refs, group_sizes):
    it = iter(refs)
    y = None
    for n_in in group_sizes:
        acts = [next(it)[...].astype(F32) for _ in range(n_in)]
        w_ref = next(it)
        a = acts[0]
        for extra in acts[1:]:
            a = a + extra
        part = _dot(a, w_ref[...])
        y = part if y is None else y + part
    x_ref, g_ref, b_ref, o_ref = next(it), next(it), next(it), next(it)
    o_ref[...] = _layer_norm(DN_ALPHA * x_ref[...] + y, g_ref[...], b_ref[...])


def _outproj_ln(groups, x2, g, b):
    T = x2.shape[0]
    tm = ROW_TILE
    in_specs, args, sizes = [], [], []
    for acts, w in groups:
        for a in acts:
            in_specs.append(pl.BlockSpec((tm, a.shape[1]), lambda i: (i, 0)))
            args.append(a)
        in_specs.append(pl.BlockSpec(w.shape, lambda i: (0, 0)))
        args.append(w)
        sizes.append(len(acts))
    in_specs += [pl.BlockSpec((tm, D_MODEL), lambda i: (i, 0)),
                 pl.BlockSpec((1, D_MODEL), lambda i: (0, 0)),
                 pl.BlockSpec((1, D_MODEL), lambda i: (0, 0))]
    args += [x2, g.reshape(1, -1), b.reshape(1, -1)]
    return pl.pallas_call(
        functools.partial(_outproj_kernel, group_sizes=tuple(sizes)),
        grid=(T // tm,),
        in_specs=in_specs,
        out_specs=pl.BlockSpec((tm, D_MODEL), lambda i: (i, 0)),
        out_shape=jax.ShapeDtypeStruct((T, D_MODEL), F32),
        compiler_params=_cparams("parallel"),
    )(*args)


def _router_kernel(x_ref, w_ref, bias_ref, g_ref):
    logits = _dot_split_both(x_ref[...], w_ref[...])
    lane = _lane_iota(logits.shape)
    real = lane < N_EXPERTS
    scores = jax.nn.sigmoid(logits)
    choice = jnp.where(real, scores + bias_ref[...], NEG_INF)
    top2 = jnp.where(_group_rank(choice, GROUP_SIZE) < 2, choice, 0.0)
    grp_score = jnp.where(real, _group_sum(top2, GROUP_SIZE), NEG_INF)
    grp_taken = _take_top(grp_score, lane // GROUP_SIZE, TOPK_GROUPS)
    masked = jnp.where(grp_taken & real, choice, NEG_INF)
    taken = _take_top(masked, lane, TOP_K)
    w = jnp.where(taken, scores, 0.0)
    w = w / jnp.sum(w, axis=-1, keepdims=True) * ROUTED_SCALE
    g_ref[...] = jnp.where(lane == N_EXPERTS, 1.0, w)


def _router(x2, w_router_pad, bias_pad):
    T = x2.shape[0]
    tm = SELECT_TILE
    return pl.pallas_call(
        _router_kernel,
        grid=(T // tm,),
        in_specs=[pl.BlockSpec((tm, D_MODEL), lambda i: (i, 0)),
                  pl.BlockSpec(w_router_pad.shape, lambda i: (0, 0)),
                  pl.BlockSpec((1, LANES), lambda i: (0, 0))],
        out_specs=pl.BlockSpec((tm, LANES), lambda i: (i, 0)),
        out_shape=jax.ShapeDtypeStruct((T, LANES), F32),
        compiler_params=_cparams("parallel"),
    )(x2, w_router_pad, bias_pad)


def _moe_kernel(x_ref, gates_ref, wg_ref, wu_ref, wd_ref, g_ref, b_ref, o_ref, xb_sc, acc_sc):
    step = pl.program_id(1)
    last = pl.num_programs(1) - 1
    per = wg_ref.shape[0]

    @pl.when(step == 0)
    def _():
        xb_sc[...] = x_ref[...].astype(BF16)
        acc_sc[...] = jnp.zeros(acc_sc.shape, F32)

    xb = xb_sc[...]
    gates = gates_ref[...]
    lane = _lane_iota(gates.shape)
    hidden = []
    for j in range(per):
        hg = jnp.dot(xb, wg_ref[j], preferred_element_type=F32)
        hu = jnp.dot(xb, wu_ref[j], preferred_element_type=F32)
        col = jnp.sum(jnp.where(lane == step * per + j, gates, 0.0), axis=-1, keepdims=True)
        hidden.append((jax.nn.silu(hg) * hu * col).astype(BF16))
    h = jnp.concatenate(hidden, axis=-1)
    wd = wd_ref[...].reshape(per * EXPERT_FF, D_MODEL)
    acc_sc[...] += jnp.dot(h, wd, preferred_element_type=F32)

    @pl.when(step == last)
    def _():
        o_ref[...] = _layer_norm(DN_ALPHA * x_ref[...] + acc_sc[...], g_ref[...], b_ref[...])


def _moe_ln(x2, gates, wg, wu, wd, g, b):
    T = x2.shape[0]
    tm = MOE_TILE
    per = MOE_EXPERTS_PER_STEP
    n_steps = wg.shape[0] // per
    assert n_steps * per == wg.shape[0]
    return pl.pallas_call(
        _moe_kernel,
        grid=(T // tm, n_steps),
        in_specs=[pl.BlockSpec((tm, D_MODEL), lambda i, e: (i, 0)),
                  pl.BlockSpec((tm, LANES), lambda i, e: (i, 0)),
                  pl.BlockSpec((per, D_MODEL, EXPERT_FF), lambda i, e: (e, 0, 0)),
                  pl.BlockSpec((per, D_MODEL, EXPERT_FF), lambda i, e: (e, 0, 0)),
                  pl.BlockSpec((per, EXPERT_FF, D_MODEL), lambda i, e: (e, 0, 0)),
                  pl.BlockSpec((1, D_MODEL), lambda i, e: (0, 0)),
                  pl.BlockSpec((1, D_MODEL), lambda i, e: (0, 0))],
        out_specs=pl.BlockSpec((tm, D_MODEL), lambda i, e: (i, 0)),
        out_shape=jax.ShapeDtypeStruct((T, D_MODEL), F32),
        scratch_shapes=[pltpu.VMEM((tm, D_MODEL), BF16), pltpu.VMEM((tm, D_MODEL), F32)],
        compiler_params=_cparams("parallel", "arbitrary"),
    )(x2, gates, wg, wu, wd, g.reshape(1, -1), b.reshape(1, -1))


def _moe_block(x2, router, router_bias, exp_gate, exp_up, exp_down, sh_gate, sh_up, sh_down, g, b):
    w_router_pad = jnp.pad(router, ((0, 0), (0, LANES - N_EXPERTS)))
    bias_pad = jnp.pad(router_bias, (0, LANES - N_EXPERTS)).reshape(1, LANES)
    gates = _router(x2, w_router_pad, bias_pad)
    def stack(routed, shared):
        out = jnp.zeros((N_EXPERTS + 1,) + routed.shape[1:], BF16)
        out = lax.dynamic_update_slice(out, routed.astype(BF16), (0, 0, 0))
        return lax.dynamic_update_slice(out, shared.astype(BF16)[None], (N_EXPERTS, 0, 0))

    return _moe_ln(x2, gates, stack(exp_gate, sh_gate), stack(exp_up, sh_up),
                   stack(exp_down, sh_down), g, b)


def _qkv1_kernel(x_ref, w_ref, tab_ref, q_ref, k_ref, v_ref, kmean_ref, *, seq_tiles):
    tm = x_ref.shape[0]
    xb = x_ref[...].astype(BF16)
    width = MOBA_HEADS * HEAD_DIM
    lane = _lane_iota((tm, LANES))
    low = lane < HEAD_DIM
    pair_first = (lane % HEAD_DIM) < (HEAD_DIM // 2)
    cos_p, sin_p = tab_ref[0], tab_ref[1]
    pos = (pl.program_id(0) % seq_tiles) * tm + lax.broadcasted_iota(jnp.int32, (tm, 1), 0)
    onehot = _block_onehot(pos, MOBA_BLOCK, (tm, LANES))
    one_lane = jnp.where(lane == HEAD_DIM, 1.0, 0.0)
    wide = 2 * LANES
    for j in range(width // wide):
        q4 = jnp.dot(xb, w_ref[:, j * wide:(j + 1) * wide], preferred_element_type=F32)
        k4 = jnp.dot(xb, w_ref[:, width + j * wide:width + (j + 1) * wide], preferred_element_type=F32)
        v4 = jnp.dot(xb, w_ref[:, 2 * width + j * wide:2 * width + (j + 1) * wide],
                     preferred_element_type=F32)
        for half in range(2):
            c = j * wide + half * LANES
            part = slice(half * LANES, (half + 1) * LANES)
            q = _rope_lanes(q4[:, part], cos_p, sin_p, pair_first, 32)
            q_ref[:, c:c + LANES] = (q * (HEAD_DIM ** -0.5)).astype(BF16)
            k = _rope_lanes(k4[:, part], cos_p, sin_p, pair_first, 32)
            k_ref[:, 2 * c:2 * c + LANES] = jnp.where(low, k, onehot).astype(BF16)
            k_ref[:, 2 * c + LANES:2 * c + 2 * LANES] = jnp.where(
                low, pltpu.roll(k, HEAD_DIM, 1), onehot).astype(BF16)
            for blk in range(tm // MOBA_BLOCK):
                kmean_ref[blk, :, c:c + LANES] = jnp.mean(
                    k[blk * MOBA_BLOCK:(blk + 1) * MOBA_BLOCK], axis=0, keepdims=True)
            v = v4[:, part]
            v_ref[:, 2 * c:2 * c + LANES] = jnp.where(low, v, one_lane).astype(BF16)
            v_ref[:, 2 * c + LANES:2 * c + 2 * LANES] = jnp.where(
                low, pltpu.roll(v, HEAD_DIM, 1), one_lane).astype(BF16)


def _qkv1(x2, w_qkv, tabs, S):
    T = x2.shape[0]
    tm = ROW_TILE
    assert tm % MOBA_BLOCK == 0
    ns = S // tm
    width = MOBA_HEADS * HEAD_DIM
    row = lambda w: pl.BlockSpec((tm, w), lambda i: (i, 0))
    return pl.pallas_call(
        functools.partial(_qkv1_kernel, seq_tiles=ns),
        grid=(T // tm,),
        in_specs=[pl.BlockSpec((tm, D_MODEL), lambda i: (i, 0)),
                  pl.BlockSpec(w_qkv.shape, lambda i: (0, 0)),
                  pl.BlockSpec((2, tm, LANES), lambda i: (0, i % ns, 0))],
        out_specs=[row(width), row(2 * width), row(2 * width),
                   pl.BlockSpec((tm // MOBA_BLOCK, 1, width), lambda i: (i, 0, 0))],
        out_shape=[jax.ShapeDtypeStruct((T, width), BF16), jax.ShapeDtypeStruct((T, 2 * width), BF16),
                   jax.ShapeDtypeStruct((T, 2 * width), BF16),
                   jax.ShapeDtypeStruct((T // MOBA_BLOCK, 1, width), F32)],
        compiler_params=_cparams("parallel"),
    )(x2, w_qkv, tabs)


def _moba_select_kernel(q_ref, km_ref, qa_ref, *, n_top, n_blocks):
    i = pl.program_id(1)
    tq = q_ref.shape[1]
    gate = _dot_split_lhs_rhs(q_ref[0], km_ref[0])
    lane = _lane_iota(gate.shape)
    blk = lane % 8
    own = (i * tq + lax.broadcasted_iota(jnp.int32, (tq, 1), 0)) // MOBA_BLOCK
    score = jnp.where((blk < own) & (blk < n_blocks), gate, NEG_INF)
    rank = _group_rank(score, 8)
    chosen = ((score > NEG_INF) & (rank < n_top)) | (blk == own)
    bias = jnp.where(chosen, 0.0, -1e30)
    low = lane < HEAD_DIM
    in_bias = (lane >= HEAD_DIM) & (lane < HEAD_DIM + 8)
    for pair in range(MOBA_HEADS // 2):
        qf = q_ref[0, :, pair * LANES:(pair + 1) * LANES].astype(F32)
        for half in range(2):
            h = 2 * pair + half
            x = qf if half == 0 else pltpu.roll(qf, HEAD_DIM, 1)
            b = jnp.where(in_bias, pltpu.roll(bias, (HEAD_DIM - 8 * h) % LANES, 1), 0.0)
            qa_ref[0, :, h * LANES:(h + 1) * LANES] = jnp.where(low, x, b).astype(BF16)


def _dot_split_lhs_rhs(q_bf16, b):
    hi, lo = _split(b)
    return (jnp.dot(q_bf16, hi, preferred_element_type=F32)
            + jnp.dot(q_bf16, lo, preferred_element_type=F32))


def _moba_select(q3, km, n_top, n_blocks):
    B, S, width = q3.shape
    tq = SELECT_TILE
    kern = functools.partial(_moba_select_kernel, n_top=n_top, n_blocks=n_blocks)
    return pl.pallas_call(
        kern,
        grid=(B, S // tq),
        in_specs=[pl.BlockSpec((1, tq, width), lambda b, i: (b, i, 0)),
                  pl.BlockSpec((1, width, LANES), lambda b, i: (b, 0, 0))],
        out_specs=pl.BlockSpec((1, tq, MOBA_HEADS * LANES), lambda b, i: (b, i, 0)),
        out_shape=jax.ShapeDtypeStruct((B, S, MOBA_HEADS * LANES), BF16),
        compiler_params=_cparams("parallel", "parallel"),
    )(q3, km)


def _layer0(x2, B, S, w_in, pe_k, pe_v, cmp_k1, cmp_k2, cmp_v1, cmp_v2,
            q_norm, w_uq, kv_norm, w_ukv, w_out, ln_g, ln_b):
    T = B * S
    G = NSA_KV_HEADS
    w_perm = _permute_columns(w_in, _proj0_columns()).astype(BF16)
    tabs = _rope_tables(S)
    qd = MLA_NOPE_DIM + MLA_ROPE_DIM
    wuq = jnp.pad(w_uq.reshape(MLA_Q_RANK, MLA_HEADS, qd), ((0, 0), (0, 0), (0, LANES - qd)))
    wuq = wuq.reshape(MLA_Q_RANK, MLA_HEADS * LANES).astype(BF16)
    wukv = w_ukv.reshape(MLA_KV_RANK, MLA_HEADS, MLA_NOPE_DIM + MLA_V_DIM)
    wk = jnp.pad(wukv[:, :, :MLA_NOPE_DIM], ((0, 0), (0, 0), (0, LANES - MLA_NOPE_DIM)))
    wk = wk.reshape(MLA_KV_RANK, MLA_HEADS * LANES).astype(BF16)
    wv = jnp.pad(wukv[:, :, MLA_NOPE_DIM:], ((0, 0), (0, 0), (0, LANES - MLA_V_DIM)))
    wv = wv.reshape(MLA_KV_RANK, MLA_HEADS * LANES).astype(BF16)

    q, ks, vs, kw, vw, kc, vc, gate, qm, km, vm = _proj0(
        x2, w_perm, tabs, q_norm.reshape(1, -1), kv_norm.reshape(1, -1), wuq, wk, wv, S)

    n_chunks = S // NSA_CMP_STRIDE
    chunk_w = NSA_CMP_STRIDE * HEAD_DIM

    def chunks(t):
        t = t.reshape(B, S, G, HEAD_DIM).transpose(0, 2, 1, 3)
        return t.reshape(B * G, n_chunks, chunk_w)

    pe = jnp.stack([pe_k[:NSA_CMP_STRIDE].reshape(-1), pe_k[NSA_CMP_STRIDE:].reshape(-1),
                    pe_v[:NSA_CMP_STRIDE].reshape(-1), pe_v[NSA_CMP_STRIDE:].reshape(-1)])
    pad2 = lambda w: jnp.pad(w, ((0, 0), (0, LANES - HEAD_DIM))).astype(BF16)
    kcmp, vcmp = _compress(chunks(kc), chunks(vc), pe, cmp_k1.astype(BF16), pad2(cmp_k2),
                           cmp_v1.astype(BF16), pad2(cmp_v2), _cmp_rope_tables(n_chunks))

    n_sel = S // NSA_SEL_BLOCK
    n_cmp = (S - NSA_CMP_LEN) // NSA_CMP_STRIDE + 1
    tok = np.arange(n_chunks)[:, None] * NSA_CMP_STRIDE + np.arange(NSA_CMP_LEN)[None, :]
    c2s = (tok[:, :, None] // NSA_SEL_BLOCK == np.arange(LANES)[None, None, :]).sum(1) / NSA_CMP_LEN
    c2s[n_cmp:] = 0.0
    c2s = jnp.asarray(c2s, dtype=BF16)

    q3 = q.reshape(B, S, -1)
    gate3 = gate.reshape(B, S, -1)
    o_cmp, q_aug = _cmp_attn(q3, kcmp, vcmp, gate3, c2s, n_sel)

    r3 = lambda t: t.reshape(B, S, -1)
    nsa_heads = [(r, r // NSA_GROUP, r // NSA_GROUP) for r in range(NSA_HEADS)]
    gate_col = lambda branch, r: (r // NSA_GROUP) * LANES + branch * NSA_GROUP + r % NSA_GROUP
    o_sel = _flash(q_aug, r3(ks), r3(vs), gate3, n_steps=1, q_w=NSA_HEADS * LANES, k_w=G * LANES,
                   v_w=G * LANES, heads=nsa_heads, gate_cols=[gate_col(1, r) for r in range(NSA_HEADS)])
    o_win = _flash(q_aug, r3(kw), r3(vw), gate3, n_steps=1, q_w=NSA_HEADS * LANES, k_w=G * LANES,
                   v_w=G * LANES, heads=nsa_heads, window=NSA_WINDOW,
                   gate_cols=[gate_col(2, r) for r in range(NSA_HEADS)])
    o_mla = _flash(r3(qm), r3(km), r3(vm), None, n_steps=MLA_HEADS // FLASH_HEADS,
                   q_w=FLASH_HEADS * LANES, k_w=FLASH_HEADS * LANES, v_w=FLASH_HEADS * LANES,
                   heads=_OWN_KV_HEADS, out_dtype=BF16)

    n_nsa = NSA_HEADS * HEAD_DIM
    w_out_b = w_out.astype(BF16)
    groups = [([o_cmp.reshape(T, -1), o_sel.reshape(T, -1), o_win.reshape(T, -1)], w_out_b[:n_nsa]),
              ([o_mla.reshape(T, -1)], w_out_b[n_nsa:])]
    return _outproj_ln(groups, x2, ln_g, ln_b)


def _layer1(x2, B, S, w_qkv, w_out, ln_g, ln_b):
    T = B * S
    width = MOBA_HEADS * HEAD_DIM
    tabs = _rope_tables(S)[0:2]
    q, k, v, kmean = _qkv1(x2, w_qkv.astype(BF16), tabs, S)
    n_blocks = S // MOBA_BLOCK
    n_top = min(MOBA_TOPK, max(n_blocks - 1, 1))
    km = kmean.reshape(B, n_blocks, width).transpose(0, 2, 1)
    km = jnp.pad(km, ((0, 0), (0, 0), (0, 8 - n_blocks)))
    km = jnp.tile(km, (1, 1, MOBA_HEADS))
    diag = (np.arange(width)[:, None] // HEAD_DIM) == (np.arange(LANES)[None, :] // 8)
    km = jnp.where(jnp.asarray(diag)[None], km, 0.0)
    q3 = q.reshape(B, S, width)
    q_aug = _moba_select(q3, km, n_top, n_blocks)
    o = _flash(q_aug, k.reshape(B, S, 2 * width), v.reshape(B, S, 2 * width), None,
               n_steps=MOBA_HEADS // FLASH_HEADS, q_w=FLASH_HEADS * LANES, k_w=FLASH_HEADS * LANES,
               v_w=FLASH_HEADS * LANES, heads=_OWN_KV_HEADS, out_dtype=BF16)
    return _outproj_ln([([o.reshape(T, width)], w_out.astype(BF16))], x2, ln_g, ln_b)


def kernel(x, l0_w_in, l0_nsa_pe_k, l0_nsa_pe_v, l0_nsa_cmp_k1, l0_nsa_cmp_k2, l0_nsa_cmp_v1, l0_nsa_cmp_v2, l0_mla_q_norm, l0_mla_w_uq, l0_mla_kv_norm, l0_mla_w_ukv, l0_w_out, l0_ln1_g, l0_ln1_b, l0_router, l0_router_bias, l0_exp_gate, l0_exp_up, l0_exp_down, l0_sh_gate, l0_sh_up, l0_sh_down, l0_ln2_g, l0_ln2_b, l1_w_qkv, l1_w_out, l1_ln1_g, l1_ln1_b, l1_router, l1_router_bias, l1_exp_gate, l1_exp_up, l1_exp_down, l1_sh_gate, l1_sh_up, l1_sh_down, l1_ln2_g, l1_ln2_b):
    B, S, D = x.shape
    x2 = x.reshape(B * S, D)
    x2 = _layer0(x2, B, S, l0_w_in, l0_nsa_pe_k, l0_nsa_pe_v, l0_nsa_cmp_k1, l0_nsa_cmp_k2,
                 l0_nsa_cmp_v1, l0_nsa_cmp_v2, l0_mla_q_norm, l0_mla_w_uq, l0_mla_kv_norm,
                 l0_mla_w_ukv, l0_w_out, l0_ln1_g, l0_ln1_b)
    x2 = _moe_block(x2, l0_router, l0_router_bias, l0_exp_gate, l0_exp_up, l0_exp_down,
                    l0_sh_gate, l0_sh_up, l0_sh_down, l0_ln2_g, l0_ln2_b)
    x2 = _layer1(x2, B, S, l1_w_qkv, l1_w_out, l1_ln1_g, l1_ln1_b)
    x2 = _moe_block(x2, l1_router, l1_router_bias, l1_exp_gate, l1_exp_up, l1_exp_down,
                    l1_sh_gate, l1_sh_up, l1_sh_down, l1_ln2_g, l1_ln2_b)
    return x2.reshape(B, S, D)
```
